```python
import jax
import jax.numpy as jnp
from jax import lax
import numpy as np

D_MODEL = 2048
BATCH = 16
SEQ = 256
DEPTH = 4
DEC_BATCH = 8
DEC_SEQ = 1024
PAST_LEN = 256

GRID_W = 64
D_A = D_MODEL // 2
A_HEADS = 8
A_BLOCK = D_A // A_HEADS
CONV_W = 4
RG_C = 8.0
D_B = D_MODEL // 2
B_HEAD_DIM = 128
B_HEADS = D_B // B_HEAD_DIM
CHUNK = 64
D_C = D_MODEL // 2
HEAD_DIM = 128
Q_HEADS = D_C // HEAD_DIM
KV_HEADS = 2
GROUP = Q_HEADS // KV_HEADS
KV_DIM = KV_HEADS * HEAD_DIM
ROPE_FREQS = HEAD_DIM // 4
ROPE_BASE = 10000.0
Q_BLOCK = 128
N_EXPERTS = 16
EC_FACTOR = 2
D_EXPERT = D_MODEL // 2
N_SUB = 6
SPLIT_SIZES = (D_A, D_A, D_B, D_B, D_B, D_B, D_B, D_C, KV_DIM, KV_DIM, D_MODEL, D_MODEL, D_MODEL)
N_IN = 2 * D_A + 5 * D_B + D_C + 2 * KV_DIM + 3 * D_MODEL
EPS = 1e-6

kernel_name = 'hybrid_dit_rglru_hgrn2_gqa_ec_step'


def rms_norm(x, g):
    xf = x.astype(jnp.float32)
    y = xf * lax.rsqrt(jnp.mean(xf * xf, axis=-1, keepdims=True) + EPS)
    return (y * g.astype(jnp.float32)).astype(x.dtype)


def centred_depthwise_conv(x, w, b):
    n = x.shape[1]
    xp = jnp.pad(x, ((0, 0), ((CONV_W - 1) // 2, CONV_W // 2), (0, 0)))
    return sum(xp[:, j:j + n] * w[j] for j in range(CONV_W)) + b


def linear_scan(a, b, h0):
    b = b.at[:, 0].add(a[:, 0] * h0.astype(b.dtype))

    def combine(left, right):
        a1, b1 = left
        a2, b2 = right
        return a1 * a2, a2 * b1 + b2

    _, h = lax.associative_scan(combine, (a, b), axis=1)
    return h


def rglru_direction(x, wr, br, wi, bi, lam, h0):
    bsz, n, _ = x.shape
    xh = x.reshape(bsz, n, A_HEADS, A_BLOCK)
    r = jax.nn.sigmoid(jnp.einsum('blhi,hij->blhj', xh, wr).reshape(bsz, n, D_A) + br)
    i = jax.nn.sigmoid(jnp.einsum('blhi,hij->blhj', xh, wi).reshape(bsz, n, D_A) + bi)
    log_a = -RG_C * r * jax.nn.softplus(-lam)
    a = jnp.exp(log_a)
    h = linear_scan(a, jnp.sqrt(-jnp.expm1(2.0 * log_a)) * (i * x), h0)
    return h, h[:, -1]


def hgrn2_chunked(q, k, v, logf, s0):
    bsz, n, nh, dk = q.shape
    dv = v.shape[-1]
    nc = n // CHUNK

    def blocks(t):
        return t.reshape(bsz, nc, CHUNK, nh, t.shape[-1]).transpose(1, 0, 3, 2, 4)

    lower = jnp.tril(jnp.ones((CHUNK, CHUNK), dtype=bool))[:, :, None]

    def step(S, inp):
        qc, kc, vc, gc = inp
        bcum = jnp.cumsum(gc.astype(jnp.float32), axis=2)
        diff = bcum[:, :, :, None, :] - bcum[:, :, None, :, :]
        dec = jnp.exp(jnp.where(lower, diff, -jnp.inf)).astype(qc.dtype)
        scores = jnp.einsum('bhtk,bhsk,bhtsk->bhts', qc, kc, dec)
        o = (jnp.einsum('bhts,bhsv->bhtv', scores, vc)
             + jnp.einsum('bhtk,bhkv->bhtv', qc * jnp.exp(bcum).astype(qc.dtype), S))
        b_end = bcum[:, :, -1:, :]
        S_new = (jnp.exp(b_end[:, :, 0, :, None]).astype(S.dtype) * S
                 + jnp.einsum('bhsk,bhsv->bhkv', kc * jnp.exp(b_end - bcum).astype(kc.dtype), vc))
        return S_new.astype(S.dtype), o

    s_fin, o = lax.scan(step, s0.astype(q.dtype), (blocks(q), blocks(k), blocks(v), blocks(logf)))
    o = o.transpose(1, 0, 3, 2, 4).reshape(bsz, n, nh, dv)
    return o, s_fin


def axial_rope(n):
    rows = n // GRID_W
    r = jnp.repeat(jnp.arange(rows, dtype=jnp.float32), GRID_W)
    col = jnp.tile(jnp.arange(GRID_W, dtype=jnp.float32), rows)
    inv = ROPE_BASE ** (-jnp.arange(ROPE_FREQS, dtype=jnp.float32) / ROPE_FREQS)
    ar = r[:, None] * inv
    ac = col[:, None] * inv
    ang = jnp.concatenate([ar, ar, ac, ac], axis=-1)
    return jnp.cos(ang), jnp.sin(ang)


def apply_axial_rope(x, cos, sin):
    xs = x.reshape(*x.shape[:-1], 2, 2, ROPE_FREQS)
    rot = jnp.stack([-xs[..., 1, :], xs[..., 0, :]], axis=-2).reshape(x.shape)
    return x * cos[None, :, None, :].astype(x.dtype) + rot * sin[None, :, None, :].astype(x.dtype)


def attend(q, k, v):
    bsz, nq = q.shape[:2]
    nb = nq // Q_BLOCK
    qb = q.reshape(bsz, nb, Q_BLOCK, KV_HEADS, GROUP, HEAD_DIM).transpose(1, 0, 2, 3, 4, 5)
    scale = HEAD_DIM ** -0.5

    def one_block(qblk):
        s = jnp.einsum('bqhgd,bkhd->bhgqk', qblk, k).astype(jnp.float32) * scale
        pr = jax.nn.softmax(s, axis=-1).astype(v.dtype)
        return jnp.einsum('bhgqk,bkhd->bqhgd', pr, v)

    o = lax.map(one_block, qb)
    return o.transpose(1, 0, 2, 3, 4, 5).reshape(bsz, nq, Q_HEADS * HEAD_DIM)


def token_mixers(u, p, h0, s0, ctx_k, ctx_v, rope):
    bsz, n, _ = u.shape
    offsets = [int(o) for o in np.cumsum(SPLIT_SIZES)[:-1]]
    (xa, ya, qb, fz_f, fz_b, ib, gb, qc, kc, vc, gate_a, gate_b, gate_c) = jnp.split(u @ p['w_in'], offsets, axis=-1)
    flip = lambda t: t[:, ::-1]

    xa = centred_depthwise_conv(xa, p['conv_w'], p['conv_b'])
    ha_f, fin_a_f = rglru_direction(xa, p['rg_wr'][0], p['rg_br'][0], p['rg_wi'][0], p['rg_bi'][0], p['rg_lambda'][0], h0[:, 0])
    ha_b, fin_a_b = rglru_direction(flip(xa), p['rg_wr'][1], p['rg_br'][1], p['rg_wi'][1], p['rg_bi'][1], p['rg_lambda'][1], h0[:, 1])
    y_a = (ha_f + flip(ha_b)) * jax.nn.gelu(ya)

    heads = lambda t: t.reshape(bsz, n, B_HEADS, B_HEAD_DIM)
    qh = heads(jax.nn.silu(qb))
    vh = heads(ib)
    lb = p['lb']
    f_f = lb[0] + (1.0 - lb[0]) * jax.nn.sigmoid(fz_f)
    f_b = lb[1] + (1.0 - lb[1]) * jax.nn.sigmoid(fz_b)
    ob_f, sb_f = hgrn2_chunked(qh, heads(1.0 - f_f), vh, heads(jnp.log(f_f)), s0[:, 0])
    ob_b, sb_b = hgrn2_chunked(flip(qh), flip(heads(1.0 - f_b)), flip(vh), flip(heads(jnp.log(f_b))), s0[:, 1])
    y_b = rms_norm(ob_f + flip(ob_b), p['hgrn_norm_g']).reshape(bsz, n, D_B) * jax.nn.silu(gb)

    qh_c = rms_norm(qc.reshape(bsz, n, Q_HEADS, HEAD_DIM), p['q_norm_g'])
    kh_c = rms_norm(kc.reshape(bsz, n, KV_HEADS, HEAD_DIM), p['k_norm_g'])
    vh_c = vc.reshape(bsz, n, KV_HEADS, HEAD_DIM)
    if rope is None:
        y_c = attend(qh_c, kh_c, vh_c)
    else:
        cos, sin = rope
        keys = jnp.concatenate([ctx_k.astype(kh_c.dtype), apply_axial_rope(kh_c, cos, sin)], axis=1)
        vals = jnp.concatenate([ctx_v.astype(vh_c.dtype), vh_c], axis=1)
        y_c = attend(apply_axial_rope(qh_c, cos, sin), keys, vals)

    merged = (jax.nn.sigmoid(gate_a) * (y_a @ p['w_branch_a'])
              + jax.nn.sigmoid(gate_b) * (y_b @ p['w_branch_b'])
              + jax.nn.sigmoid(gate_c) * (y_c @ p['w_branch_c']))
    out = merged @ p['w_out']
    ctx_tensors = (kh_c, vh_c, jnp.stack([fin_a_f, fin_a_b], axis=1), jnp.stack([sb_f, sb_b], axis=1))
    return out, ctx_tensors


def ec_moe(u, p):
    bsz, n, _ = u.shape
    cap = EC_FACTOR * n // N_EXPERTS
    aff = jax.nn.softmax((u @ p['w_router']).astype(jnp.float32), axis=-1)
    gate, idx = lax.top_k(aff.transpose(0, 2, 1), cap)
    bidx = jnp.arange(bsz)[:, None, None]
    xin = u[bidx, idx]
    h = jax.nn.silu(jnp.einsum('becd,edf->becf', xin, p['w_exp_gate'])) * jnp.einsum('becd,edf->becf', xin, p['w_exp_up'])
    y = jnp.einsum('becf,efd->becd', h, p['w_exp_down']) * gate[..., None].astype(u.dtype)
    return jnp.zeros_like(u).at[bidx, idx].add(y)


def trunk_layer(x, cond, p, h0, s0, ctx_k, ctx_v, rope):
    mod = (jax.nn.silu(cond) @ p['w_ada'] + p['b_ada'])[:, None, :]
    sh1, sc1, g1, sh2, sc2, g2 = jnp.split(mod, N_SUB, axis=-1)
    u = rms_norm(x, p['norm1_g']) * (1.0 + sc1) + sh1
    mix, ctx_tensors = token_mixers(u, p, h0, s0, ctx_k, ctx_v, rope)
    x = x + g1 * mix
    u = rms_norm(x, p['norm2_g']) * (1.0 + sc2) + sh2
    x = x + g2 * ec_moe(u, p)
    return x, ctx_tensors


def setup_inputs(seed: int = 0) -> dict:
    key = jax.random.key(seed)
    ks = jax.random.split(key, 36)

    def nrm(i, shape, scale):
        return jax.random.normal(ks[i], shape, jnp.float32) * scale

    a_pow = jax.random.uniform(ks[19], (DEPTH, 2, D_A), jnp.float32, 0.9, 0.999)
    sig = a_pow ** (1.0 / RG_C)
    return {
        'x_prompt': nrm(0, (BATCH, SEQ, D_MODEL), 1.0),
        'x_sample': nrm(1, (DEC_BATCH, DEC_SEQ, D_MODEL), 1.0),
        'cache_k': nrm(2, (DEC_BATCH, DEPTH, PAST_LEN, KV_HEADS, HEAD_DIM), 1.0),
        'cache_v': nrm(3, (DEC_BATCH, DEPTH, PAST_LEN, KV_HEADS, HEAD_DIM), 1.0),
        'state_rglru': nrm(4, (DEC_BATCH, DEPTH, 2, D_A), 0.5),
        'state_hgrn': nrm(5, (DEC_BATCH, DEPTH, 2, B_HEADS, B_HEAD_DIM, B_HEAD_DIM), 0.3),
        'c': nrm(6, (DEC_BATCH, D_MODEL), 1.0),
        'c_ctx': nrm(7, (D_MODEL,), 1.0),
        'w_ada': nrm(8, (DEPTH, D_MODEL, N_SUB * D_MODEL), D_MODEL ** -0.5),
        'b_ada': nrm(9, (DEPTH, N_SUB * D_MODEL), 0.02),
        'norm1_g': 1.0 + nrm(10, (DEPTH, D_MODEL), 0.02),
        'norm2_g': 1.0 + nrm(11, (DEPTH, D_MODEL), 0.02),
        'w_in': nrm(12, (DEPTH, D_MODEL, N_IN), D_MODEL ** -0.5),
        'conv_w': nrm(13, (DEPTH, CONV_W, D_A), CONV_W ** -0.5),
        'conv_b': nrm(14, (DEPTH, D_A), 0.01),
        'rg_wr': nrm(15, (DEPTH, 2, A_HEADS, A_BLOCK, A_BLOCK), A_BLOCK ** -0.5),
        'rg_br': nrm(16, (DEPTH, 2, D_A), 0.01),
        'rg_wi': nrm(17, (DEPTH, 2, A_HEADS, A_BLOCK, A_BLOCK), A_BLOCK ** -0.5),
        'rg_bi': nrm(18, (DEPTH, 2, D_A), 0.01),
        'rg_lambda': jnp.log(sig) - jnp.log1p(-sig),
        'hgrn_lb_logits': nrm(20, (DEPTH, 2, D_B), 0.5),
        'hgrn_norm_g': 1.0 + nrm(21, (DEPTH, B_HEAD_DIM), 0.02),
        'q_norm_g': 1.0 + nrm(22, (DEPTH, HEAD_DIM), 0.02),
        'k_norm_g': 1.0 + nrm(23, (DEPTH, HEAD_DIM), 0.02),
        'w_branch_a': nrm(24, (DEPTH, D_A, D_MODEL), D_A ** -0.5),
        'w_branch_b': nrm(25, (DEPTH, D_B, D_MODEL), D_B ** -0.5),
        'w_branch_c': nrm(26, (DEPTH, D_C, D_MODEL), D_C ** -0.5),
        'w_out': nrm(27, (DEPTH, D_MODEL, D_MODEL), D_MODEL ** -0.5),
        'w_router': nrm(28, (DEPTH, D_MODEL, N_EXPERTS), D_MODEL ** -0.5),
        'w_exp_gate': nrm(29, (DEPTH, N_EXPERTS, D_MODEL, D_EXPERT), D_MODEL ** -0.5),
        'w_exp_up': nrm(30, (DEPTH, N_EXPERTS, D_MODEL, D_EXPERT), D_MODEL ** -0.5),
        'w_exp_down': nrm(31, (DEPTH, N_EXPERTS, D_EXPERT, D_MODEL), D_EXPERT ** -0.5),
        'final_norm_g': 1.0 + nrm(32, (D_MODEL,), 0.02),
    }


def reference(x_prompt, x_sample, cache_k, cache_v, state_rglru, state_hgrn, c, c_ctx,
              w_ada, b_ada, norm1_g, norm2_g, w_in, conv_w, conv_b, rg_wr, rg_br, rg_wi, rg_bi,
              rg_lambda, hgrn_lb_logits, hgrn_norm_g, q_norm_g, k_norm_g, w_branch_a, w_branch_b,
              w_branch_c, w_out, w_router, w_exp_gate, w_exp_up, w_exp_down, final_norm_g):
    lb_all = jnp.cumsum(jax.nn.softmax(hgrn_lb_logits.astype(jnp.float32), axis=0), axis=0)
    lower_bounds = (lb_all - lb_all[0]).astype(x_prompt.dtype)

    bp = x_prompt.shape[0]
    h0_ctx = jnp.zeros((bp, 2, D_A), x_prompt.dtype)
    s0_ctx = jnp.zeros((bp, 2, B_HEADS, B_HEAD_DIM, B_HEAD_DIM), x_prompt.dtype)
    cond_ctx = jnp.broadcast_to(c_ctx, (bp, D_MODEL))
    rope = axial_rope(x_sample.shape[1])

    xp, xs = x_prompt, x_sample
    k_list, v_list, ra_list, hb_list = [], [], [], []
    for l in range(DEPTH):
        p = {
            'w_ada': w_ada[l], 'b_ada': b_ada[l], 'norm1_g': norm1_g[l], 'norm2_g': norm2_g[l],
            'w_in': w_in[l], 'conv_w': conv_w[l], 'conv_b': conv_b[l],
            'rg_wr': rg_wr[l], 'rg_br': rg_br[l], 'rg_wi': rg_wi[l], 'rg_bi': rg_bi[l],
            'rg_lambda': rg_lambda[l], 'lb': lower_bounds[l], 'hgrn_norm_g': hgrn_norm_g[l],
            'q_norm_g': q_norm_g[l], 'k_norm_g': k_norm_g[l],
            'w_branch_a': w_branch_a[l], 'w_branch_b': w_branch_b[l], 'w_branch_c': w_branch_c[l],
            'w_out': w_out[l], 'w_router': w_router[l], 'w_exp_gate': w_exp_gate[l],
            'w_exp_up': w_exp_up[l], 'w_exp_down': w_exp_down[l],
        }
        xp, (k_l, v_l, ra_l, hb_l) = trunk_layer(xp, cond_ctx, p, h0_ctx, s0_ctx, None, None, None)
        k_list.append(k_l)
        v_list.append(v_l)
        ra_list.append(ra_l)
        hb_list.append(hb_l)
        xs, _ = trunk_layer(xs, c, p, state_rglru[:, l], state_hgrn[:, l], cache_k[:, l], cache_v[:, l], rope)

    y_prompt = rms_norm(xp, final_norm_g)
    y_sample = rms_norm(xs, final_norm_g)
    new_cache_k = jnp.stack(k_list, axis=1)
    new_cache_v = jnp.stack(v_list, axis=1)
    new_state_rglru = jnp.stack(ra_list, axis=1)
    new_state_hgrn = jnp.stack(hb_list, axis=1)
    return (y_prompt, y_sample, new_cache_k, new_cache_v, new_state_rglru, new_state_hgrn)
```

```python
import functools

import numpy as np
import jax
import jax.numpy as jnp
from jax import lax
from jax.experimental import pallas as pl
from jax.experimental.pallas import tpu as pltpu

F32 = jnp.float32
BF16 = jnp.bfloat16

D_MODEL = 2048
D_MIX = D_MODEL // 2
HEAD = 128
N_HEADS = D_MIX // HEAD
CONV_W = 4
RG_C = 8.0
CHUNK = 64
N_LEVELS = 6
KV_HEADS = 2
GROUP = N_HEADS // KV_HEADS
ROPE_FREQS = HEAD // 4
ROPE_BASE = 10000.0
GRID_W = 64
N_EXPERTS = 16
EC_FACTOR = 2
D_EXPERT = D_MODEL // 2
N_SUB = 6
EPS = 1e-6
TILE = 1024
E_PAD = 128
Q_BLK = 128
VMEM_LIMIT = 48 * 1024 * 1024

OFF_XA, OFF_YA, OFF_QB, OFF_FZF, OFF_FZB, OFF_IB, OFF_GB, OFF_QC = [i * D_MIX for i in range(8)]
OFF_KC = 8 * D_MIX
OFF_VC = OFF_KC + KV_HEADS * HEAD
OFF_GA = OFF_VC + KV_HEADS * HEAD
OFF_GB2 = OFF_GA + D_MODEL
OFF_GC = OFF_GB2 + D_MODEL
N_IN = OFF_GC + D_MODEL


def _cparams(sem):
    return pltpu.CompilerParams(dimension_semantics=sem, vmem_limit_bytes=VMEM_LIMIT)


def _sig(x):
    return 1.0 / (1.0 + jnp.exp(-x))


def _silu(x):
    return x * _sig(x)


def _dot(a, b):
    return jnp.dot(a, b, preferred_element_type=F32)


def _dot_nt(a, b):
    return lax.dot_general(a, b, (((1,), (1,)), ((), ())), preferred_element_type=F32)


def _rms(x, g):
    ms = jnp.mean(x * x, axis=-1, keepdims=True)
    return x * lax.rsqrt(ms + EPS) * g


def _ada_kernel(c_ref, w_ref, b_ref, o_ref):
    s = _silu(c_ref[...]).astype(BF16)
    o_ref[...] = _dot(s, w_ref[...].astype(BF16)) + b_ref[...]


def _ada_mod(cond, w_ada, b_ada):
    depth, _, n = w_ada.shape
    rows = cond.shape[0]
    tn = 1024
    return pl.pallas_call(
        _ada_kernel,
        out_shape=jax.ShapeDtypeStruct((depth, rows, n), F32),
        grid=(depth, n // tn),
        in_specs=[
            pl.BlockSpec((rows, D_MODEL), lambda l, j: (0, 0)),
            pl.BlockSpec((None, D_MODEL, tn), lambda l, j: (l, 0, j)),
            pl.BlockSpec((None, 1, tn), lambda l, j: (l, 0, j)),
        ],
        out_specs=pl.BlockSpec((None, rows, tn), lambda l, j: (l, 0, j)),
        compiler_params=_cparams(("parallel", "parallel")),
        name="ada_mod",
    )(cond, w_ada, b_ada.reshape(depth, 1, n))


def _in_proj_kernel(x_ref, g_ref, sc_ref, sh_ref, w_ref, o_ref, u_ref):
    @pl.when(pl.program_id(1) == 0)
    def _():
        u = _rms(x_ref[...], g_ref[...]) * (1.0 + sc_ref[...]) + sh_ref[...]
        u_ref[...] = u.astype(BF16)

    o_ref[...] = _dot(u_ref[...], w_ref[...].astype(BF16))


def _in_proj(x, norm_g, mod, w_in, layer, n_pt):
    m = x.shape[0]
    tn = 512
    reqs_row = lambda i: jnp.maximum(i - n_pt + 1, 0)
    return pl.pallas_call(
        _in_proj_kernel,
        out_shape=jax.ShapeDtypeStruct((m, N_IN), F32),
        grid=(m // TILE, N_IN // tn),
        in_specs=[
            pl.BlockSpec((TILE, D_MODEL), lambda i, j: (i, 0)),
            pl.BlockSpec((None, 1, D_MODEL), lambda i, j: (layer, 0, 0)),
            pl.BlockSpec((None, None, 1, D_MODEL), lambda i, j: (reqs_row(i), 1, 0, 0)),
            pl.BlockSpec((None, None, 1, D_MODEL), lambda i, j: (reqs_row(i), 0, 0, 0)),
            pl.BlockSpec((None, D_MODEL, tn), lambda i, j: (layer, 0, j)),
        ],
        out_specs=pl.BlockSpec((TILE, tn), lambda i, j: (i, j)),
        scratch_shapes=[pltpu.VMEM((TILE, D_MODEL), BF16)],
        compiler_params=_cparams(("parallel", "arbitrary")),
        name="in_proj",
    )(x, norm_g, mod, mod, w_in)


def _softplus(z):
    return jnp.maximum(z, 0.0) + jnp.log1p(jnp.exp(-jnp.abs(z)))


def _gelu_tanh(x):
    return 0.5 * x * (1.0 + jnp.tanh(np.sqrt(2.0 / np.pi).astype(np.float32) * (x + 0.044715 * (x * x * x))))


def _mixa_kernel(n_pt, seq, xa_ref, ya_ref, cw_ref, cb_ref, wr_ref, br_ref, wi_ref, bi_ref, lam_ref, h0_ref,
                 y_ref, fin_ref, af_ref, bf_ref, ab_ref, bb_ref, hf_ref, hb_ref):
    i = pl.program_id(0)

    def body(seg_len, use_state):
        x = xa_ref[...]
        pos = lax.broadcasted_iota(jnp.int32, (TILE, HEAD), 0) & (seg_len - 1)
        cw = cw_ref[...]
        xc = (jnp.where(pos >= 1, pltpu.roll(x, 1, 0), 0.0) * cw[0:1]
              + x * cw[1:2]
              + jnp.where(pos < seg_len - 1, pltpu.roll(x, TILE - 1, 0), 0.0) * cw[2:3]
              + jnp.where(pos < seg_len - 2, pltpu.roll(x, TILE - 2, 0), 0.0) * cw[3:4]
              + cb_ref[...])
        xcb = xc.astype(BF16)
        for d, (a_ref, b_ref) in enumerate(((af_ref, bf_ref), (ab_ref, bb_ref))):
            r = _sig(_dot(xcb, wr_ref[d].astype(BF16)) + br_ref[d])
            ig = _sig(_dot(xcb, wi_ref[d].astype(BF16)) + bi_ref[d])
            log_a = (-RG_C) * r * _softplus(-lam_ref[d])
            a = jnp.exp(log_a)
            a_ref[...] = a
            b_ref[...] = jnp.sqrt(-jnp.tanh(log_a) * (a * a + 1.0)) * (ig * xc)

        nb = seg_len // 8
        rowi = lax.broadcasted_iota(jnp.int32, (8, HEAD), 0)
        n_seg = TILE // seg_len
        for s in range(n_seg):
            base = s * seg_len
            if use_state:
                c0 = (h0_ref[0:1, :], h0_ref[1:2, :])
            else:
                c0 = (jnp.zeros((1, HEAD), F32), jnp.zeros((1, HEAD), F32))

            def step(j, carry):
                cf, cb = carry
                of = pl.multiple_of(base + j * 8, 8)
                ob = pl.multiple_of(base + (nb - 1 - j) * 8, 8)
                a = af_ref[pl.ds(of, 8), :]
                b = bf_ref[pl.ds(of, 8), :]
                for d in (1, 2, 4):
                    m = rowi >= d
                    b = jnp.where(m, a * pltpu.roll(b, d, 0) + b, b)
                    a = jnp.where(m, a * pltpu.roll(a, d, 0), a)
                h = b + a * cf
                hf_ref[pl.ds(of, 8), :] = h
                cf = h[7:8, :]
                a = ab_ref[pl.ds(ob, 8), :]
                b = bb_ref[pl.ds(ob, 8), :]
                for d in (1, 2, 4):
                    m = rowi < 8 - d
                    b = jnp.where(m, a * pltpu.roll(b, 8 - d, 0) + b, b)
                    a = jnp.where(m, a * pltpu.roll(a, 8 - d, 0), a)
                h = b + a * cb
                hb_ref[pl.ds(ob, 8), :] = h
                cb = h[0:1, :]
                return cf, cb

            cf, cb = lax.fori_loop(0, nb, step, c0)
            fin_ref[s, 0:1, :] = cf
            fin_ref[s, 1:2, :] = cb
        for s in range(n_seg, TILE // seq):
            fin_ref[s] = jnp.zeros((2, HEAD), F32)
        y_ref[...] = ((hf_ref[...] + hb_ref[...]) * _gelu_tanh(ya_ref[...])).astype(BF16)

    @pl.when(i < n_pt)
    def _():
        body(seq, False)

    @pl.when(i >= n_pt)
    def _():
        body(TILE, True)


def _mixer_a(y_in, conv_w, conv_b, rg_wr, rg_br, rg_wi, rg_bi, rg_lambda, state, layer, n_pt, seq):
    m = y_in.shape[0]
    depth = conv_w.shape[0]
    nt = m // TILE
    spt = TILE // seq
    req = lambda i: jnp.maximum(i - n_pt, 0)
    vec = lambda a: a.reshape(depth, 2, 1, D_MIX)
    gate_w = pl.BlockSpec((None, 2, None, HEAD, HEAD), lambda i, h: (layer, 0, h, 0, 0))
    gate_b = pl.BlockSpec((None, 2, 1, HEAD), lambda i, h: (layer, 0, 0, h))
    slab = lambda off: pl.BlockSpec((TILE, HEAD), lambda i, h: (i, off // HEAD + h))
    return pl.pallas_call(
        functools.partial(_mixa_kernel, n_pt, seq),
        out_shape=(jax.ShapeDtypeStruct((m, D_MIX), BF16),
                   jax.ShapeDtypeStruct((nt, spt, 2, D_MIX), F32)),
        grid=(nt, N_HEADS),
        in_specs=[
            slab(OFF_XA), slab(OFF_YA),
            pl.BlockSpec((None, CONV_W, HEAD), lambda i, h: (layer, 0, h)),
            pl.BlockSpec((None, 1, HEAD), lambda i, h: (layer, 0, h)),
            gate_w, gate_b, gate_w, gate_b, gate_b,
            pl.BlockSpec((None, None, 2, HEAD), lambda i, h: (req(i), layer, 0, h)),
        ],
        out_specs=(pl.BlockSpec((TILE, HEAD), lambda i, h: (i, h)),
                   pl.BlockSpec((None, spt, 2, HEAD), lambda i, h: (i, 0, 0, h))),
        scratch_shapes=[pltpu.VMEM((TILE, HEAD), F32)] * 6,
        compiler_params=_cparams(("parallel", "parallel")),
        name="mixer_a",
    )(y_in, y_in, conv_w, conv_b.reshape(depth, 1, D_MIX), rg_wr, vec(rg_br), rg_wi, vec(rg_bi), vec(rg_lambda),
      state)


def _hgrn_matrices():
    c = CHUNK
    mats = []
    for reverse in (False, True):
        mm = np.zeros((2 + N_LEVELS, c, c), np.float32)
        for t in range(c):
            if not reverse:
                mm[0, t, :t + 1] = 1.0
                mm[1, t, t + 1:] = 1.0
            else:
                mm[0, t, t:] = 1.0
                mm[1, t, :t] = 1.0
            for lvl in range(N_LEVELS):
                m = c >> (lvl + 1)
                blk, p = divmod(t, 2 * m)
                if not reverse:
                    ref = blk * 2 * m + m - 1
                    if p >= m:
                        mm[2 + lvl, t, ref + 1:t + 1] = 1.0
                    else:
                        mm[2 + lvl, t, t + 1:ref + 1] = 1.0
                else:
                    ref = blk * 2 * m + m
                    if p < m:
                        mm[2 + lvl, t, t:ref] = 1.0
                    else:
                        mm[2 + lvl, t, ref:t] = 1.0
        mats.append(mm.reshape((2 + N_LEVELS) * c, c))
    return np.stack(mats)


def _hgrn_chunk(q, k, v, g, st, mmat, reverse):
    c = CHUNK
    gh = g.astype(BF16)
    gl = (g - gh.astype(F32)).astype(BF16)
    e_all = _dot(mmat, gh) + _dot(mmat, gl)
    e_in = jnp.exp(e_all[0:c])
    e_out = jnp.exp(e_all[c:2 * c])
    vb = v.astype(BF16)
    o = _dot_nt((q * e_in).astype(BF16), st.astype(BF16))
    rowi = lax.broadcasted_iota(jnp.int32, (c, HEAD), 0)
    ti = lax.broadcasted_iota(jnp.int32, (c, c), 0)
    si = lax.broadcasted_iota(jnp.int32, (c, c), 1)
    scores = jnp.where(ti == si, _dot_nt(q.astype(BF16), k.astype(BF16)), 0.0)
    for lvl in range(N_LEVELS):
        m = c >> (lvl + 1)
        e = jnp.exp(e_all[(2 + lvl) * c:(3 + lvl) * c])
        late = (rowi & m) != 0
        q_side = late if not reverse else jnp.logical_not(late)
        qd = jnp.where(q_side, q * e, 0.0).astype(BF16)
        kd = jnp.where(q_side, 0.0, k * e).astype(BF16)
        scores = scores + jnp.where((ti ^ si) < 2 * m, _dot_nt(qd, kd), 0.0)
    o = o + _dot(scores.astype(BF16), vb)
    d_end = e_in[c - 1:c] if not reverse else e_in[0:1]
    ke = (k * e_out).astype(BF16)
    st_new = st * d_end + lax.dot_general(vb, ke, (((0,), (0,)), ((), ())), preferred_element_type=F32)
    return o, st_new


def _mixb_kernel(n_pt, seq, layer, qb_ref, fzf_ref, fzb_ref, ib_ref, gb_ref, lbl_ref, s0_ref, ng_ref, mm_ref,
                 y_ref, sfin_ref, of_ref):
    i = pl.program_id(0)
    depth = lbl_ref.shape[0]

    def lower_bound(d):
        lg = [lbl_ref[j, d] for j in range(depth)]
        mx = functools.reduce(jnp.maximum, lg)
        ex = [jnp.exp(v - mx) for v in lg]
        tot = functools.reduce(lambda a, b: a + b, ex)
        acc = jnp.zeros_like(tot)
        for j in range(1, layer + 1):
            acc = acc + ex[j] / tot
        return acc

    def body(seg_len, use_state):
        nc = seg_len // CHUNK
        lbs = (lower_bound(0), lower_bound(1))

        def load(c0, d):
            rows = pl.ds(c0, CHUNK)
            q = _silu(qb_ref[rows, :])
            v = ib_ref[rows, :]
            fz = (fzf_ref if d == 0 else fzb_ref)[rows, :]
            f = lbs[d] + (1.0 - lbs[d]) * _sig(fz)
            return q, 1.0 - f, v, jnp.log(f)

        for s in range(TILE // seg_len):
            base = s * seg_len
            if use_state:
                st_f, st_b = s0_ref[0].T, s0_ref[1].T
            else:
                st_f = st_b = jnp.zeros((HEAD, HEAD), F32)

            def fwd(j, st):
                c0 = pl.multiple_of(base + j * CHUNK, CHUNK)
                q, k, v, g = load(c0, 0)
                o, st = _hgrn_chunk(q, k, v, g, st, mm_ref[0], False)
                of_ref[pl.ds(c0, CHUNK), :] = o
                return st

            st_f = lax.fori_loop(0, nc, fwd, st_f)

            def bwd(j, st):
                c0 = pl.multiple_of(base + (nc - 1 - j) * CHUNK, CHUNK)
                q, k, v, g = load(c0, 1)
                o, st = _hgrn_chunk(q, k, v, g, st, mm_ref[1], True)
                rows = pl.ds(c0, CHUNK)
                o = _rms(o + of_ref[rows, :], ng_ref[...]) * _silu(gb_ref[rows, :])
                y_ref[rows, :] = o.astype(BF16)
                return st

            st_b = lax.fori_loop(0, nc, bwd, st_b)
            sfin_ref[s, 0] = st_f.T
            sfin_ref[s, 1] = st_b.T
        for s in range(TILE // seg_len, TILE // seq):
            sfin_ref[s] = jnp.zeros((2, HEAD, HEAD), F32)

    @pl.when(i < n_pt)
    def _():
        body(seq, False)

    @pl.when(i >= n_pt)
    def _():
        body(TILE, True)


def _mixer_b(y_in, lb_logits, norm_g, state, mmats, layer, n_pt, seq):
    m = y_in.shape[0]
    depth = lb_logits.shape[0]
    nt = m // TILE
    spt = TILE // seq
    req = lambda i: jnp.maximum(i - n_pt, 0)
    slab = lambda off: pl.BlockSpec((TILE, HEAD), lambda i, h: (i, off // HEAD + h))
    return pl.pallas_call(
        functools.partial(_mixb_kernel, n_pt, seq, layer),
        out_shape=(jax.ShapeDtypeStruct((m, D_MIX), BF16),
                   jax.ShapeDtypeStruct((nt, spt, 2, N_HEADS, HEAD, HEAD), F32)),
        grid=(nt, N_HEADS),
        in_specs=[
            slab(OFF_QB), slab(OFF_FZF), slab(OFF_FZB), slab(OFF_IB), slab(OFF_GB),
            pl.BlockSpec((depth, 2, 1, HEAD), lambda i, h: (0, 0, 0, h)),
            pl.BlockSpec((None, None, 2, None, HEAD, HEAD), lambda i, h: (req(i), layer, 0, h, 0, 0)),
            pl.BlockSpec((None, 1, HEAD), lambda i, h: (layer, 0, 0)),
            pl.BlockSpec(mmats.shape, lambda i, h: (0, 0, 0)),
        ],
        out_specs=(pl.BlockSpec((TILE, HEAD), lambda i, h: (i, h)),
                   pl.BlockSpec((None, spt, 2, None, HEAD, HEAD), lambda i, h: (i, 0, 0, h, 0, 0))),
        scratch_shapes=[pltpu.VMEM((TILE, HEAD), F32)],
        compiler_params=_cparams(("parallel", "parallel")),
        name="mixer_b",
    )(y_in, y_in, y_in, y_in, y_in, lb_logits.reshape(depth, 2, 1, D_MIX), state,
      norm_g.reshape(depth, 1, HEAD), mmats)


def _rope(x, cos, sin_lo, sin_hi):
    return x * cos + pltpu.roll(x, HEAD - ROPE_FREQS, 1) * sin_lo + pltpu.roll(x, ROPE_FREQS, 1) * sin_hi


def _attn_kernel(n_pt, seq, q_ref, k_ref, v_ref, ck_ref, cv_ref, cos_ref, slo_ref, shi_ref, qg_ref, kg_ref,
                 y_ref, kn_ref, qs_ref, ks_ref, vs_ref):
    i = pl.program_id(0)
    scale = HEAD ** -0.5

    def body(seg_len, ctx):
        kn = _rms(k_ref[...], kg_ref[...])
        kn_ref[...] = kn
        if ctx:
            kn = _rope(kn, cos_ref[...], slo_ref[...], shi_ref[...])
        ks_ref[...] = kn.astype(BF16)
        vs_ref[...] = v_ref[...].astype(BF16)
        for g in range(GROUP):
            qn = _rms(q_ref[:, g * HEAD:(g + 1) * HEAD], qg_ref[...])
            if ctx:
                qn = _rope(qn, cos_ref[...], slo_ref[...], shi_ref[...])
            qs_ref[g] = (qn * scale).astype(BF16)
        if ctx:
            ckb = ck_ref[...].astype(BF16)
            cvb = cv_ref[...].astype(BF16)
        for s in range(TILE // seg_len):
            base = s * seg_len

            def qblock(j, carry):
                r0 = pl.multiple_of(base + j * Q_BLK, Q_BLK)
                rows = pl.ds(r0, Q_BLK)
                qcat = jnp.concatenate([qs_ref[g, rows, :] for g in range(GROUP)], axis=0)
                keys = ks_ref[pl.ds(base, seg_len), :]
                vals = vs_ref[pl.ds(base, seg_len), :]
                sc = _dot_nt(qcat, keys)
                mx = jnp.max(sc, axis=-1, keepdims=True)
                if ctx:
                    sc_c = _dot_nt(qcat, ckb)
                    mx = jnp.maximum(mx, jnp.max(sc_c, axis=-1, keepdims=True))
                p = jnp.exp(sc - mx)
                den = jnp.sum(p, axis=-1, keepdims=True)
                o = _dot(p.astype(BF16), vals)
                if ctx:
                    pc = jnp.exp(sc_c - mx)
                    den = den + jnp.sum(pc, axis=-1, keepdims=True)
                    o = o + _dot(pc.astype(BF16), cvb)
                o = o / den
                for g in range(GROUP):
                    y_ref[rows, g * HEAD:(g + 1) * HEAD] = o[g * Q_BLK:(g + 1) * Q_BLK].astype(BF16)
                return carry

            lax.fori_loop(0, seg_len // Q_BLK, qblock, 0)

    @pl.when(i < n_pt)
    def _():
        body(seq, False)

    @pl.when(i >= n_pt)
    def _():
        body(TILE, True)


def _attention(y_in, cache_k, cache_v, rope_tabs, q_norm_g, k_norm_g, layer, n_pt, seq):
    m = y_in.shape[0]
    depth = q_norm_g.shape[0]
    nt = m // TILE
    gw = GROUP * HEAD
    past = cache_k.shape[2]
    req = lambda i: jnp.maximum(i - n_pt, 0)
    ctx_spec = pl.BlockSpec((None, None, past, HEAD), lambda i, h: (req(i), layer, 0, h))
    tab = pl.BlockSpec((TILE, HEAD), lambda i, h: (0, 0))
    gspec = pl.BlockSpec((None, 1, HEAD), lambda i, h: (layer, 0, 0))
    return pl.pallas_call(
        functools.partial(_attn_kernel, n_pt, seq),
        out_shape=(jax.ShapeDtypeStruct((m, D_MIX), BF16),
                   jax.ShapeDtypeStruct((m, KV_HEADS * HEAD), F32)),
        grid=(nt, KV_HEADS),
        in_specs=[
            pl.BlockSpec((TILE, gw), lambda i, h: (i, OFF_QC // gw + h)),
            pl.BlockSpec((TILE, HEAD), lambda i, h: (i, OFF_KC // HEAD + h)),
            pl.BlockSpec((TILE, HEAD), lambda i, h: (i, OFF_VC // HEAD + h)),
            ctx_spec, ctx_spec, tab, tab, tab, gspec, gspec,
        ],
        out_specs=(pl.BlockSpec((TILE, gw), lambda i, h: (i, h)),
                   pl.BlockSpec((TILE, HEAD), lambda i, h: (i, h))),
        scratch_shapes=[pltpu.VMEM((GROUP, TILE, HEAD), BF16), pltpu.VMEM((TILE, HEAD), BF16),
                        pltpu.VMEM((TILE, HEAD), BF16)],
        compiler_params=_cparams(("parallel", "parallel")),
        name="attention",
    )(y_in, y_in, y_in, cache_k, cache_v, *rope_tabs, q_norm_g.reshape(depth, 1, HEAD),
      k_norm_g.reshape(depth, 1, HEAD))


def _rope_tables(n):
    rows = n // GRID_W
    r = jnp.repeat(jnp.arange(rows, dtype=F32), GRID_W)
    col = jnp.tile(jnp.arange(GRID_W, dtype=F32), rows)
    inv = ROPE_BASE ** (-jnp.arange(ROPE_FREQS, dtype=F32) / ROPE_FREQS)
    ar = r[:, None] * inv
    ac = col[:, None] * inv
    ang = jnp.concatenate([ar, ar, ac, ac], axis=-1)
    cos, sin = jnp.cos(ang), jnp.sin(ang)
    first_half = (jnp.arange(HEAD) // ROPE_FREQS) % 2 == 0
    return cos, jnp.where(first_half, -sin, 0.0), jnp.where(first_half, 0.0, sin)


def _merge_kernel(ya_ref, yb_ref, yc_ref, wa_ref, wb_ref, wc_ref, ga_ref, gb_ref, gc_ref, o_ref):
    acc = _sig(ga_ref[...]) * _dot(ya_ref[...], wa_ref[...].astype(BF16))
    acc = acc + _sig(gb_ref[...]) * _dot(yb_ref[...], wb_ref[...].astype(BF16))
    acc = acc + _sig(gc_ref[...]) * _dot(yc_ref[...], wc_ref[...].astype(BF16))
    o_ref[...] = acc.astype(BF16)


def _merge(y_a, y_b, y_c, y_in, w_a, w_b, w_c, layer):
    m = y_in.shape[0]
    tn = 512
    act = pl.BlockSpec((TILE, D_MIX), lambda i, j: (i, 0))
    wsp = pl.BlockSpec((None, D_MIX, tn), lambda i, j: (layer, 0, j))
    gate = lambda off: pl.BlockSpec((TILE, tn), lambda i, j: (i, off // tn + j))
    return pl.pallas_call(
        _merge_kernel,
        out_shape=jax.ShapeDtypeStruct((m, D_MODEL), BF16),
        grid=(m // TILE, D_MODEL // tn),
        in_specs=[act, act, act, wsp, wsp, wsp, gate(OFF_GA), gate(OFF_GB2), gate(OFF_GC)],
        out_specs=pl.BlockSpec((TILE, tn), lambda i, j: (i, j)),
        compiler_params=_cparams(("parallel", "parallel")),
        name="merge",
    )(y_a, y_b, y_c, w_a, w_b, w_c, y_in, y_in, y_in)


def _out_proj_kernel(mg_ref, w_ref, x_ref, g1_ref, ng_ref, sc_ref, sh_ref, wr_ref, xo_ref, u_ref, lg_ref, acc_ref):
    k = pl.program_id(1)

    @pl.when(k == 0)
    def _():
        acc_ref[...] = jnp.zeros_like(acc_ref)

    acc_ref[...] += _dot(mg_ref[...], w_ref[...].astype(BF16))

    @pl.when(k == pl.num_programs(1) - 1)
    def _():
        xn = x_ref[...] + g1_ref[...] * acc_ref[...]
        xo_ref[...] = xn
        u = _rms(xn, ng_ref[...]) * (1.0 + sc_ref[...]) + sh_ref[...]
        u_ref[...] = u.astype(BF16)
        w = wr_ref[...]
        uh = u.astype(BF16)
        ul = (u - uh.astype(F32)).astype(BF16)
        wh = w.astype(BF16)
        wl = (w - wh.astype(F32)).astype(BF16)
        lg_ref[...] = _dot(uh, wh) + (_dot(uh, wl) + _dot(ul, wh))


def _out_proj(merged, w_out, x, mod, norm_g, w_router_pad, layer, n_pt):
    m = x.shape[0]
    tm, tk = 512, 512
    per = TILE // tm
    row = lambda i: jnp.maximum(i // per - n_pt + 1, 0)
    mspec = lambda c: pl.BlockSpec((None, None, 1, D_MODEL), lambda i, k: (row(i), c, 0, 0))
    return pl.pallas_call(
        _out_proj_kernel,
        out_shape=(jax.ShapeDtypeStruct((m, D_MODEL), F32),
                   jax.ShapeDtypeStruct((m, D_MODEL), BF16),
                   jax.ShapeDtypeStruct((m, E_PAD), F32)),
        grid=(m // tm, D_MODEL // tk),
        in_specs=[
            pl.BlockSpec((tm, tk), lambda i, k: (i, k)),
            pl.BlockSpec((None, tk, D_MODEL), lambda i, k: (layer, k, 0)),
            pl.BlockSpec((tm, D_MODEL), lambda i, k: (i, 0)),
            mspec(2),
            pl.BlockSpec((None, 1, D_MODEL), lambda i, k: (layer, 0, 0)),
            mspec(4), mspec(3),
            pl.BlockSpec((None, D_MODEL, E_PAD), lambda i, k: (layer, 0, 0)),
        ],
        out_specs=(pl.BlockSpec((tm, D_MODEL), lambda i, k: (i, 0)),
                   pl.BlockSpec((tm, D_MODEL), lambda i, k: (i, 0)),
                   pl.BlockSpec((tm, E_PAD), lambda i, k: (i, 0))),
        scratch_shapes=[pltpu.VMEM((tm, D_MODEL), F32)],
        compiler_params=_cparams(("parallel", "arbitrary")),
        name="out_proj",
    )(merged, w_out, x, mod, norm_g, mod, mod, w_router_pad)


def _route_kernel(n_pt, seq, lg_ref, u_ref, xin_ref, pt_ref, gate_ref, rank_ref, afft_ref):
    i = pl.program_id(0)
    slots = TILE // (N_EXPERTS // EC_FACTOR)

    def body(seg_len):
        cap = EC_FACTOR * seg_len // N_EXPERTS
        shift = seg_len.bit_length() - 1
        lt = lg_ref[...].T[0:N_EXPERTS]
        ex = jnp.exp(lt - jnp.max(lt, axis=0, keepdims=True))
        aff = ex / jnp.sum(ex, axis=0, keepdims=True)
        afft_ref[...] = jnp.concatenate([aff, jnp.zeros((E_PAD - N_EXPERTS, TILE), F32)], axis=0).T
        rank_ref[...] = jnp.zeros_like(rank_ref)
        blk = 128
        for e in range(N_EXPERTS):
            for s in range(TILE // seg_len):
                base = s * seg_len
                a_row = aff[e:e + 1, base:base + seg_len]
                s_idx = lax.broadcasted_iota(jnp.int32, (blk, seg_len), 1)

                def tblock(j, acc):
                    t0 = pl.multiple_of(j * blk, blk)
                    a_col = afft_ref[pl.ds(base + t0, blk), :][:, e:e + 1]
                    t_idx = lax.broadcasted_iota(jnp.int32, (blk, seg_len), 0) + t0
                    beats = jnp.logical_or(a_col > a_row, jnp.logical_and(a_col == a_row, t_idx < s_idx))
                    return acc + jnp.where(beats, 1.0, 0.0).reshape(blk // 8, 8, seg_len).sum(axis=0)

                acc = lax.fori_loop(0, seg_len // blk, tblock, jnp.zeros((8, seg_len), F32))
                rank_ref[e:e + 1, base:base + seg_len] = jnp.sum(acc, axis=0, keepdims=True)
        rank = rank_ref[...]
        rank_t = rank.T
        tok_row = lax.broadcasted_iota(jnp.int32, (slots, TILE), 1)
        slot_row = (rank + ((tok_row[0:1] >> shift) * cap).astype(F32))[0:N_EXPERTS]
        tok_col = lax.broadcasted_iota(jnp.int32, (TILE, slots), 0)
        slot_col = rank_t + ((tok_col[:, 0:1] >> shift) * cap).astype(F32)
        sub = lax.broadcasted_iota(jnp.int32, (slots, TILE), 0).astype(F32)
        lane = lax.broadcasted_iota(jnp.int32, (TILE, slots), 1).astype(F32)
        ub = u_ref[...]
        for e in range(N_EXPERTS):
            hit = jnp.logical_and(slot_row[e:e + 1] == sub, rank[e:e + 1] < cap)
            gate_ref[e] = jnp.sum(jnp.where(hit, aff[e:e + 1], 0.0), axis=1, keepdims=True)
            xin_ref[e] = _dot(jnp.where(hit, 1.0, 0.0).astype(BF16), ub).astype(BF16)
            hit_t = jnp.logical_and(slot_col[:, e:e + 1] == lane, rank_t[:, e:e + 1] < cap)
            pt_ref[:, e * slots:(e + 1) * slots] = jnp.where(hit_t, 1.0, 0.0).astype(BF16)

    @pl.when(i < n_pt)
    def _():
        body(seq)

    @pl.when(i >= n_pt)
    def _():
        body(TILE)


def _route(logits, u, n_pt, seq):
    m = u.shape[0]
    nt = m // TILE
    slots = TILE // (N_EXPERTS // EC_FACTOR)
    return pl.pallas_call(
        functools.partial(_route_kernel, n_pt, seq),
        out_shape=(jax.ShapeDtypeStruct((N_EXPERTS, nt * slots, D_MODEL), BF16),
                   jax.ShapeDtypeStruct((m, N_EXPERTS * slots), BF16),
                   jax.ShapeDtypeStruct((N_EXPERTS, nt * slots, 1), F32)),
        grid=(nt,),
        in_specs=[pl.BlockSpec((TILE, E_PAD), lambda i: (i, 0)),
                  pl.BlockSpec((TILE, D_MODEL), lambda i: (i, 0))],
        out_specs=(pl.BlockSpec((N_EXPERTS, slots, D_MODEL), lambda i: (0, i, 0)),
                   pl.BlockSpec((TILE, N_EXPERTS * slots), lambda i: (i, 0)),
                   pl.BlockSpec((N_EXPERTS, slots, 1), lambda i: (0, i, 0))),
        scratch_shapes=[pltpu.VMEM((E_PAD, TILE), F32), pltpu.VMEM((TILE, E_PAD), F32)],
        compiler_params=_cparams(("parallel",)),
        name="route",
    )(logits, u)


def _expert_kernel(x_ref, wg_ref, wu_ref, wd_ref, gate_ref, o_ref, acc_ref):
    f = pl.program_id(2)

    @pl.when(f == 0)
    def _():
        acc_ref[...] = jnp.zeros_like(acc_ref)

    x = x_ref[...]
    h = _silu(_dot(x, wg_ref[...].astype(BF16))) * _dot(x, wu_ref[...].astype(BF16))
    acc_ref[...] += _dot(h.astype(BF16), wd_ref[...].astype(BF16))

    @pl.when(f == pl.num_programs(2) - 1)
    def _():
        o_ref[...] = (acc_ref[...] * gate_ref[...]).astype(BF16)


def _experts(xin, gates, w_gate, w_up, w_down, layer):
    _, rows, _ = xin.shape
    n_r = 2 if rows % 32 == 0 else 1
    tr = rows // n_r
    tf = 256
    return pl.pallas_call(
        _expert_kernel,
        out_shape=jax.ShapeDtypeStruct((N_EXPERTS, rows, D_MODEL), BF16),
        grid=(N_EXPERTS, n_r, D_EXPERT // tf),
        in_specs=[
            pl.BlockSpec((None, tr, D_MODEL), lambda e, r, f: (e, r, 0)),
            pl.BlockSpec((None, None, D_MODEL, tf), lambda e, r, f: (layer, e, 0, f)),
            pl.BlockSpec((None, None, D_MODEL, tf), lambda e, r, f: (layer, e, 0, f)),
            pl.BlockSpec((None, None, tf, D_MODEL), lambda e, r, f: (layer, e, f, 0)),
            pl.BlockSpec((None, tr, 1), lambda e, r, f: (e, r, 0)),
        ],
        out_specs=pl.BlockSpec((None, tr, D_MODEL), lambda e, r, f: (e, r, 0)),
        scratch_shapes=[pltpu.VMEM((tr, D_MODEL), F32)],
        compiler_params=_cparams(("parallel", "parallel", "arbitrary")),
        name="experts",
    )(xin, w_gate, w_up, w_down, gates)


def _scatter_kernel(pt_ref, y_ref, x_ref, g2_ref, o_ref):
    tn = y_ref.shape[-1]
    y = y_ref[...].reshape(pt_ref.shape[1], tn)
    o_ref[...] = x_ref[...] + g2_ref[...] * _dot(pt_ref[...], y)


def _scatter(pt, y, x, mod, n_pt):
    m = x.shape[0]
    slots = TILE // (N_EXPERTS // EC_FACTOR)
    tn = 512
    row = lambda i: jnp.maximum(i - n_pt + 1, 0)
    return pl.pallas_call(
        _scatter_kernel,
        out_shape=jax.ShapeDtypeStruct((m, D_MODEL), F32),
        grid=(m // TILE, D_MODEL // tn),
        in_specs=[
            pl.BlockSpec((TILE, N_EXPERTS * slots), lambda i, j: (i, 0)),
            pl.BlockSpec((N_EXPERTS, slots, tn), lambda i, j: (0, i, j)),
            pl.BlockSpec((TILE, tn), lambda i, j: (i, j)),
            pl.BlockSpec((None, None, 1, tn), lambda i, j: (row(i), 5, 0, j)),
        ],
        out_specs=pl.BlockSpec((TILE, tn), lambda i, j: (i, j)),
        compiler_params=_cparams(("parallel", "parallel")),
        name="scatter",
    )(pt, y, x, mod)


def _final_norm_kernel(x_ref, g_ref, o_ref):
    o_ref[...] = _rms(x_ref[...], g_ref[...])


def _final_norm(x, g, first_tile, n_tiles):
    tm = 512
    per = TILE // tm
    return pl.pallas_call(
        _final_norm_kernel,
        out_shape=jax.ShapeDtypeStruct((n_tiles * TILE, D_MODEL), F32),
        grid=(n_tiles * per,),
        in_specs=[pl.BlockSpec((tm, D_MODEL), lambda i: (i + first_tile * per, 0)),
                  pl.BlockSpec((1, D_MODEL), lambda i: (0, 0))],
        out_specs=pl.BlockSpec((tm, D_MODEL), lambda i: (i, 0)),
        compiler_params=_cparams(("parallel",)),
        name="final_norm",
    )(x, g.reshape(1, D_MODEL))


def kernel(x_prompt, x_sample, cache_k, cache_v, state_rglru, state_hgrn, c, c_ctx, w_ada, b_ada, norm1_g, norm2_g, w_in, conv_w, conv_b, rg_wr, rg_br, rg_wi, rg_bi, rg_lambda, hgrn_lb_logits, hgrn_norm_g, q_norm_g, k_norm_g, w_branch_a, w_branch_b, w_branch_c, w_out, w_router, w_exp_gate, w_exp_up, w_exp_down, final_norm_g):
    bp, seq, _ = x_prompt.shape
    bs, dec_seq, _ = x_sample.shape
    depth = w_in.shape[0]
    past = cache_k.shape[2]
    assert dec_seq == TILE and TILE % seq == 0 and (bp * seq) % TILE == 0 and seq & (seq - 1) == 0
    assert seq % Q_BLK == 0 and past == seq
    n_pt = bp * seq // TILE
    spt = TILE // seq

    x = jnp.concatenate([x_prompt.reshape(bp * seq, D_MODEL), x_sample.reshape(bs * dec_seq, D_MODEL)], axis=0)
    n_cond = -(-(1 + bs) // 8) * 8
    cond = jnp.concatenate([c_ctx[None], c, jnp.zeros((n_cond - 1 - bs, D_MODEL), F32)], axis=0)
    mod_all = _ada_mod(cond, w_ada, b_ada).reshape(depth, n_cond, N_SUB, 1, D_MODEL)
    rope_tabs = _rope_tables(dec_seq)
    mmats = jnp.asarray(_hgrn_matrices(), BF16)
    w_router_pad = jnp.pad(w_router, ((0, 0), (0, 0), (0, E_PAD - N_EXPERTS)))
    ck = cache_k.reshape(bs, depth, past, KV_HEADS * HEAD)
    cv = cache_v.reshape(bs, depth, past, KV_HEADS * HEAD)
    n1 = norm1_g.reshape(depth, 1, D_MODEL)
    n2 = norm2_g.reshape(depth, 1, D_MODEL)

    k_list, v_list, ra_list, hb_list = [], [], [], []
    for l in range(depth):
        mod = mod_all[l]
        y_in = _in_proj(x, n1, mod, w_in, l, n_pt)
        y_a, fin_a = _mixer_a(y_in, conv_w, conv_b, rg_wr, rg_br, rg_wi, rg_bi, rg_lambda, state_rglru, l, n_pt, seq)
        y_b, fin_b = _mixer_b(y_in, hgrn_lb_logits, hgrn_norm_g, state_hgrn, mmats, l, n_pt, seq)
        y_c, k_n = _attention(y_in, ck, cv, rope_tabs, q_norm_g, k_norm_g, l, n_pt, seq)
        merged = _merge(y_a, y_b, y_c, y_in, w_branch_a, w_branch_b, w_branch_c, l)
        x, u2, logits = _out_proj(merged, w_out, x, mod, n2, w_router_pad, l, n_pt)
        xin, pt, gates = _route(logits, u2, n_pt, seq)
        y_e = _experts(xin, gates, w_exp_gate, w_exp_up, w_exp_down, l)
        x = _scatter(pt, y_e, x, mod, n_pt)
        k_list.append(k_n[:bp * seq].reshape(bp, seq, KV_HEADS, HEAD))
        v_list.append(y_in[:bp * seq, OFF_VC:OFF_VC + KV_HEADS * HEAD].reshape(bp, seq, KV_HEADS, HEAD))
        ra_list.append(fin_a[:n_pt].reshape(bp, 2, D_MIX))
        hb_list.append(fin_b[:n_pt].reshape(bp, 2, N_HEADS, HEAD, HEAD))

    y_prompt = _final_norm(x, final_norm_g, 0, n_pt).reshape(bp, seq, D_MODEL)
    y_sample = _final_norm(x, final_norm_g, n_pt, bs).reshape(bs, dec_seq, D_MODEL)
    return (y_prompt, y_sample, jnp.stack(k_list, axis=1), jnp.stack(v_list, axis=1),
            jnp.stack(ra_list, axis=1), jnp.stack(hb_list, axis=1))
```

```python
import functools

import numpy as np
import jax
import jax.numpy as jnp
from jax import lax
from jax.experimental import pallas as pl
from jax.experimental.pallas import tpu as pltpu

F32 = jnp.float32
BF16 = jnp.bfloat16

D_MODEL = 2048
D_MIX = D_MODEL // 2
HEAD = 128
N_HEADS = D_MIX // HEAD
CONV_W = 4
RG_C = 8.0
CHUNK = 64
N_LEVELS = 6
KV_HEADS = 2
KV_DIM = KV_HEADS * HEAD
GROUP = N_HEADS // KV_HEADS
ROPE_FREQS = HEAD // 4
ROPE_BASE = 10000.0
GRID_W = 64
N_EXPERTS = 16
EC_FACTOR = 2
D_EXPERT = D_MODEL // 2
N_SUB = 6
EPS = 1e-6
TILE = 1024
BIG_TILE = 2 * TILE
SLOTS = TILE * EC_FACTOR // N_EXPERTS
E_PAD = 128
Q_BLK = 128
F32_BITS_MAX_FINITE = 0x7F7FFFFF
VMEM_LIMIT = 48 * 1024 * 1024

OFF_XA, OFF_YA, OFF_QB, OFF_FZF, OFF_FZB, OFF_IB, OFF_GB, OFF_QC = [i * D_MIX for i in range(8)]
OFF_KC = 8 * D_MIX
OFF_VC = OFF_KC + KV_DIM
OFF_GA = OFF_VC + KV_DIM
OFF_GB2 = OFF_GA + D_MODEL
OFF_GC = OFF_GB2 + D_MODEL
N_IN = OFF_GC + D_MODEL
IN_TN = 2 * KV_DIM


def _cparams(sem):
    return pltpu.CompilerParams(dimension_semantics=sem, vmem_limit_bytes=VMEM_LIMIT)


def _sig(x):
    return 0.5 * jnp.tanh(0.5 * x) + 0.5


def _silu(x):
    return x * _sig(x)


def _dot(a, b):
    return jnp.dot(a, b, preferred_element_type=F32)


def _dot_nt(a, b):
    return lax.dot_general(a, b, (((1,), (1,)), ((), ())), preferred_element_type=F32)


def _rms(x, g):
    ms = jnp.mean(x * x, axis=-1, keepdims=True)
    return x * lax.rsqrt(ms + EPS) * g


def _mod_row(tile_rows, n_pt):
    per = TILE // tile_rows if tile_rows <= TILE else None
    if per is None:
        mult = tile_rows // TILE
        return lambda i: jnp.maximum(i * mult - n_pt + 1, 0)
    return lambda i: jnp.maximum(i // per - n_pt + 1, 0)


def _ada_kernel(c_ref, w_ref, b_ref, o_ref):
    s = _silu(c_ref[...]).astype(BF16)
    o_ref[...] = _dot(s, w_ref[...].astype(BF16)) + b_ref[...]


def _ada_mod(cond, w_ada, b_ada):
    depth, _, n = w_ada.shape
    rows = cond.shape[0]
    tn = 1024
    return pl.pallas_call(
        _ada_kernel,
        out_shape=jax.ShapeDtypeStruct((depth, rows, n), F32),
        grid=(depth, n // tn),
        in_specs=[
            pl.BlockSpec((rows, D_MODEL), lambda l, j: (0, 0)),
            pl.BlockSpec((None, D_MODEL, tn), lambda l, j: (l, 0, j)),
            pl.BlockSpec((None, 1, tn), lambda l, j: (l, 0, j)),
        ],
        out_specs=pl.BlockSpec((None, rows, tn), lambda l, j: (l, 0, j)),
        compiler_params=_cparams(("parallel", "parallel")),
        name="ada_mod",
    )(cond, w_ada, b_ada.reshape(depth, 1, n))


def _norm_mod_kernel(x_ref, g_ref, sc_ref, sh_ref, u_ref):
    u_ref[...] = (_rms(x_ref[...], g_ref[...]) * (1.0 + sc_ref[...]) + sh_ref[...]).astype(BF16)


def _norm_router_kernel(x_ref, g_ref, sc_ref, sh_ref, wr_ref, u_ref, lg_ref):
    u = _rms(x_ref[...], g_ref[...]) * (1.0 + sc_ref[...]) + sh_ref[...]
    u_ref[...] = u.astype(BF16)
    w = wr_ref[...]
    uh = u.astype(BF16)
    ul = (u - uh.astype(F32)).astype(BF16)
    wh = w.astype(BF16)
    wl = (w - wh.astype(F32)).astype(BF16)
    lg_ref[...] = _dot(uh, wh) + (_dot(uh, wl) + _dot(ul, wh))


def _norm_mod(x, norm_g, mod, layer, n_pt, sub, w_router_pad=None):
    m = x.shape[0]
    tm = 512
    row = _mod_row(tm, n_pt)
    mspec = lambda c: pl.BlockSpec((None, None, 1, D_MODEL), lambda i: (row(i), c, 0, 0))
    in_specs = [pl.BlockSpec((tm, D_MODEL), lambda i: (i, 0)),
                pl.BlockSpec((None, 1, D_MODEL), lambda i: (layer, 0, 0)),
                mspec(3 * sub + 1), mspec(3 * sub)]
    u_shape = jax.ShapeDtypeStruct((m, D_MODEL), BF16)
    u_spec = pl.BlockSpec((tm, D_MODEL), lambda i: (i, 0))
    if w_router_pad is None:
        return pl.pallas_call(
            _norm_mod_kernel, out_shape=u_shape, grid=(m // tm,), in_specs=in_specs, out_specs=u_spec,
            compiler_params=_cparams(("parallel",)), name="norm_mod",
        )(x, norm_g, mod, mod)
    return pl.pallas_call(
        _norm_router_kernel,
        out_shape=(u_shape, jax.ShapeDtypeStruct((m, E_PAD), F32)),
        grid=(m // tm,),
        in_specs=in_specs + [pl.BlockSpec((None, D_MODEL, E_PAD), lambda i: (layer, 0, 0))],
        out_specs=(u_spec, pl.BlockSpec((tm, E_PAD), lambda i: (i, 0))),
        compiler_params=_cparams(("parallel",)), name="norm_router",
    )(x, norm_g, mod, mod, w_router_pad)


def _in_proj_kernel(kv_tile, u_ref, w_ref, o_ref, kv_ref):
    acc = _dot(u_ref[...], w_ref[...].astype(BF16))
    o_ref[...] = acc.astype(BF16)

    @pl.when(pl.program_id(1) == kv_tile)
    def _():
        kv_ref[...] = acc


def _in_proj(u, w_in, layer):
    m = u.shape[0]
    assert OFF_KC % IN_TN == 0 and N_IN % IN_TN == 0
    return pl.pallas_call(
        functools.partial(_in_proj_kernel, OFF_KC // IN_TN),
        out_shape=(jax.ShapeDtypeStruct((m, N_IN), BF16), jax.ShapeDtypeStruct((m, IN_TN), F32)),
        grid=(m // BIG_TILE, N_IN // IN_TN),
        in_specs=[
            pl.BlockSpec((BIG_TILE, D_MODEL), lambda i, j: (i, 0)),
            pl.BlockSpec((None, D_MODEL, IN_TN), lambda i, j: (layer, 0, j)),
        ],
        out_specs=(pl.BlockSpec((BIG_TILE, IN_TN), lambda i, j: (i, j)),
                   pl.BlockSpec((BIG_TILE, IN_TN), lambda i, j: (i, 0))),
        compiler_params=_cparams(("parallel", "arbitrary")),
        name="in_proj",
    )(u, w_in)


def _softplus(z):
    return jnp.maximum(z, 0.0) + jnp.log1p(jnp.exp(-jnp.abs(z)))


def _gelu_tanh(x):
    return 0.5 * x * (1.0 + jnp.tanh(np.sqrt(2.0 / np.pi).astype(np.float32) * (x + 0.044715 * (x * x * x))))


def _mixa_kernel(n_pt, seq, xa_ref, ya_ref, cw_ref, cb_ref, wr_ref, br_ref, wi_ref, bi_ref, lam_ref, h0_ref,
                 y_ref, fin_ref, af_ref, bf_ref, ab_ref, bb_ref, hf_ref, hb_ref):
    i = pl.program_id(0)

    def body(seg_len, use_state):
        x = xa_ref[...].astype(F32)
        pos = lax.broadcasted_iota(jnp.int32, (TILE, HEAD), 0) & (seg_len - 1)
        cw = cw_ref[...]
        xc = (jnp.where(pos >= 1, pltpu.roll(x, 1, 0), 0.0) * cw[0:1]
              + x * cw[1:2]
              + jnp.where(pos < seg_len - 1, pltpu.roll(x, TILE - 1, 0), 0.0) * cw[2:3]
              + jnp.where(pos < seg_len - 2, pltpu.roll(x, TILE - 2, 0), 0.0) * cw[3:4]
              + cb_ref[...])
        xcb = xc.astype(BF16)
        for d, (a_ref, b_ref) in enumerate(((af_ref, bf_ref), (ab_ref, bb_ref))):
            r = _sig(_dot(xcb, wr_ref[d].astype(BF16)) + br_ref[d])
            ig = _sig(_dot(xcb, wi_ref[d].astype(BF16)) + bi_ref[d])
            log_a = (-RG_C) * r * _softplus(-lam_ref[d])
            a = jnp.exp(log_a)
            a_ref[...] = a
            b_ref[...] = jnp.sqrt(-jnp.tanh(log_a) * (a * a + 1.0)) * (ig * xc)

        nb = seg_len // 8
        rowi = lax.broadcasted_iota(jnp.int32, (8, HEAD), 0)
        n_seg = TILE // seg_len
        for s in range(n_seg):
            base = s * seg_len
            if use_state:
                c0 = (h0_ref[0:1, :], h0_ref[1:2, :])
            else:
                c0 = (jnp.zeros((1, HEAD), F32), jnp.zeros((1, HEAD), F32))

            def step(j, carry):
                cf, cb = carry
                of = pl.multiple_of(base + j * 8, 8)
                ob = pl.multiple_of(base + (nb - 1 - j) * 8, 8)
                a = af_ref[pl.ds(of, 8), :]
                b = bf_ref[pl.ds(of, 8), :]
                for d in (1, 2, 4):
                    m = rowi >= d
                    b = jnp.where(m, a * pltpu.roll(b, d, 0) + b, b)
                    a = jnp.where(m, a * pltpu.roll(a, d, 0), a)
                h = b + a * cf
                hf_ref[pl.ds(of, 8), :] = h
                cf = h[7:8, :]
                a = ab_ref[pl.ds(ob, 8), :]
                b = bb_ref[pl.ds(ob, 8), :]
                for d in (1, 2, 4):
                    m = rowi < 8 - d
                    b = jnp.where(m, a * pltpu.roll(b, 8 - d, 0) + b, b)
                    a = jnp.where(m, a * pltpu.roll(a, 8 - d, 0), a)
                h = b + a * cb
                hb_ref[pl.ds(ob, 8), :] = h
                cb = h[0:1, :]
                return cf, cb

            cf, cb = lax.fori_loop(0, nb, step, c0, unroll=4)
            fin_ref[s, 0:1, :] = cf
            fin_ref[s, 1:2, :] = cb
        for s in range(n_seg, TILE // seq):
            fin_ref[s] = jnp.zeros((2, HEAD), F32)
        y_ref[...] = ((hf_ref[...] + hb_ref[...]) * _gelu_tanh(ya_ref[...].astype(F32))).astype(BF16)

    @pl.when(i < n_pt)
    def _():
        body(seq, False)

    @pl.when(i >= n_pt)
    def _():
        body(TILE, True)


def _mixer_a(y_in, conv_w, conv_b, rg_wr, rg_br, rg_wi, rg_bi, rg_lambda, state, layer, n_pt, seq):
    m = y_in.shape[0]
    depth = conv_w.shape[0]
    nt = m // TILE
    spt = TILE // seq
    req = lambda i: jnp.maximum(i - n_pt, 0)
    vec = lambda a: a.reshape(depth, 2, 1, D_MIX)
    gate_w = pl.BlockSpec((None, 2, None, HEAD, HEAD), lambda i, h: (layer, 0, h, 0, 0))
    gate_b = pl.BlockSpec((None, 2, 1, HEAD), lambda i, h: (layer, 0, 0, h))
    slab = lambda off: pl.BlockSpec((TILE, HEAD), lambda i, h: (i, off // HEAD + h))
    return pl.pallas_call(
        functools.partial(_mixa_kernel, n_pt, seq),
        out_shape=(jax.ShapeDtypeStruct((m, D_MIX), BF16),
                   jax.ShapeDtypeStruct((nt, spt, 2, D_MIX), F32)),
        grid=(nt, N_HEADS),
        in_specs=[
            slab(OFF_XA), slab(OFF_YA),
            pl.BlockSpec((None, CONV_W, HEAD), lambda i, h: (layer, 0, h)),
            pl.BlockSpec((None, 1, HEAD), lambda i, h: (layer, 0, h)),
            gate_w, gate_b, gate_w, gate_b, gate_b,
            pl.BlockSpec((None, None, 2, HEAD), lambda i, h: (req(i), layer, 0, h)),
        ],
        out_specs=(pl.BlockSpec((TILE, HEAD), lambda i, h: (i, h)),
                   pl.BlockSpec((None, spt, 2, HEAD), lambda i, h: (i, 0, 0, h))),
        scratch_shapes=[pltpu.VMEM((TILE, HEAD), F32)] * 6,
        compiler_params=_cparams(("parallel", "parallel")),
        name="mixer_a",
    )(y_in, y_in, conv_w, conv_b.reshape(depth, 1, D_MIX), rg_wr, vec(rg_br), rg_wi, vec(rg_bi), vec(rg_lambda),
      state)


def _hgrn_matrices():
    c = CHUNK
    mats = []
    for reverse in (False, True):
        mm = np.zeros((2 + N_LEVELS, c, c), np.float32)
        for t in range(c):
            if not reverse:
                mm[0, t, :t + 1] = 1.0
                mm[1, t, t + 1:] = 1.0
            else:
                mm[0, t, t:] = 1.0
                mm[1, t, :t] = 1.0
            for lvl in range(N_LEVELS):
                m = c >> (lvl + 1)
                blk, p = divmod(t, 2 * m)
                if not reverse:
                    ref = blk * 2 * m + m - 1
                    if p >= m:
                        mm[2 + lvl, t, ref + 1:t + 1] = 1.0
                    else:
                        mm[2 + lvl, t, t + 1:ref + 1] = 1.0
                else:
                    ref = blk * 2 * m + m
                    if p < m:
                        mm[2 + lvl, t, t:ref] = 1.0
                    else:
                        mm[2 + lvl, t, ref:t] = 1.0
        mats.append(mm.reshape((2 + N_LEVELS) * c, c))
    return np.stack(mats)


def _hgrn_chunk(q, k, v, g, st, mmat, reverse):
    c = CHUNK
    gh = g.astype(BF16)
    gl = (g - gh.astype(F32)).astype(BF16)
    e_all = _dot(mmat, gh) + _dot(mmat, gl)
    e_in = jnp.exp(e_all[0:c])
    e_out = jnp.exp(e_all[c:2 * c])
    vb = v.astype(BF16)
    o = _dot_nt((q * e_in).astype(BF16), st.astype(BF16))
    rowi = lax.broadcasted_iota(jnp.int32, (c, HEAD), 0)
    ti = lax.broadcasted_iota(jnp.int32, (c, c), 0)
    si = lax.broadcasted_iota(jnp.int32, (c, c), 1)
    scores = jnp.where(ti == si, _dot_nt(q.astype(BF16), k.astype(BF16)), 0.0)
    for lvl in range(N_LEVELS):
        m = c >> (lvl + 1)
        e = jnp.exp(e_all[(2 + lvl) * c:(3 + lvl) * c])
        late = (rowi & m) != 0
        q_side = late if not reverse else jnp.logical_not(late)
        qd = jnp.where(q_side, q * e, 0.0).astype(BF16)
        kd = jnp.where(q_side, 0.0, k * e).astype(BF16)
        scores = scores + jnp.where((ti ^ si) < 2 * m, _dot_nt(qd, kd), 0.0)
    o = o + _dot(scores.astype(BF16), vb)
    d_end = e_in[c - 1:c] if not reverse else e_in[0:1]
    ke = (k * e_out).astype(BF16)
    st_new = st * d_end + lax.dot_general(vb, ke, (((0,), (0,)), ((), ())), preferred_element_type=F32)
    return o, st_new


def _mixb_kernel(n_pt, seq, layer, qb_ref, fzf_ref, fzb_ref, ib_ref, gb_ref, lbl_ref, s0_ref, ng_ref, mm_ref,
                 y_ref, sfin_ref, of_ref, ob_ref):
    i = pl.program_id(0)
    depth = lbl_ref.shape[0]

    def lower_bound(d):
        lg = [lbl_ref[j, d] for j in range(depth)]
        mx = functools.reduce(jnp.maximum, lg)
        ex = [jnp.exp(v - mx) for v in lg]
        tot = functools.reduce(lambda a, b: a + b, ex)
        acc = jnp.zeros_like(tot)
        for j in range(1, layer + 1):
            acc = acc + ex[j] / tot
        return acc

    def body(seg_len, use_state):
        nc = seg_len // CHUNK
        lbs = (lower_bound(0), lower_bound(1))

        def load(c0, d):
            rows = pl.ds(c0, CHUNK)
            q = _silu(qb_ref[rows, :].astype(F32))
            v = ib_ref[rows, :].astype(F32)
            fz = (fzf_ref if d == 0 else fzb_ref)[rows, :].astype(F32)
            f = lbs[d] + (1.0 - lbs[d]) / (1.0 + jnp.exp(-fz))
            return q, 1.0 - f, v, jnp.log(f)

        for s in range(TILE // seg_len):
            base = s * seg_len
            if use_state:
                st0 = (s0_ref[0].T, s0_ref[1].T)
            else:
                st0 = (jnp.zeros((HEAD, HEAD), F32), jnp.zeros((HEAD, HEAD), F32))

            def step(j, carry):
                st_f, st_b = carry
                cf = pl.multiple_of(base + j * CHUNK, CHUNK)
                cb = pl.multiple_of(base + (nc - 1 - j) * CHUNK, CHUNK)
                q, k, v, g = load(cf, 0)
                o, st_f = _hgrn_chunk(q, k, v, g, st_f, mm_ref[0], False)
                of_ref[pl.ds(cf, CHUNK), :] = o
                q, k, v, g = load(cb, 1)
                o, st_b = _hgrn_chunk(q, k, v, g, st_b, mm_ref[1], True)
                ob_ref[pl.ds(cb, CHUNK), :] = o
                return st_f, st_b

            st_f, st_b = lax.fori_loop(0, nc, step, st0, unroll=2)
            sfin_ref[s, 0] = st_f.T
            sfin_ref[s, 1] = st_b.T
        for s in range(TILE // seg_len, TILE // seq):
            sfin_ref[s] = jnp.zeros((2, HEAD, HEAD), F32)
        o = _rms(of_ref[...] + ob_ref[...], ng_ref[...]) * _silu(gb_ref[...].astype(F32))
        y_ref[...] = o.astype(BF16)

    @pl.when(i < n_pt)
    def _():
        body(seq, False)

    @pl.when(i >= n_pt)
    def _():
        body(TILE, True)


def _mixer_b(y_in, lb_logits, norm_g, state, mmats, layer, n_pt, seq):
    m = y_in.shape[0]
    depth = lb_logits.shape[0]
    nt = m // TILE
    spt = TILE // seq
    req = lambda i: jnp.maximum(i - n_pt, 0)
    slab = lambda off: pl.BlockSpec((TILE, HEAD), lambda i, h: (i, off // HEAD + h))
    return pl.pallas_call(
        functools.partial(_mixb_kernel, n_pt, seq, layer),
        out_shape=(jax.ShapeDtypeStruct((m, D_MIX), BF16),
                   jax.ShapeDtypeStruct((nt, spt, 2, N_HEADS, HEAD, HEAD), F32)),
        grid=(nt, N_HEADS),
        in_specs=[
            slab(OFF_QB), slab(OFF_FZF), slab(OFF_FZB), slab(OFF_IB), slab(OFF_GB),
            pl.BlockSpec((depth, 2, 1, HEAD), lambda i, h: (0, 0, 0, h)),
            pl.BlockSpec((None, None, 2, None, HEAD, HEAD), lambda i, h: (req(i), layer, 0, h, 0, 0)),
            pl.BlockSpec((None, 1, HEAD), lambda i, h: (layer, 0, 0)),
            pl.BlockSpec(mmats.shape, lambda i, h: (0, 0, 0)),
        ],
        out_specs=(pl.BlockSpec((TILE, HEAD), lambda i, h: (i, h)),
                   pl.BlockSpec((None, spt, 2, None, HEAD, HEAD), lambda i, h: (i, 0, 0, h, 0, 0))),
        scratch_shapes=[pltpu.VMEM((TILE, HEAD), F32)] * 2,
        compiler_params=_cparams(("parallel", "parallel")),
        name="mixer_b",
    )(y_in, y_in, y_in, y_in, y_in, lb_logits.reshape(depth, 2, 1, D_MIX), state,
      norm_g.reshape(depth, 1, HEAD), mmats)


def _rope(x, cos, sin_lo, sin_hi):
    return x * cos + pltpu.roll(x, HEAD - ROPE_FREQS, 1) * sin_lo + pltpu.roll(x, ROPE_FREQS, 1) * sin_hi


def _attn_kernel(n_pt, seq, q_ref, k_ref, v_ref, ck_ref, cv_ref, cos_ref, slo_ref, shi_ref, qg_ref, kg_ref,
                 y_ref, kn_ref, qs_ref, ks_ref, vs_ref):
    i = pl.program_id(0)
    scale = HEAD ** -0.5

    def body(seg_len, ctx):
        kn = _rms(k_ref[...], kg_ref[...])
        kn_ref[...] = kn
        if ctx:
            kn = _rope(kn, cos_ref[...], slo_ref[...], shi_ref[...])
        ks_ref[...] = kn.astype(BF16)
        vs_ref[...] = v_ref[...].astype(BF16)
        for g in range(GROUP):
            qn = _rms(q_ref[:, g * HEAD:(g + 1) * HEAD].astype(F32), qg_ref[...])
            if ctx:
                qn = _rope(qn, cos_ref[...], slo_ref[...], shi_ref[...])
            qs_ref[g] = (qn * scale).astype(BF16)
        if ctx:
            ckb = ck_ref[...].astype(BF16)
            cvb = cv_ref[...].astype(BF16)
        for s in range(TILE // seg_len):
            base = s * seg_len

            def qblock(j, carry):
                r0 = pl.multiple_of(base + j * Q_BLK, Q_BLK)
                rows = pl.ds(r0, Q_BLK)
                qcat = jnp.concatenate([qs_ref[g, rows, :] for g in range(GROUP)], axis=0)
                keys = ks_ref[pl.ds(base, seg_len), :]
                vals = vs_ref[pl.ds(base, seg_len), :]
                sc = _dot_nt(qcat, keys)
                mx = jnp.max(sc, axis=-1, keepdims=True)
                if ctx:
                    sc_c = _dot_nt(qcat, ckb)
                    mx = jnp.maximum(mx, jnp.max(sc_c, axis=-1, keepdims=True))
                p = jnp.exp(sc - mx)
                den = jnp.sum(p, axis=-1, keepdims=True)
                o = _dot(p.astype(BF16), vals)
                if ctx:
                    pc = jnp.exp(sc_c - mx)
                    den = den + jnp.sum(pc, axis=-1, keepdims=True)
                    o = o + _dot(pc.astype(BF16), cvb)
                o = o / den
                for g in range(GROUP):
                    y_ref[rows, g * HEAD:(g + 1) * HEAD] = o[g * Q_BLK:(g + 1) * Q_BLK].astype(BF16)
                return carry

            lax.fori_loop(0, seg_len // Q_BLK, qblock, 0, unroll=2)

    @pl.when(i < n_pt)
    def _():
        body(seq, False)

    @pl.when(i >= n_pt)
    def _():
        body(TILE, True)


def _attention(y_in, kv, cache_k, cache_v, rope_tabs, q_norm_g, k_norm_g, layer, n_pt, seq):
    m = y_in.shape[0]
    depth = q_norm_g.shape[0]
    nt = m // TILE
    gw = GROUP * HEAD
    past = cache_k.shape[2]
    req = lambda i: jnp.maximum(i - n_pt, 0)
    ctx_spec = pl.BlockSpec((None, None, past, HEAD), lambda i, h: (req(i), layer, 0, h))
    tab = pl.BlockSpec((TILE, HEAD), lambda i, h: (0, 0))
    gspec = pl.BlockSpec((None, 1, HEAD), lambda i, h: (layer, 0, 0))
    return pl.pallas_call(
        functools.partial(_attn_kernel, n_pt, seq),
        out_shape=(jax.ShapeDtypeStruct((m, D_MIX), BF16),
                   jax.ShapeDtypeStruct((m, KV_DIM), F32)),
        grid=(nt, KV_HEADS),
        in_specs=[
            pl.BlockSpec((TILE, gw), lambda i, h: (i, OFF_QC // gw + h)),
            pl.BlockSpec((TILE, HEAD), lambda i, h: (i, h)),
            pl.BlockSpec((TILE, HEAD), lambda i, h: (i, KV_HEADS + h)),
            ctx_spec, ctx_spec, tab, tab, tab, gspec, gspec,
        ],
        out_specs=(pl.BlockSpec((TILE, gw), lambda i, h: (i, h)),
                   pl.BlockSpec((TILE, HEAD), lambda i, h: (i, h))),
        scratch_shapes=[pltpu.VMEM((GROUP, TILE, HEAD), BF16), pltpu.VMEM((TILE, HEAD), BF16),
                        pltpu.VMEM((TILE, HEAD), BF16)],
        compiler_params=_cparams(("parallel", "parallel")),
        name="attention",
    )(y_in, kv, kv, cache_k, cache_v, *rope_tabs, q_norm_g.reshape(depth, 1, HEAD),
      k_norm_g.reshape(depth, 1, HEAD))


def _rope_tables(n):
    rows = n // GRID_W
    r = jnp.repeat(jnp.arange(rows, dtype=F32), GRID_W)
    col = jnp.tile(jnp.arange(GRID_W, dtype=F32), rows)
    inv = ROPE_BASE ** (-jnp.arange(ROPE_FREQS, dtype=F32) / ROPE_FREQS)
    ar = r[:, None] * inv
    ac = col[:, None] * inv
    ang = jnp.concatenate([ar, ar, ac, ac], axis=-1)
    cos, sin = jnp.cos(ang), jnp.sin(ang)
    first_half = (jnp.arange(HEAD) // ROPE_FREQS) % 2 == 0
    return cos, jnp.where(first_half, -sin, 0.0), jnp.where(first_half, 0.0, sin)


def _merge_kernel(ya_ref, yb_ref, yc_ref, wa_ref, wb_ref, wc_ref, ga_ref, gb_ref, gc_ref, o_ref):
    acc = _sig(ga_ref[...].astype(F32)) * _dot(ya_ref[...], wa_ref[...].astype(BF16))
    acc = acc + _sig(gb_ref[...].astype(F32)) * _dot(yb_ref[...], wb_ref[...].astype(BF16))
    acc = acc + _sig(gc_ref[...].astype(F32)) * _dot(yc_ref[...], wc_ref[...].astype(BF16))
    o_ref[...] = acc.astype(BF16)


def _merge(y_a, y_b, y_c, y_in, w_a, w_b, w_c, layer):
    m = y_in.shape[0]
    tn = 512
    act = pl.BlockSpec((TILE, D_MIX), lambda i, j: (i, 0))
    wsp = pl.BlockSpec((None, D_MIX, tn), lambda i, j: (layer, 0, j))
    gate = lambda off: pl.BlockSpec((TILE, tn), lambda i, j: (i, off // tn + j))
    return pl.pallas_call(
        _merge_kernel,
        out_shape=jax.ShapeDtypeStruct((m, D_MODEL), BF16),
        grid=(m // TILE, D_MODEL // tn),
        in_specs=[act, act, act, wsp, wsp, wsp, gate(OFF_GA), gate(OFF_GB2), gate(OFF_GC)],
        out_specs=pl.BlockSpec((TILE, tn), lambda i, j: (i, j)),
        compiler_params=_cparams(("parallel", "parallel")),
        name="merge",
    )(y_a, y_b, y_c, w_a, w_b, w_c, y_in, y_in, y_in)


def _out_proj_kernel(mg_ref, w_ref, x_ref, g1_ref, xo_ref):
    xo_ref[...] = x_ref[...] + g1_ref[...] * _dot(mg_ref[...], w_ref[...].astype(BF16))


def _out_proj(merged, w_out, x, mod, layer, n_pt):
    m = x.shape[0]
    tn = 512
    row = _mod_row(TILE, n_pt)
    return pl.pallas_call(
        _out_proj_kernel,
        out_shape=jax.ShapeDtypeStruct((m, D_MODEL), F32),
        grid=(m // TILE, D_MODEL // tn),
        in_specs=[
            pl.BlockSpec((TILE, D_MODEL), lambda i, j: (i, 0)),
            pl.BlockSpec((None, D_MODEL, tn), lambda i, j: (layer, 0, j)),
            pl.BlockSpec((TILE, tn), lambda i, j: (i, j)),
            pl.BlockSpec((None, None, 1, tn), lambda i, j: (row(i), 2, 0, j)),
        ],
        out_specs=pl.BlockSpec((TILE, tn), lambda i, j: (i, j)),
        compiler_params=_cparams(("parallel", "parallel")),
        name="out_proj",
    )(merged, w_out, x, mod)


def _prefix_matrices(seq):
    t = np.arange(TILE)
    before = t[:, None] < t[None, :]
    same = (t[:, None] // seq) == (t[None, :] // seq)
    return np.stack([before & same, before]).astype(np.float32)


def _route_kernel(n_pt, seq, lg_ref, u_ref, tri_ref, xin_ref, pt_ref, gate_ref, p_ref):
    i = pl.program_id(0)

    def body(seg_len):
        n_seg = TILE // seg_len
        cap = EC_FACTOR * seg_len // N_EXPERTS
        lt = lg_ref[...].T[0:N_EXPERTS]
        ex = jnp.exp(lt - jnp.max(lt, axis=0, keepdims=True))
        aff = ex / jnp.sum(ex, axis=0, keepdims=True)
        bits = pltpu.bitcast(aff, jnp.int32)
        segs = [bits[:, s * seg_len:(s + 1) * seg_len] for s in range(n_seg)]

        def count_ge(seg, thr):
            return jnp.sum(jnp.where(seg >= thr, 1.0, 0.0), axis=1, keepdims=True)

        def bisect(_, carry):
            out = []
            for seg, (lo, hi) in zip(segs, carry):
                mid = lo + ((hi - lo + 1) >> 1)
                ok = count_ge(seg, mid) >= cap
                out.append((jnp.where(ok, mid, lo), jnp.where(ok, hi, mid - 1)))
            return tuple(out)

        init = tuple((jnp.zeros((N_EXPERTS, 1), jnp.int32),
                      jnp.full((N_EXPERTS, 1), F32_BITS_MAX_FINITE, jnp.int32)) for _ in segs)
        bounds = lax.fori_loop(0, 31, bisect, init)
        gt_l, eq_l, need_l = [], [], []
        for seg, (thr, _) in zip(segs, bounds):
            gt = seg > thr
            gt_l.append(jnp.where(gt, 1.0, 0.0))
            eq_l.append(jnp.where(seg == thr, 1.0, 0.0))
            n_gt = jnp.sum(gt_l[-1], axis=1, keepdims=True)
            need_l.append(jnp.broadcast_to(cap - n_gt, (N_EXPERTS, seg_len)))
        gt = jnp.concatenate(gt_l, axis=1)
        eq = jnp.concatenate(eq_l, axis=1)
        need = jnp.concatenate(need_l, axis=1)
        before = _dot(jnp.concatenate([gt, eq], axis=0).astype(BF16), tri_ref[...])
        gt_before, eq_before = before[0:N_EXPERTS], before[N_EXPERTS:]
        chosen = jnp.logical_or(gt > 0.0, jnp.logical_and(eq > 0.0, eq_before < need))
        tok = lax.broadcasted_iota(jnp.int32, (N_EXPERTS, TILE), 1)
        seg_off = ((tok // seg_len) * cap).astype(F32)
        slot = jnp.where(chosen, gt_before + jnp.minimum(eq_before, need) + seg_off, -1.0)
        slot_t = jnp.concatenate([slot, jnp.full((E_PAD - N_EXPERTS, TILE), -1.0, F32)], axis=0).T
        sub = lax.broadcasted_iota(jnp.int32, (SLOTS, TILE), 0).astype(F32)
        lane = lax.broadcasted_iota(jnp.int32, (TILE, SLOTS), 1).astype(F32)
        for e in range(N_EXPERTS):
            hit = slot[e:e + 1] == sub
            gate_ref[e] = jnp.sum(jnp.where(hit, aff[e:e + 1], 0.0), axis=1, keepdims=True)
            p_ref[e * SLOTS:(e + 1) * SLOTS, :] = jnp.where(hit, 1.0, 0.0).astype(BF16)
            hit_t = slot_t[:, e:e + 1] == lane
            pt_ref[:, e * SLOTS:(e + 1) * SLOTS] = jnp.where(hit_t, 1.0, 0.0).astype(BF16)
        tn = 512
        for n in range(D_MODEL // tn):
            rows = _dot(p_ref[...], u_ref[:, n * tn:(n + 1) * tn])
            xin_ref[:, :, n * tn:(n + 1) * tn] = rows.astype(BF16).reshape(N_EXPERTS, SLOTS, tn)

    @pl.when(i < n_pt)
    def _():
        body(seq)

    @pl.when(i >= n_pt)
    def _():
        body(TILE)


def _route(logits, u, tri, n_pt, seq):
    m = u.shape[0]
    nt = m // TILE
    return pl.pallas_call(
        functools.partial(_route_kernel, n_pt, seq),
        out_shape=(jax.ShapeDtypeStruct((N_EXPERTS, nt * SLOTS, D_MODEL), BF16),
                   jax.ShapeDtypeStruct((m, N_EXPERTS * SLOTS), BF16),
                   jax.ShapeDtypeStruct((N_EXPERTS, nt * SLOTS, 1), F32)),
        grid=(nt,),
        in_specs=[pl.BlockSpec((TILE, E_PAD), lambda i: (i, 0)),
                  pl.BlockSpec((TILE, D_MODEL), lambda i: (i, 0)),
                  pl.BlockSpec((None, TILE, TILE), lambda i: (jnp.where(i < n_pt, 0, 1), 0, 0))],
        out_specs=(pl.BlockSpec((N_EXPERTS, SLOTS, D_MODEL), lambda i: (0, i, 0)),
                   pl.BlockSpec((TILE, N_EXPERTS * SLOTS), lambda i: (i, 0)),
                   pl.BlockSpec((N_EXPERTS, SLOTS, 1), lambda i: (0, i, 0))),
        scratch_shapes=[pltpu.VMEM((N_EXPERTS * SLOTS, TILE), BF16)],
        compiler_params=_cparams(("parallel",)),
        name="route",
    )(logits, u, tri)


def _expert_kernel(n_f, x_ref, wg_ref, wu_ref, wd_ref, gate_ref, o_ref, h_ref):
    s = pl.program_id(1)
    tf = wg_ref.shape[-1]

    @pl.when(s < n_f)
    def _():
        x = x_ref[...]
        h = _silu(_dot(x, wg_ref[...].astype(BF16))) * _dot(x, wu_ref[...].astype(BF16))
        h_ref[s] = h.astype(BF16)

    @pl.when(s >= n_f)
    def _():
        acc = _dot(h_ref[0], wd_ref[0:tf, :].astype(BF16))
        for k in range(1, n_f):
            acc = acc + _dot(h_ref[k], wd_ref[k * tf:(k + 1) * tf, :].astype(BF16))
        o_ref[...] = (acc * gate_ref[...]).astype(BF16)


def _experts(xin, gates, w_gate, w_up, w_down, layer):
    _, rows, _ = xin.shape
    tf, tn = 256, 512
    n_f, n_o = D_EXPERT // tf, D_MODEL // tn
    f_idx = lambda s: jnp.minimum(s, n_f - 1)
    o_idx = lambda s: jnp.maximum(s - n_f, 0)
    return pl.pallas_call(
        functools.partial(_expert_kernel, n_f),
        out_shape=jax.ShapeDtypeStruct((N_EXPERTS, rows, D_MODEL), BF16),
        grid=(N_EXPERTS, n_f + n_o),
        in_specs=[
            pl.BlockSpec((None, rows, D_MODEL), lambda e, s: (e, 0, 0)),
            pl.BlockSpec((None, None, D_MODEL, tf), lambda e, s: (layer, e, 0, f_idx(s))),
            pl.BlockSpec((None, None, D_MODEL, tf), lambda e, s: (layer, e, 0, f_idx(s))),
            pl.BlockSpec((None, None, D_EXPERT, tn), lambda e, s: (layer, e, 0, o_idx(s))),
            pl.BlockSpec((None, rows, 1), lambda e, s: (e, 0, 0)),
        ],
        out_specs=pl.BlockSpec((None, rows, tn), lambda e, s: (e, 0, o_idx(s))),
        scratch_shapes=[pltpu.VMEM((n_f, rows, tf), BF16)],
        compiler_params=_cparams(("parallel", "arbitrary")),
        name="experts",
    )(xin, w_gate, w_up, w_down, gates)


def _scatter_kernel(pt_ref, y_ref, x_ref, g2_ref, o_ref):
    tn = y_ref.shape[-1]
    y = y_ref[...].reshape(pt_ref.shape[1], tn)
    o_ref[...] = x_ref[...] + g2_ref[...] * _dot(pt_ref[...], y)


def _scatter(pt, y, x, mod, n_pt):
    m = x.shape[0]
    tn = 512
    row = _mod_row(TILE, n_pt)
    return pl.pallas_call(
        _scatter_kernel,
        out_shape=jax.ShapeDtypeStruct((m, D_MODEL), F32),
        grid=(m // TILE, D_MODEL // tn),
        in_specs=[
            pl.BlockSpec((TILE, N_EXPERTS * SLOTS), lambda i, j: (i, 0)),
            pl.BlockSpec((N_EXPERTS, SLOTS, tn), lambda i, j: (0, i, j)),
            pl.BlockSpec((TILE, tn), lambda i, j: (i, j)),
            pl.BlockSpec((None, None, 1, tn), lambda i, j: (row(i), 5, 0, j)),
        ],
        out_specs=pl.BlockSpec((TILE, tn), lambda i, j: (i, j)),
        compiler_params=_cparams(("parallel", "parallel")),
        name="scatter",
    )(pt, y, x, mod)


def _final_norm_kernel(x_ref, g_ref, o_ref):
    o_ref[...] = _rms(x_ref[...], g_ref[...])


def _final_norm(x, g, first_tile, n_tiles):
    tm = 512
    per = TILE // tm
    return pl.pallas_call(
        _final_norm_kernel,
        out_shape=jax.ShapeDtypeStruct((n_tiles * TILE, D_MODEL), F32),
        grid=(n_tiles * per,),
        in_specs=[pl.BlockSpec((tm, D_MODEL), lambda i: (i + first_tile * per, 0)),
                  pl.BlockSpec((1, D_MODEL), lambda i: (0, 0))],
        out_specs=pl.BlockSpec((tm, D_MODEL), lambda i: (i, 0)),
        compiler_params=_cparams(("parallel",)),
        name="final_norm",
    )(x, g.reshape(1, D_MODEL))


def kernel(x_prompt, x_sample, cache_k, cache_v, state_rglru, state_hgrn, c, c_ctx, w_ada, b_ada, norm1_g, norm2_g, w_in, conv_w, conv_b, rg_wr, rg_br, rg_wi, rg_bi, rg_lambda, hgrn_lb_logits, hgrn_norm_g, q_norm_g, k_norm_g, w_branch_a, w_branch_b, w_branch_c, w_out, w_router, w_exp_gate, w_exp_up, w_exp_down, final_norm_g):
    bp, seq, _ = x_prompt.shape
    bs, dec_seq, _ = x_sample.shape
    depth = w_in.shape[0]
    past = cache_k.shape[2]
    m = bp * seq + bs * dec_seq
    assert dec_seq == TILE and TILE % seq == 0 and (bp * seq) % TILE == 0 and seq & (seq - 1) == 0
    assert seq % Q_BLK == 0 and past == seq and m % BIG_TILE == 0
    n_pt = bp * seq // TILE

    x = jnp.concatenate([x_prompt.reshape(bp * seq, D_MODEL), x_sample.reshape(bs * dec_seq, D_MODEL)], axis=0)
    n_cond = -(-(1 + bs) // 8) * 8
    cond = jnp.concatenate([c_ctx[None], c, jnp.zeros((n_cond - 1 - bs, D_MODEL), F32)], axis=0)
    mod_all = _ada_mod(cond, w_ada, b_ada).reshape(depth, n_cond, N_SUB, 1, D_MODEL)
    rope_tabs = _rope_tables(dec_seq)
    mmats = jnp.asarray(_hgrn_matrices(), BF16)
    tri = jnp.asarray(_prefix_matrices(seq), BF16)
    w_router_pad = jnp.pad(w_router, ((0, 0), (0, 0), (0, E_PAD - N_EXPERTS)))
    ck = cache_k.reshape(bs, depth, past, KV_DIM)
    cv = cache_v.reshape(bs, depth, past, KV_DIM)
    n1 = norm1_g.reshape(depth, 1, D_MODEL)
    n2 = norm2_g.reshape(depth, 1, D_MODEL)

    k_list, v_list, ra_list, hb_list = [], [], [], []
    for l in range(depth):
        mod = mod_all[l]
        u1 = _norm_mod(x, n1, mod, l, n_pt, 0)
        y_in, kv = _in_proj(u1, w_in, l)
        y_a, fin_a = _mixer_a(y_in, conv_w, conv_b, rg_wr, rg_br, rg_wi, rg_bi, rg_lambda, state_rglru, l, n_pt, seq)
        y_b, fin_b = _mixer_b(y_in, hgrn_lb_logits, hgrn_norm_g, state_hgrn, mmats, l, n_pt, seq)
        y_c, k_n = _attention(y_in, kv, ck, cv, rope_tabs, q_norm_g, k_norm_g, l, n_pt, seq)
        merged = _merge(y_a, y_b, y_c, y_in, w_branch_a, w_branch_b, w_branch_c, l)
        x = _out_proj(merged, w_out, x, mod, l, n_pt)
        u2, logits = _norm_mod(x, n2, mod, l, n_pt, 1, w_router_pad)
        xin, pt, gates = _route(logits, u2, tri, n_pt, seq)
        y_e = _experts(xin, gates, w_exp_gate, w_exp_up, w_exp_down, l)
        x = _scatter(pt, y_e, x, mod, n_pt)
        k_list.append(k_n[:bp * seq].reshape(bp, seq, KV_HEADS, HEAD))
        v_list.append(kv[:bp * seq, KV_DIM:].reshape(bp, seq, KV_HEADS, HEAD))
        ra_list.append(fin_a[:n_pt].reshape(bp, 2, D_MIX))
        hb_list.append(fin_b[:n_pt].reshape(bp, 2, N_HEADS, HEAD, HEAD))

    y_prompt = _final_norm(x, final_norm_g, 0, n_pt).reshape(bp, seq, D_MODEL)
    y_sample = _final_norm(x, final_norm_g, n_pt, bs).reshape(bs, dec_seq, D_MODEL)
    return (y_prompt, y_sample, jnp.stack(k_list, axis=1), jnp.stack(v_list, axis=1),
            jnp.stack(ra_list, axis=1), jnp.stack(hb_list, axis=1))
```

```python
import functools

import numpy as np
import jax
import jax.numpy as jnp
from jax import lax
from jax.experimental import pallas as pl
from jax.experimental.pallas import tpu as pltpu

F32 = jnp.float32
BF16 = jnp.bfloat16

D_MODEL = 2048
D_MIX = D_MODEL // 2
HEAD = 128
N_HEADS = D_MIX // HEAD
CONV_W = 4
RG_C = 8.0
CHUNK = 64
N_LEVELS = 6
KV_HEADS = 2
KV_DIM = KV_HEADS * HEAD
GROUP = N_HEADS // KV_HEADS
ROPE_FREQS = HEAD // 4
ROPE_BASE = 10000.0
GRID_W = 64
N_EXPERTS = 16
EC_FACTOR = 2
D_EXPERT = D_MODEL // 2
N_SUB = 6
EPS = 1e-6
TILE = 1024
BIG_TILE = 2 * TILE
SLOTS = TILE * EC_FACTOR // N_EXPERTS
E_PAD = 128
Q_BLK = 128
Q_PAR = 2
HGRN_GROUP = 4
HGRN_ROWS = HGRN_GROUP * CHUNK
F32_BITS_MAX_FINITE = 0x7F7FFFFF
VMEM_LIMIT = 48 * 1024 * 1024

OFF_XA, OFF_YA, OFF_QB, OFF_FZF, OFF_FZB, OFF_IB, OFF_GB, OFF_QC = [i * D_MIX for i in range(8)]
OFF_KC = 8 * D_MIX
OFF_VC = OFF_KC + KV_DIM
OFF_GA = OFF_VC + KV_DIM
OFF_GB2 = OFF_GA + D_MODEL
OFF_GC = OFF_GB2 + D_MODEL
N_IN = OFF_GC + D_MODEL
IN_TN = 2 * KV_DIM


def _cparams(sem):
    return pltpu.CompilerParams(dimension_semantics=sem, vmem_limit_bytes=VMEM_LIMIT)


def _sig(x):
    return 0.5 * jnp.tanh(0.5 * x) + 0.5


def _silu(x):
    return x * _sig(x)


def _dot(a, b):
    return jnp.dot(a, b, preferred_element_type=F32)


def _dot_nt(a, b):
    return lax.dot_general(a, b, (((1,), (1,)), ((), ())), preferred_element_type=F32)


def _rms(x, g):
    ms = jnp.mean(x * x, axis=-1, keepdims=True)
    return x * lax.rsqrt(ms + EPS) * g


def _mod_row(tile_rows, n_pt):
    per = TILE // tile_rows if tile_rows <= TILE else None
    if per is None:
        mult = tile_rows // TILE
        return lambda i: jnp.maximum(i * mult - n_pt + 1, 0)
    return lambda i: jnp.maximum(i // per - n_pt + 1, 0)


def _ada_kernel(c_ref, w_ref, b_ref, o_ref):
    s = _silu(c_ref[...]).astype(BF16)
    o_ref[...] = _dot(s, w_ref[...].astype(BF16)) + b_ref[...]


def _ada_mod(cond, w_ada, b_ada):
    depth, _, n = w_ada.shape
    rows = cond.shape[0]
    tn = 1024
    return pl.pallas_call(
        _ada_kernel,
        out_shape=jax.ShapeDtypeStruct((depth, rows, n), F32),
        grid=(depth, n // tn),
        in_specs=[
            pl.BlockSpec((rows, D_MODEL), lambda l, j: (0, 0)),
            pl.BlockSpec((None, D_MODEL, tn), lambda l, j: (l, 0, j)),
            pl.BlockSpec((None, 1, tn), lambda l, j: (l, 0, j)),
        ],
        out_specs=pl.BlockSpec((None, rows, tn), lambda l, j: (l, 0, j)),
        compiler_params=_cparams(("parallel", "parallel")),
        name="ada_mod",
    )(cond, w_ada, b_ada.reshape(depth, 1, n))


def _norm_mod_kernel(x_ref, g_ref, sc_ref, sh_ref, u_ref):
    u_ref[...] = (_rms(x_ref[...], g_ref[...]) * (1.0 + sc_ref[...]) + sh_ref[...]).astype(BF16)


def _norm_router_kernel(x_ref, g_ref, sc_ref, sh_ref, wr_ref, u_ref, lg_ref):
    u = _rms(x_ref[...], g_ref[...]) * (1.0 + sc_ref[...]) + sh_ref[...]
    u_ref[...] = u.astype(BF16)
    w = wr_ref[...]
    uh = u.astype(BF16)
    ul = (u - uh.astype(F32)).astype(BF16)
    wh = w.astype(BF16)
    wl = (w - wh.astype(F32)).astype(BF16)
    lg_ref[...] = _dot(uh, wh) + (_dot(uh, wl) + _dot(ul, wh))


def _norm_mod(x, norm_g, mod, layer, n_pt, sub, w_router_pad=None):
    m = x.shape[0]
    tm = 512
    row = _mod_row(tm, n_pt)
    mspec = lambda c: pl.BlockSpec((None, None, 1, D_MODEL), lambda i: (row(i), c, 0, 0))
    in_specs = [pl.BlockSpec((tm, D_MODEL), lambda i: (i, 0)),
                pl.BlockSpec((None, 1, D_MODEL), lambda i: (layer, 0, 0)),
                mspec(3 * sub + 1), mspec(3 * sub)]
    u_shape = jax.ShapeDtypeStruct((m, D_MODEL), BF16)
    u_spec = pl.BlockSpec((tm, D_MODEL), lambda i: (i, 0))
    if w_router_pad is None:
        return pl.pallas_call(
            _norm_mod_kernel, out_shape=u_shape, grid=(m // tm,), in_specs=in_specs, out_specs=u_spec,
            compiler_params=_cparams(("parallel",)), name="norm_mod",
        )(x, norm_g, mod, mod)
    return pl.pallas_call(
        _norm_router_kernel,
        out_shape=(u_shape, jax.ShapeDtypeStruct((m, E_PAD), F32)),
        grid=(m // tm,),
        in_specs=in_specs + [pl.BlockSpec((None, D_MODEL, E_PAD), lambda i: (layer, 0, 0))],
        out_specs=(u_spec, pl.BlockSpec((tm, E_PAD), lambda i: (i, 0))),
        compiler_params=_cparams(("parallel",)), name="norm_router",
    )(x, norm_g, mod, mod, w_router_pad)


def _in_proj_kernel(kv_tile, u_ref, w_ref, o_ref, kv_ref):
    acc = _dot(u_ref[...], w_ref[...].astype(BF16))
    o_ref[...] = acc.astype(BF16)

    @pl.when(pl.program_id(1) == kv_tile)
    def _():
        kv_ref[...] = acc


def _in_proj(u, w_in, layer):
    m = u.shape[0]
    assert OFF_KC % IN_TN == 0 and N_IN % IN_TN == 0
    return pl.pallas_call(
        functools.partial(_in_proj_kernel, OFF_KC // IN_TN),
        out_shape=(jax.ShapeDtypeStruct((m, N_IN), BF16), jax.ShapeDtypeStruct((m, IN_TN), F32)),
        grid=(m // BIG_TILE, N_IN // IN_TN),
        in_specs=[
            pl.BlockSpec((BIG_TILE, D_MODEL), lambda i, j: (i, 0)),
            pl.BlockSpec((None, D_MODEL, IN_TN), lambda i, j: (layer, 0, j)),
        ],
        out_specs=(pl.BlockSpec((BIG_TILE, IN_TN), lambda i, j: (i, j)),
                   pl.BlockSpec((BIG_TILE, IN_TN), lambda i, j: (i, 0))),
        compiler_params=_cparams(("parallel", "arbitrary")),
        name="in_proj",
    )(u, w_in)


def _softplus(z):
    return jnp.maximum(z, 0.0) + jnp.log1p(jnp.exp(-jnp.abs(z)))


def _gelu_tanh(x):
    return 0.5 * x * (1.0 + jnp.tanh(np.sqrt(2.0 / np.pi).astype(np.float32) * (x + 0.044715 * (x * x * x))))


def _mixa_kernel(n_pt, seq, xa_ref, ya_ref, cw_ref, cb_ref, wr_ref, br_ref, wi_ref, bi_ref, lam_ref, h0_ref,
                 y_ref, fin_ref, af_ref, bf_ref, ab_ref, bb_ref, hf_ref, hb_ref):
    i = pl.program_id(0)

    def body(seg_len, use_state):
        x = xa_ref[...].astype(F32)
        pos = lax.broadcasted_iota(jnp.int32, (TILE, HEAD), 0) & (seg_len - 1)
        cw = cw_ref[...]
        xc = (jnp.where(pos >= 1, pltpu.roll(x, 1, 0), 0.0) * cw[0:1]
              + x * cw[1:2]
              + jnp.where(pos < seg_len - 1, pltpu.roll(x, TILE - 1, 0), 0.0) * cw[2:3]
              + jnp.where(pos < seg_len - 2, pltpu.roll(x, TILE - 2, 0), 0.0) * cw[3:4]
              + cb_ref[...])
        xcb = xc.astype(BF16)
        for d, (a_ref, b_ref) in enumerate(((af_ref, bf_ref), (ab_ref, bb_ref))):
            r = _sig(_dot(xcb, wr_ref[d].astype(BF16)) + br_ref[d])
            ig = _sig(_dot(xcb, wi_ref[d].astype(BF16)) + bi_ref[d])
            log_a = (-RG_C) * r * _softplus(-lam_ref[d])
            a = jnp.exp(log_a)
            a_ref[...] = a
            b_ref[...] = jnp.sqrt(-jnp.tanh(log_a) * (a * a + 1.0)) * (ig * xc)

        nb = seg_len // 8
        rowi = lax.broadcasted_iota(jnp.int32, (8, HEAD), 0)
        n_seg = TILE // seg_len
        for s in range(n_seg):
            base = s * seg_len
            if use_state:
                c0 = (h0_ref[0:1, :], h0_ref[1:2, :])
            else:
                c0 = (jnp.zeros((1, HEAD), F32), jnp.zeros((1, HEAD), F32))

            def step(j, carry):
                cf, cb = carry
                of = pl.multiple_of(base + j * 8, 8)
                ob = pl.multiple_of(base + (nb - 1 - j) * 8, 8)
                a = af_ref[pl.ds(of, 8), :]
                b = bf_ref[pl.ds(of, 8), :]
                for d in (1, 2, 4):
                    m = rowi >= d
                    b = jnp.where(m, a * pltpu.roll(b, d, 0) + b, b)
                    a = jnp.where(m, a * pltpu.roll(a, d, 0), a)
                h = b + a * cf
                hf_ref[pl.ds(of, 8), :] = h
                cf = h[7:8, :]
                a = ab_ref[pl.ds(ob, 8), :]
                b = bb_ref[pl.ds(ob, 8), :]
                for d in (1, 2, 4):
                    m = rowi < 8 - d
                    b = jnp.where(m, a * pltpu.roll(b, 8 - d, 0) + b, b)
                    a = jnp.where(m, a * pltpu.roll(a, 8 - d, 0), a)
                h = b + a * cb
                hb_ref[pl.ds(ob, 8), :] = h
                cb = h[0:1, :]
                return cf, cb

            cf, cb = lax.fori_loop(0, nb, step, c0, unroll=4)
            fin_ref[s, 0:1, :] = cf
            fin_ref[s, 1:2, :] = cb
        for s in range(n_seg, TILE // seq):
            fin_ref[s] = jnp.zeros((2, HEAD), F32)
        y_ref[...] = ((hf_ref[...] + hb_ref[...]) * _gelu_tanh(ya_ref[...].astype(F32))).astype(BF16)

    @pl.when(i < n_pt)
    def _():
        body(seq, False)

    @pl.when(i >= n_pt)
    def _():
        body(TILE, True)


def _mixer_a(y_in, conv_w, conv_b, rg_wr, rg_br, rg_wi, rg_bi, rg_lambda, state, layer, n_pt, seq):
    m = y_in.shape[0]
    depth = conv_w.shape[0]
    nt = m // TILE
    spt = TILE // seq
    req = lambda i: jnp.maximum(i - n_pt, 0)
    vec = lambda a: a.reshape(depth, 2, 1, D_MIX)
    gate_w = pl.BlockSpec((None, 2, None, HEAD, HEAD), lambda i, h: (layer, 0, h, 0, 0))
    gate_b = pl.BlockSpec((None, 2, 1, HEAD), lambda i, h: (layer, 0, 0, h))
    slab = lambda off: pl.BlockSpec((TILE, HEAD), lambda i, h: (i, off // HEAD + h))
    return pl.pallas_call(
        functools.partial(_mixa_kernel, n_pt, seq),
        out_shape=(jax.ShapeDtypeStruct((m, D_MIX), BF16),
                   jax.ShapeDtypeStruct((nt, spt, 2, D_MIX), F32)),
        grid=(nt, N_HEADS),
        in_specs=[
            slab(OFF_XA), slab(OFF_YA),
            pl.BlockSpec((None, CONV_W, HEAD), lambda i, h: (layer, 0, h)),
            pl.BlockSpec((None, 1, HEAD), lambda i, h: (layer, 0, h)),
            gate_w, gate_b, gate_w, gate_b, gate_b,
            pl.BlockSpec((None, None, 2, HEAD), lambda i, h: (req(i), layer, 0, h)),
        ],
        out_specs=(pl.BlockSpec((TILE, HEAD), lambda i, h: (i, h)),
                   pl.BlockSpec((None, spt, 2, HEAD), lambda i, h: (i, 0, 0, h))),
        scratch_shapes=[pltpu.VMEM((TILE, HEAD), F32)] * 6,
        compiler_params=_cparams(("parallel", "parallel")),
        name="mixer_a",
    )(y_in, y_in, conv_w, conv_b.reshape(depth, 1, D_MIX), rg_wr, vec(rg_br), rg_wi, vec(rg_bi), vec(rg_lambda),
      state)


def _hgrn_matrices():
    c = CHUNK
    mats = []
    for reverse in (False, True):
        mm = np.zeros((2 + N_LEVELS, c, c), np.float32)
        for t in range(c):
            if not reverse:
                mm[0, t, :t + 1] = 1.0
                mm[1, t, t + 1:] = 1.0
            else:
                mm[0, t, t:] = 1.0
                mm[1, t, :t] = 1.0
            for lvl in range(N_LEVELS):
                m = c >> (lvl + 1)
                blk, p = divmod(t, 2 * m)
                if not reverse:
                    ref = blk * 2 * m + m - 1
                    if p >= m:
                        mm[2 + lvl, t, ref + 1:t + 1] = 1.0
                    else:
                        mm[2 + lvl, t, t + 1:ref + 1] = 1.0
                else:
                    ref = blk * 2 * m + m
                    if p < m:
                        mm[2 + lvl, t, t:ref] = 1.0
                    else:
                        mm[2 + lvl, t, ref:t] = 1.0
        mats.append(mm.reshape((2 + N_LEVELS) * c, c))
    return np.stack(mats)


def _hgrn_decays(g, mmat):
    g_cat = jnp.concatenate([g[n * CHUNK:(n + 1) * CHUNK] for n in range(HGRN_GROUP)], axis=1)
    gh = g_cat.astype(BF16)
    gl = (g_cat - gh.astype(F32)).astype(BF16)
    return _dot(mmat, gh) + _dot(mmat, gl)


def _hgrn_chunk_local(q, k, v, e_all, reverse):
    c = CHUNK
    e_in = jnp.exp(e_all[0:c])
    e_out = jnp.exp(e_all[c:2 * c])
    vb = v.astype(BF16)
    rowi = lax.broadcasted_iota(jnp.int32, (c, HEAD), 0)
    ti = lax.broadcasted_iota(jnp.int32, (c, c), 0)
    si = lax.broadcasted_iota(jnp.int32, (c, c), 1)
    scores = jnp.where(ti == si, _dot_nt(q.astype(BF16), k.astype(BF16)), 0.0)
    for lvl in range(N_LEVELS):
        m = c >> (lvl + 1)
        e = jnp.exp(e_all[(2 + lvl) * c:(3 + lvl) * c])
        late = (rowi & m) != 0
        q_side = late if not reverse else jnp.logical_not(late)
        qd = jnp.where(q_side, q * e, 0.0).astype(BF16)
        kd = jnp.where(q_side, 0.0, k * e).astype(BF16)
        scores = scores + jnp.where((ti ^ si) < 2 * m, _dot_nt(qd, kd), 0.0)
    ke = (k * e_out).astype(BF16)
    own = lax.dot_general(vb, ke, (((0,), (0,)), ((), ())), preferred_element_type=F32)
    d_end = e_in[c - 1:c] if not reverse else e_in[0:1]
    return (q * e_in).astype(BF16), scores.astype(BF16), vb, own, d_end


def _mixb_kernel(n_pt, seq, layer, qb_ref, fzf_ref, fzb_ref, ib_ref, gb_ref, lbl_ref, s0_ref, ng_ref, mm_ref,
                 y_ref, sfin_ref, of_ref, ob_ref):
    i = pl.program_id(0)
    depth = lbl_ref.shape[0]

    def lower_bound(d):
        lg = [lbl_ref[j, d] for j in range(depth)]
        mx = functools.reduce(jnp.maximum, lg)
        ex = [jnp.exp(v - mx) for v in lg]
        tot = functools.reduce(lambda a, b: a + b, ex)
        acc = jnp.zeros_like(tot)
        for j in range(1, layer + 1):
            acc = acc + ex[j] / tot
        return acc

    def body(seg_len, use_state):
        lbs = (lower_bound(0), lower_bound(1))
        n_grp = seg_len // HGRN_ROWS
        chunk_rows = lambda a, n: a[n * CHUNK:(n + 1) * CHUNK]

        def load(r0, d):
            rows = pl.ds(r0, HGRN_ROWS)
            q = _silu(qb_ref[rows, :].astype(F32))
            v = ib_ref[rows, :].astype(F32)
            fz = (fzf_ref if d == 0 else fzb_ref)[rows, :].astype(F32)
            f = lbs[d] + (1.0 - lbs[d]) / (1.0 + jnp.exp(-fz))
            return q, 1.0 - f, v, jnp.log(f)

        def group(r_f, r_b, st_f, st_b):
            qf, kf, vf, gf = load(r_f, 0)
            qb, kb, vb, gb = load(r_b, 1)
            ef = _hgrn_decays(gf, mm_ref[0])
            eb = _hgrn_decays(gb, mm_ref[1])
            loc_f = [_hgrn_chunk_local(chunk_rows(qf, n), chunk_rows(kf, n), chunk_rows(vf, n),
                                       ef[:, n * HEAD:(n + 1) * HEAD], False) for n in range(HGRN_GROUP)]
            loc_b = [_hgrn_chunk_local(chunk_rows(qb, n), chunk_rows(kb, n), chunk_rows(vb, n),
                                       eb[:, n * HEAD:(n + 1) * HEAD], True) for n in range(HGRN_GROUP)]
            for n in range(HGRN_GROUP):
                qe, sc, vals, own, d_end = loc_f[n]
                of_ref[pl.ds(r_f + n * CHUNK, CHUNK), :] = _dot_nt(qe, st_f.astype(BF16)) + _dot(sc, vals)
                st_f = st_f * d_end + own
            for n in reversed(range(HGRN_GROUP)):
                qe, sc, vals, own, d_end = loc_b[n]
                ob_ref[pl.ds(r_b + n * CHUNK, CHUNK), :] = _dot_nt(qe, st_b.astype(BF16)) + _dot(sc, vals)
                st_b = st_b * d_end + own
            return st_f, st_b

        if use_state:
            def step(j, carry):
                r_f = pl.multiple_of(j * HGRN_ROWS, HGRN_ROWS)
                r_b = pl.multiple_of((n_grp - 1 - j) * HGRN_ROWS, HGRN_ROWS)
                return group(r_f, r_b, *carry)

            st_f, st_b = lax.fori_loop(0, n_grp, step, (s0_ref[0].T, s0_ref[1].T))
            sfin_ref[0, 0] = st_f.T
            sfin_ref[0, 1] = st_b.T
            for s in range(1, TILE // seq):
                sfin_ref[s] = jnp.zeros((2, HEAD, HEAD), F32)
        else:
            def request(s, carry):
                r0 = pl.multiple_of(s * HGRN_ROWS, HGRN_ROWS)
                zero = jnp.zeros((HEAD, HEAD), F32)
                st_f, st_b = group(r0, r0, zero, zero)
                sfin_ref[s, 0] = st_f.T
                sfin_ref[s, 1] = st_b.T
                return carry

            lax.fori_loop(0, TILE // seg_len, request, 0)
        o = _rms(of_ref[...] + ob_ref[...], ng_ref[...]) * _silu(gb_ref[...].astype(F32))
        y_ref[...] = o.astype(BF16)

    @pl.when(i < n_pt)
    def _():
        body(seq, False)

    @pl.when(i >= n_pt)
    def _():
        body(TILE, True)


def _mixer_b(y_in, lb_logits, norm_g, state, mmats, layer, n_pt, seq):
    m = y_in.shape[0]
    depth = lb_logits.shape[0]
    nt = m // TILE
    spt = TILE // seq
    req = lambda i: jnp.maximum(i - n_pt, 0)
    slab = lambda off: pl.BlockSpec((TILE, HEAD), lambda i, h: (i, off // HEAD + h))
    return pl.pallas_call(
        functools.partial(_mixb_kernel, n_pt, seq, layer),
        out_shape=(jax.ShapeDtypeStruct((m, D_MIX), BF16),
                   jax.ShapeDtypeStruct((nt, spt, 2, N_HEADS, HEAD, HEAD), F32)),
        grid=(nt, N_HEADS),
        in_specs=[
            slab(OFF_QB), slab(OFF_FZF), slab(OFF_FZB), slab(OFF_IB), slab(OFF_GB),
            pl.BlockSpec((depth, 2, 1, HEAD), lambda i, h: (0, 0, 0, h)),
            pl.BlockSpec((None, None, 2, None, HEAD, HEAD), lambda i, h: (req(i), layer, 0, h, 0, 0)),
            pl.BlockSpec((None, 1, HEAD), lambda i, h: (layer, 0, 0)),
            pl.BlockSpec(mmats.shape, lambda i, h: (0, 0, 0)),
        ],
        out_specs=(pl.BlockSpec((TILE, HEAD), lambda i, h: (i, h)),
                   pl.BlockSpec((None, spt, 2, None, HEAD, HEAD), lambda i, h: (i, 0, 0, h, 0, 0))),
        scratch_shapes=[pltpu.VMEM((TILE, HEAD), F32)] * 2,
        compiler_params=_cparams(("parallel", "parallel")),
        name="mixer_b",
    )(y_in, y_in, y_in, y_in, y_in, lb_logits.reshape(depth, 2, 1, D_MIX), state,
      norm_g.reshape(depth, 1, HEAD), mmats)


def _rope(x, cos, sin_lo, sin_hi):
    return x * cos + pltpu.roll(x, HEAD - ROPE_FREQS, 1) * sin_lo + pltpu.roll(x, ROPE_FREQS, 1) * sin_hi


def _attn_kernel(n_pt, seq, q_ref, k_ref, v_ref, ck_ref, cv_ref, cos_ref, slo_ref, shi_ref, qg_ref, kg_ref,
                 y_ref, kn_ref, qs_ref, ks_ref, vs_ref):
    i = pl.program_id(0)
    scale = HEAD ** -0.5

    def body(seg_len, ctx):
        kn = _rms(k_ref[...], kg_ref[...])
        kn_ref[...] = kn
        if ctx:
            kn = _rope(kn, cos_ref[...], slo_ref[...], shi_ref[...])
        ks_ref[...] = kn.astype(BF16)
        vs_ref[...] = v_ref[...].astype(BF16)
        for g in range(GROUP):
            qn = _rms(q_ref[:, g * HEAD:(g + 1) * HEAD].astype(F32), qg_ref[...])
            if ctx:
                qn = _rope(qn, cos_ref[...], slo_ref[...], shi_ref[...])
            qs_ref[g] = (qn * scale).astype(BF16)
        if ctx:
            ckb = ck_ref[...].astype(BF16)
            cvb = cv_ref[...].astype(BF16)
        for s in range(TILE // seg_len):
            base = s * seg_len

            def qblocks(j, carry):
                keys = ks_ref[pl.ds(base, seg_len), :]
                vals = vs_ref[pl.ds(base, seg_len), :]
                rows = [pl.ds(pl.multiple_of(base + (j * Q_PAR + t) * Q_BLK, Q_BLK), Q_BLK) for t in range(Q_PAR)]
                qcat = [jnp.concatenate([qs_ref[g, r, :] for g in range(GROUP)], axis=0) for r in rows]
                sc = [_dot_nt(qc, keys) for qc in qcat]
                sc_c = [_dot_nt(qc, ckb) for qc in qcat] if ctx else None
                for t in range(Q_PAR):
                    mx = jnp.max(sc[t], axis=-1, keepdims=True)
                    if ctx:
                        mx = jnp.maximum(mx, jnp.max(sc_c[t], axis=-1, keepdims=True))
                    p = jnp.exp(sc[t] - mx)
                    den = jnp.sum(p, axis=-1, keepdims=True)
                    o = _dot(p.astype(BF16), vals)
                    if ctx:
                        pc = jnp.exp(sc_c[t] - mx)
                        den = den + jnp.sum(pc, axis=-1, keepdims=True)
                        o = o + _dot(pc.astype(BF16), cvb)
                    o = o / den
                    for g in range(GROUP):
                        y_ref[rows[t], g * HEAD:(g + 1) * HEAD] = o[g * Q_BLK:(g + 1) * Q_BLK].astype(BF16)
                return carry

            lax.fori_loop(0, seg_len // (Q_BLK * Q_PAR), qblocks, 0)

    @pl.when(i < n_pt)
    def _():
        body(seq, False)

    @pl.when(i >= n_pt)
    def _():
        body(TILE, True)


def _attention(y_in, kv, cache_k, cache_v, rope_tabs, q_norm_g, k_norm_g, layer, n_pt, seq):
    m = y_in.shape[0]
    depth = q_norm_g.shape[0]
    nt = m // TILE
    gw = GROUP * HEAD
    past = cache_k.shape[2]
    req = lambda i: jnp.maximum(i - n_pt, 0)
    ctx_spec = pl.BlockSpec((None, None, past, HEAD), lambda i, h: (req(i), layer, 0, h))
    tab = pl.BlockSpec((TILE, HEAD), lambda i, h: (0, 0))
    gspec = pl.BlockSpec((None, 1, HEAD), lambda i, h: (layer, 0, 0))
    return pl.pallas_call(
        functools.partial(_attn_kernel, n_pt, seq),
        out_shape=(jax.ShapeDtypeStruct((m, D_MIX), BF16),
                   jax.ShapeDtypeStruct((m, KV_DIM), F32)),
        grid=(nt, KV_HEADS),
        in_specs=[
            pl.BlockSpec((TILE, gw), lambda i, h: (i, OFF_QC // gw + h)),
            pl.BlockSpec((TILE, HEAD), lambda i, h: (i, h)),
            pl.BlockSpec((TILE, HEAD), lambda i, h: (i, KV_HEADS + h)),
            ctx_spec, ctx_spec, tab, tab, tab, gspec, gspec,
        ],
        out_specs=(pl.BlockSpec((TILE, gw), lambda i, h: (i, h)),
                   pl.BlockSpec((TILE, HEAD), lambda i, h: (i, h))),
        scratch_shapes=[pltpu.VMEM((GROUP, TILE, HEAD), BF16), pltpu.VMEM((TILE, HEAD), BF16),
                        pltpu.VMEM((TILE, HEAD), BF16)],
        compiler_params=_cparams(("parallel", "parallel")),
        name="attention",
    )(y_in, kv, kv, cache_k, cache_v, *rope_tabs, q_norm_g.reshape(depth, 1, HEAD),
      k_norm_g.reshape(depth, 1, HEAD))


def _rope_tables(n):
    rows = n // GRID_W
    r = jnp.repeat(jnp.arange(rows, dtype=F32), GRID_W)
    col = jnp.tile(jnp.arange(GRID_W, dtype=F32), rows)
    inv = ROPE_BASE ** (-jnp.arange(ROPE_FREQS, dtype=F32) / ROPE_FREQS)
    ar = r[:, None] * inv
    ac = col[:, None] * inv
    ang = jnp.concatenate([ar, ar, ac, ac], axis=-1)
    cos, sin = jnp.cos(ang), jnp.sin(ang)
    first_half = (jnp.arange(HEAD) // ROPE_FREQS) % 2 == 0
    return cos, jnp.where(first_half, -sin, 0.0), jnp.where(first_half, 0.0, sin)


def _merge_kernel(ya_ref, yb_ref, yc_ref, wa_ref, wb_ref, wc_ref, ga_ref, gb_ref, gc_ref, o_ref):
    acc = _sig(ga_ref[...].astype(F32)) * _dot(ya_ref[...], wa_ref[...].astype(BF16))
    acc = acc + _sig(gb_ref[...].astype(F32)) * _dot(yb_ref[...], wb_ref[...].astype(BF16))
    acc = acc + _sig(gc_ref[...].astype(F32)) * _dot(yc_ref[...], wc_ref[...].astype(BF16))
    o_ref[...] = acc.astype(BF16)


def _merge(y_a, y_b, y_c, y_in, w_a, w_b, w_c, layer):
    m = y_in.shape[0]
    tn = 512
    act = pl.BlockSpec((TILE, D_MIX), lambda i, j: (i, 0))
    wsp = pl.BlockSpec((None, D_MIX, tn), lambda i, j: (layer, 0, j))
    gate = lambda off: pl.BlockSpec((TILE, tn), lambda i, j: (i, off // tn + j))
    return pl.pallas_call(
        _merge_kernel,
        out_shape=jax.ShapeDtypeStruct((m, D_MODEL), BF16),
        grid=(m // TILE, D_MODEL // tn),
        in_specs=[act, act, act, wsp, wsp, wsp, gate(OFF_GA), gate(OFF_GB2), gate(OFF_GC)],
        out_specs=pl.BlockSpec((TILE, tn), lambda i, j: (i, j)),
        compiler_params=_cparams(("parallel", "parallel")),
        name="merge",
    )(y_a, y_b, y_c, w_a, w_b, w_c, y_in, y_in, y_in)


def _out_proj_kernel(mg_ref, w_ref, x_ref, g1_ref, xo_ref):
    xo_ref[...] = x_ref[...] + g1_ref[...] * _dot(mg_ref[...], w_ref[...].astype(BF16))


def _out_proj(merged, w_out, x, mod, layer, n_pt):
    m = x.shape[0]
    tn = 512
    row = _mod_row(TILE, n_pt)
    return pl.pallas_call(
        _out_proj_kernel,
        out_shape=jax.ShapeDtypeStruct((m, D_MODEL), F32),
        grid=(m // TILE, D_MODEL // tn),
        in_specs=[
            pl.BlockSpec((TILE, D_MODEL), lambda i, j: (i, 0)),
            pl.BlockSpec((None, D_MODEL, tn), lambda i, j: (layer, 0, j)),
            pl.BlockSpec((TILE, tn), lambda i, j: (i, j)),
            pl.BlockSpec((None, None, 1, tn), lambda i, j: (row(i), 2, 0, j)),
        ],
        out_specs=pl.BlockSpec((TILE, tn), lambda i, j: (i, j)),
        compiler_params=_cparams(("parallel", "parallel")),
        name="out_proj",
    )(merged, w_out, x, mod)


def _prefix_matrices(seq):
    t = np.arange(TILE)
    before = t[:, None] < t[None, :]
    same = (t[:, None] // seq) == (t[None, :] // seq)
    return np.stack([before & same, before]).astype(np.float32)


def _route_kernel(n_pt, seq, lg_ref, u_ref, tri_ref, xin_ref, pt_ref, gate_ref, p_ref):
    i = pl.program_id(0)

    def body(seg_len):
        n_seg = TILE // seg_len
        cap = EC_FACTOR * seg_len // N_EXPERTS
        lt = lg_ref[...].T[0:N_EXPERTS]
        ex = jnp.exp(lt - jnp.max(lt, axis=0, keepdims=True))
        aff = ex / jnp.sum(ex, axis=0, keepdims=True)
        bits = pltpu.bitcast(aff, jnp.int32)
        segs = [bits[:, s * seg_len:(s + 1) * seg_len] for s in range(n_seg)]

        def count_ge(seg, thr):
            return jnp.sum(jnp.where(seg >= thr, 1.0, 0.0), axis=1, keepdims=True)

        def bisect(_, carry):
            out = []
            for seg, (lo, hi) in zip(segs, carry):
                mid = lo + ((hi - lo + 1) >> 1)
                ok = count_ge(seg, mid) >= cap
                out.append((jnp.where(ok, mid, lo), jnp.where(ok, hi, mid - 1)))
            return tuple(out)

        init = tuple((jnp.zeros((N_EXPERTS, 1), jnp.int32),
                      jnp.full((N_EXPERTS, 1), F32_BITS_MAX_FINITE, jnp.int32)) for _ in segs)
        bounds = lax.fori_loop(0, 31, bisect, init)
        gt_l, eq_l, need_l = [], [], []
        for seg, (thr, _) in zip(segs, bounds):
            gt = seg > thr
            gt_l.append(jnp.where(gt, 1.0, 0.0))
            eq_l.append(jnp.where(seg == thr, 1.0, 0.0))
            n_gt = jnp.sum(gt_l[-1], axis=1, keepdims=True)
            need_l.append(jnp.broadcast_to(cap - n_gt, (N_EXPERTS, seg_len)))
        gt = jnp.concatenate(gt_l, axis=1)
        eq = jnp.concatenate(eq_l, axis=1)
        need = jnp.concatenate(need_l, axis=1)
        before = _dot(jnp.concatenate([gt, eq], axis=0).astype(BF16), tri_ref[...])
        gt_before, eq_before = before[0:N_EXPERTS], before[N_EXPERTS:]
        chosen = jnp.logical_or(gt > 0.0, jnp.logical_and(eq > 0.0, eq_before < need))
        tok = lax.broadcasted_iota(jnp.int32, (N_EXPERTS, TILE), 1)
        seg_off = ((tok // seg_len) * cap).astype(F32)
        slot = jnp.where(chosen, gt_before + jnp.minimum(eq_before, need) + seg_off, -1.0)
        slot_t = jnp.concatenate([slot, jnp.full((E_PAD - N_EXPERTS, TILE), -1.0, F32)], axis=0).T
        sub = lax.broadcasted_iota(jnp.int32, (SLOTS, TILE), 0).astype(F32)
        lane = lax.broadcasted_iota(jnp.int32, (TILE, SLOTS), 1).astype(F32)
        for e in range(N_EXPERTS):
            hit = slot[e:e + 1] == sub
            gate_ref[e] = jnp.sum(jnp.where(hit, aff[e:e + 1], 0.0), axis=1, keepdims=True)
            p_ref[e * SLOTS:(e + 1) * SLOTS, :] = jnp.where(hit, 1.0, 0.0).astype(BF16)
            hit_t = slot_t[:, e:e + 1] == lane
            pt_ref[:, e * SLOTS:(e + 1) * SLOTS] = jnp.where(hit_t, 1.0, 0.0).astype(BF16)
        tn = 512
        for n in range(D_MODEL // tn):
            rows = _dot(p_ref[...], u_ref[:, n * tn:(n + 1) * tn])
            xin_ref[:, :, n * tn:(n + 1) * tn] = rows.astype(BF16).reshape(N_EXPERTS, SLOTS, tn)

    @pl.when(i < n_pt)
    def _():
        body(seq)

    @pl.when(i >= n_pt)
    def _():
        body(TILE)


def _route(logits, u, tri, n_pt, seq):
    m = u.shape[0]
    nt = m // TILE
    return pl.pallas_call(
        functools.partial(_route_kernel, n_pt, seq),
        out_shape=(jax.ShapeDtypeStruct((N_EXPERTS, nt * SLOTS, D_MODEL), BF16),
                   jax.ShapeDtypeStruct((m, N_EXPERTS * SLOTS), BF16),
                   jax.ShapeDtypeStruct((N_EXPERTS, nt * SLOTS, 1), F32)),
        grid=(nt,),
        in_specs=[pl.BlockSpec((TILE, E_PAD), lambda i: (i, 0)),
                  pl.BlockSpec((TILE, D_MODEL), lambda i: (i, 0)),
                  pl.BlockSpec((None, TILE, TILE), lambda i: (jnp.where(i < n_pt, 0, 1), 0, 0))],
        out_specs=(pl.BlockSpec((N_EXPERTS, SLOTS, D_MODEL), lambda i: (0, i, 0)),
                   pl.BlockSpec((TILE, N_EXPERTS * SLOTS), lambda i: (i, 0)),
                   pl.BlockSpec((N_EXPERTS, SLOTS, 1), lambda i: (0, i, 0))),
        scratch_shapes=[pltpu.VMEM((N_EXPERTS * SLOTS, TILE), BF16)],
        compiler_params=_cparams(("parallel",)),
        name="route",
    )(logits, u, tri)


def _expert_kernel(n_f, x_ref, wg_ref, wu_ref, wd_ref, gate_ref, o_ref, h_ref):
    s = pl.program_id(1)
    tf = wg_ref.shape[-1]

    @pl.when(s < n_f)
    def _():
        x = x_ref[...]
        h = _silu(_dot(x, wg_ref[...].astype(BF16))) * _dot(x, wu_ref[...].astype(BF16))
        h_ref[s] = h.astype(BF16)

    @pl.when(s >= n_f)
    def _():
        acc = _dot(h_ref[0], wd_ref[0:tf, :].astype(BF16))
        for k in range(1, n_f):
            acc = acc + _dot(h_ref[k], wd_ref[k * tf:(k + 1) * tf, :].astype(BF16))
        o_ref[...] = (acc * gate_ref[...]).astype(BF16)


def _experts(xin, gates, w_gate, w_up, w_down, layer):
    _, rows, _ = xin.shape
    tf, tn = 256, 512
    n_f, n_o = D_EXPERT // tf, D_MODEL // tn
    f_idx = lambda s: jnp.minimum(s, n_f - 1)
    o_idx = lambda s: jnp.maximum(s - n_f, 0)
    return pl.pallas_call(
        functools.partial(_expert_kernel, n_f),
        out_shape=jax.ShapeDtypeStruct((N_EXPERTS, rows, D_MODEL), BF16),
        grid=(N_EXPERTS, n_f + n_o),
        in_specs=[
            pl.BlockSpec((None, rows, D_MODEL), lambda e, s: (e, 0, 0)),
            pl.BlockSpec((None, None, D_MODEL, tf), lambda e, s: (layer, e, 0, f_idx(s))),
            pl.BlockSpec((None, None, D_MODEL, tf), lambda e, s: (layer, e, 0, f_idx(s))),
            pl.BlockSpec((None, None, D_EXPERT, tn), lambda e, s: (layer, e, 0, o_idx(s))),
            pl.BlockSpec((None, rows, 1), lambda e, s: (e, 0, 0)),
        ],
        out_specs=pl.BlockSpec((None, rows, tn), lambda e, s: (e, 0, o_idx(s))),
        scratch_shapes=[pltpu.VMEM((n_f, rows, tf), BF16)],
        compiler_params=_cparams(("parallel", "arbitrary")),
        name="experts",
    )(xin, w_gate, w_up, w_down, gates)


def _scatter_kernel(pt_ref, y_ref, x_ref, g2_ref, o_ref):
    tn = y_ref.shape[-1]
    y = y_ref[...].reshape(pt_ref.shape[1], tn)
    o_ref[...] = x_ref[...] + g2_ref[...] * _dot(pt_ref[...], y)


def _scatter(pt, y, x, mod, n_pt):
    m = x.shape[0]
    tn = 512
    row = _mod_row(TILE, n_pt)
    return pl.pallas_call(
        _scatter_kernel,
        out_shape=jax.ShapeDtypeStruct((m, D_MODEL), F32),
        grid=(m // TILE, D_MODEL // tn),
        in_specs=[
            pl.BlockSpec((TILE, N_EXPERTS * SLOTS), lambda i, j: (i, 0)),
            pl.BlockSpec((N_EXPERTS, SLOTS, tn), lambda i, j: (0, i, j)),
            pl.BlockSpec((TILE, tn), lambda i, j: (i, j)),
            pl.BlockSpec((None, None, 1, tn), lambda i, j: (row(i), 5, 0, j)),
        ],
        out_specs=pl.BlockSpec((TILE, tn), lambda i, j: (i, j)),
        compiler_params=_cparams(("parallel", "parallel")),
        name="scatter",
    )(pt, y, x, mod)


def _final_norm_kernel(x_ref, g_ref, o_ref):
    o_ref[...] = _rms(x_ref[...], g_ref[...])


def _final_norm(x, g, first_tile, n_tiles):
    tm = 512
    per = TILE // tm
    return pl.pallas_call(
        _final_norm_kernel,
        out_shape=jax.ShapeDtypeStruct((n_tiles * TILE, D_MODEL), F32),
        grid=(n_tiles * per,),
        in_specs=[pl.BlockSpec((tm, D_MODEL), lambda i: (i + first_tile * per, 0)),
                  pl.BlockSpec((1, D_MODEL), lambda i: (0, 0))],
        out_specs=pl.BlockSpec((tm, D_MODEL), lambda i: (i, 0)),
        compiler_params=_cparams(("parallel",)),
        name="final_norm",
    )(x, g.reshape(1, D_MODEL))


def kernel(x_prompt, x_sample, cache_k, cache_v, state_rglru, state_hgrn, c, c_ctx, w_ada, b_ada, norm1_g, norm2_g, w_in, conv_w, conv_b, rg_wr, rg_br, rg_wi, rg_bi, rg_lambda, hgrn_lb_logits, hgrn_norm_g, q_norm_g, k_norm_g, w_branch_a, w_branch_b, w_branch_c, w_out, w_router, w_exp_gate, w_exp_up, w_exp_down, final_norm_g):
    bp, seq, _ = x_prompt.shape
    bs, dec_seq, _ = x_sample.shape
    depth = w_in.shape[0]
    past = cache_k.shape[2]
    m = bp * seq + bs * dec_seq
    assert dec_seq == TILE and TILE % seq == 0 and (bp * seq) % TILE == 0 and seq & (seq - 1) == 0
    assert seq % (Q_BLK * Q_PAR) == 0 and past == seq and m % BIG_TILE == 0 and seq == HGRN_ROWS
    n_pt = bp * seq // TILE

    x = jnp.concatenate([x_prompt.reshape(bp * seq, D_MODEL), x_sample.reshape(bs * dec_seq, D_MODEL)], axis=0)
    n_cond = -(-(1 + bs) // 8) * 8
    cond = jnp.concatenate([c_ctx[None], c, jnp.zeros((n_cond - 1 - bs, D_MODEL), F32)], axis=0)
    mod_all = _ada_mod(cond, w_ada, b_ada).reshape(depth, n_cond, N_SUB, 1, D_MODEL)
    rope_tabs = _rope_tables(dec_seq)
    mmats = jnp.asarray(_hgrn_matrices(), BF16)
    tri = jnp.asarray(_prefix_matrices(seq), BF16)
    w_router_pad = jnp.pad(w_router, ((0, 0), (0, 0), (0, E_PAD - N_EXPERTS)))
    ck = cache_k.reshape(bs, depth, past, KV_DIM)
    cv = cache_v.reshape(bs, depth, past, KV_DIM)
    n1 = norm1_g.reshape(depth, 1, D_MODEL)
    n2 = norm2_g.reshape(depth, 1, D_MODEL)

    k_list, v_list, ra_list, hb_list = [], [], [], []
    for l in range(depth):
        mod = mod_all[l]
        u1 = _norm_mod(x, n1, mod, l, n_pt, 0)
        y_in, kv = _in_proj(u1, w_in, l)
        y_a, fin_a = _mixer_a(y_in, conv_w, conv_b, rg_wr, rg_br, rg_wi, rg_bi, rg_lambda, state_rglru, l, n_pt, seq)
        y_b, fin_b = _mixer_b(y_in, hgrn_lb_logits, hgrn_norm_g, state_hgrn, mmats, l, n_pt, seq)
        y_c, k_n = _attention(y_in, kv, ck, cv, rope_tabs, q_norm_g, k_norm_g, l, n_pt, seq)
        merged = _merge(y_a, y_b, y_c, y_in, w_branch_a, w_branch_b, w_branch_c, l)
        x = _out_proj(merged, w_out, x, mod, l, n_pt)
        u2, logits = _norm_mod(x, n2, mod, l, n_pt, 1, w_router_pad)
        xin, pt, gates = _route(logits, u2, tri, n_pt, seq)
        y_e = _experts(xin, gates, w_exp_gate, w_exp_up, w_exp_down, l)
        x = _scatter(pt, y_e, x, mod, n_pt)
        k_list.append(k_n[:bp * seq].reshape(bp, seq, KV_HEADS, HEAD))
        v_list.append(kv[:bp * seq, KV_DIM:].reshape(bp, seq, KV_HEADS, HEAD))
        ra_list.append(fin_a[:n_pt].reshape(bp, 2, D_MIX))
        hb_list.append(fin_b[:n_pt].reshape(bp, 2, N_HEADS, HEAD, HEAD))

    y_prompt = _final_norm(x, final_norm_g, 0, n_pt).reshape(bp, seq, D_MODEL)
    y_sample = _final_norm(x, final_norm_g, n_pt, bs).reshape(bs, dec_seq, D_MODEL)
    return (y_prompt, y_sample, jnp.stack(k_list, axis=1), jnp.stack(v_list, axis=1),
            jnp.stack(ra_list, axis=1), jnp.stack(hb_list, axis=1))
```

```python
import functools

import numpy as np
import jax
import jax.numpy as jnp
from jax import lax
from jax.experimental import pallas as pl
from jax.experimental.pallas import tpu as pltpu

F32 = jnp.float32
BF16 = jnp.bfloat16

D_MODEL = 2048
D_MIX = D_MODEL // 2
HEAD = 128
N_HEADS = D_MIX // HEAD
CONV_W = 4
RG_C = 8.0
CHUNK = 64
N_LEVELS = 6
KV_HEADS = 2
KV_DIM = KV_HEADS * HEAD
GROUP = N_HEADS // KV_HEADS
ROPE_FREQS = HEAD // 4
ROPE_BASE = 10000.0
GRID_W = 64
N_EXPERTS = 16
EC_FACTOR = 2
D_EXPERT = D_MODEL // 2
N_SUB = 6
EPS = 1e-6
TILE = 1024
BIG_TILE = 2 * TILE
SLOTS = TILE * EC_FACTOR // N_EXPERTS
E_PAD = 128
Q_BLK = 128
Q_PAR = 2
HGRN_GROUP = 4
HGRN_ROWS = HGRN_GROUP * CHUNK
F32_BITS_MAX_FINITE = 0x7F7FFFFF
ROW_TILE = 256
VMEM_LIMIT = 48 * 1024 * 1024
EXPERT_VMEM_LIMIT = 56 * 1024 * 1024

OFF_XA, OFF_YA, OFF_QB, OFF_FZF, OFF_FZB, OFF_IB, OFF_GB, OFF_QC = [i * D_MIX for i in range(8)]
OFF_KC = 8 * D_MIX
OFF_VC = OFF_KC + KV_DIM
OFF_GA = OFF_VC + KV_DIM
OFF_GB2 = OFF_GA + D_MODEL
OFF_GC = OFF_GB2 + D_MODEL
N_IN = OFF_GC + D_MODEL
IN_TN = 2 * KV_DIM


def _cparams(sem):
    return pltpu.CompilerParams(dimension_semantics=sem, vmem_limit_bytes=VMEM_LIMIT)


def _sig(x):
    return 0.5 * jnp.tanh(0.5 * x) + 0.5


def _silu(x):
    return x * _sig(x)


def _dot(a, b):
    return jnp.dot(a, b, preferred_element_type=F32)


def _dot_nt(a, b):
    return lax.dot_general(a, b, (((1,), (1,)), ((), ())), preferred_element_type=F32)


def _rms(x, g):
    ms = jnp.mean(x * x, axis=-1, keepdims=True)
    return x * lax.rsqrt(ms + EPS) * g


def _mod_row(tile_rows, n_pt):
    per = TILE // tile_rows if tile_rows <= TILE else None
    if per is None:
        mult = tile_rows // TILE
        return lambda i: jnp.maximum(i * mult - n_pt + 1, 0)
    return lambda i: jnp.maximum(i // per - n_pt + 1, 0)


def _ada_kernel(c_ref, w_ref, b_ref, o_ref):
    s = _silu(c_ref[...]).astype(BF16)
    o_ref[...] = _dot(s, w_ref[...].astype(BF16)) + b_ref[...]


def _ada_mod(cond, w_ada, b_ada):
    depth, _, n = w_ada.shape
    rows = cond.shape[0]
    tn = 1024
    return pl.pallas_call(
        _ada_kernel,
        out_shape=jax.ShapeDtypeStruct((depth, rows, n), F32),
        grid=(depth, n // tn),
        in_specs=[
            pl.BlockSpec((rows, D_MODEL), lambda l, j: (0, 0)),
            pl.BlockSpec((None, D_MODEL, tn), lambda l, j: (l, 0, j)),
            pl.BlockSpec((None, 1, tn), lambda l, j: (l, 0, j)),
        ],
        out_specs=pl.BlockSpec((None, rows, tn), lambda l, j: (l, 0, j)),
        compiler_params=_cparams(("parallel", "parallel")),
        name="ada_mod",
    )(cond, w_ada, b_ada.reshape(depth, 1, n))


def _norm_mod_kernel(x_ref, g_ref, sc_ref, sh_ref, u_ref):
    u_ref[...] = (_rms(x_ref[...], g_ref[...]) * (1.0 + sc_ref[...]) + sh_ref[...]).astype(BF16)


def _mod_specs(mod_row, sub):
    mspec = lambda c: pl.BlockSpec((None, None, 1, D_MODEL), lambda i, *_: (mod_row(i), c, 0, 0))
    return mspec(3 * sub + 1), mspec(3 * sub)


def _norm_mod(x, norm_g, mod, layer, n_pt):
    m = x.shape[0]
    tm = 512
    return pl.pallas_call(
        _norm_mod_kernel,
        out_shape=jax.ShapeDtypeStruct((m, D_MODEL), BF16),
        grid=(m // tm,),
        in_specs=[pl.BlockSpec((tm, D_MODEL), lambda i: (i, 0)),
                  pl.BlockSpec((None, 1, D_MODEL), lambda i: (layer, 0, 0)),
                  *_mod_specs(_mod_row(tm, n_pt), 0)],
        out_specs=pl.BlockSpec((tm, D_MODEL), lambda i: (i, 0)),
        compiler_params=_cparams(("parallel",)), name="norm_mod",
    )(x, norm_g, mod, mod)


def _in_proj_kernel(kv_tile, u_ref, w_ref, o_ref, kv_ref):
    acc = _dot(u_ref[...], w_ref[...].astype(BF16))
    o_ref[...] = acc.astype(BF16)

    @pl.when(pl.program_id(1) == kv_tile)
    def _():
        kv_ref[...] = acc


def _in_proj(u, w_in, layer):
    m = u.shape[0]
    assert OFF_KC % IN_TN == 0 and N_IN % IN_TN == 0
    return pl.pallas_call(
        functools.partial(_in_proj_kernel, OFF_KC // IN_TN),
        out_shape=(jax.ShapeDtypeStruct((m, N_IN), BF16), jax.ShapeDtypeStruct((m, IN_TN), F32)),
        grid=(m // BIG_TILE, N_IN // IN_TN),
        in_specs=[
            pl.BlockSpec((BIG_TILE, D_MODEL), lambda i, j: (i, 0)),
            pl.BlockSpec((None, D_MODEL, IN_TN), lambda i, j: (layer, 0, j)),
        ],
        out_specs=(pl.BlockSpec((BIG_TILE, IN_TN), lambda i, j: (i, j)),
                   pl.BlockSpec((BIG_TILE, IN_TN), lambda i, j: (i, 0))),
        compiler_params=_cparams(("parallel", "arbitrary")),
        name="in_proj",
    )(u, w_in)


def _softplus(z):
    return jnp.maximum(z, 0.0) + jnp.log1p(jnp.exp(-jnp.abs(z)))


def _gelu_tanh(x):
    return 0.5 * x * (1.0 + jnp.tanh(np.sqrt(2.0 / np.pi).astype(np.float32) * (x + 0.044715 * (x * x * x))))


def _mixa_kernel(n_pt, seq, xa_ref, ya_ref, cw_ref, cb_ref, wr_ref, br_ref, wi_ref, bi_ref, lam_ref, h0_ref,
                 y_ref, fin_ref, af_ref, bf_ref, ab_ref, bb_ref, hf_ref, hb_ref):
    i = pl.program_id(0)

    def body(seg_len, use_state):
        x = xa_ref[...].astype(F32)
        pos = lax.broadcasted_iota(jnp.int32, (TILE, HEAD), 0) & (seg_len - 1)
        cw = cw_ref[...]
        xc = (jnp.where(pos >= 1, pltpu.roll(x, 1, 0), 0.0) * cw[0:1]
              + x * cw[1:2]
              + jnp.where(pos < seg_len - 1, pltpu.roll(x, TILE - 1, 0), 0.0) * cw[2:3]
              + jnp.where(pos < seg_len - 2, pltpu.roll(x, TILE - 2, 0), 0.0) * cw[3:4]
              + cb_ref[...])
        xcb = xc.astype(BF16)
        for d, (a_ref, b_ref) in enumerate(((af_ref, bf_ref), (ab_ref, bb_ref))):
            r = _sig(_dot(xcb, wr_ref[d].astype(BF16)) + br_ref[d])
            ig = _sig(_dot(xcb, wi_ref[d].astype(BF16)) + bi_ref[d])
            log_a = (-RG_C) * r * _softplus(-lam_ref[d])
            a = jnp.exp(log_a)
            a_ref[...] = a
            b_ref[...] = jnp.sqrt(-jnp.tanh(log_a) * (a * a + 1.0)) * (ig * xc)

        nb = seg_len // 8
        rowi = lax.broadcasted_iota(jnp.int32, (8, HEAD), 0)
        n_seg = TILE // seg_len
        for s in range(n_seg):
            base = s * seg_len
            if use_state:
                c0 = (h0_ref[0:1, :], h0_ref[1:2, :])
            else:
                c0 = (jnp.zeros((1, HEAD), F32), jnp.zeros((1, HEAD), F32))

            def step(j, carry):
                cf, cb = carry
                of = pl.multiple_of(base + j * 8, 8)
                ob = pl.multiple_of(base + (nb - 1 - j) * 8, 8)
                a = af_ref[pl.ds(of, 8), :]
                b = bf_ref[pl.ds(of, 8), :]
                for d in (1, 2, 4):
                    m = rowi >= d
                    b = jnp.where(m, a * pltpu.roll(b, d, 0) + b, b)
                    a = jnp.where(m, a * pltpu.roll(a, d, 0), a)
                h = b + a * cf
                hf_ref[pl.ds(of, 8), :] = h
                cf = h[7:8, :]
                a = ab_ref[pl.ds(ob, 8), :]
                b = bb_ref[pl.ds(ob, 8), :]
                for d in (1, 2, 4):
                    m = rowi < 8 - d
                    b = jnp.where(m, a * pltpu.roll(b, 8 - d, 0) + b, b)
                    a = jnp.where(m, a * pltpu.roll(a, 8 - d, 0), a)
                h = b + a * cb
                hb_ref[pl.ds(ob, 8), :] = h
                cb = h[0:1, :]
                return cf, cb

            cf, cb = lax.fori_loop(0, nb, step, c0, unroll=4)
            fin_ref[s, 0:1, :] = cf
            fin_ref[s, 1:2, :] = cb
        for s in range(n_seg, TILE // seq):
            fin_ref[s] = jnp.zeros((2, HEAD), F32)
        y_ref[...] = ((hf_ref[...] + hb_ref[...]) * _gelu_tanh(ya_ref[...].astype(F32))).astype(BF16)

    @pl.when(i < n_pt)
    def _():
        body(seq, False)

    @pl.when(i >= n_pt)
    def _():
        body(TILE, True)


def _mixer_a(y_in, conv_w, conv_b, rg_wr, rg_br, rg_wi, rg_bi, rg_lambda, state, layer, n_pt, seq):
    m = y_in.shape[0]
    depth = conv_w.shape[0]
    nt = m // TILE
    spt = TILE // seq
    req = lambda i: jnp.maximum(i - n_pt, 0)
    vec = lambda a: a.reshape(depth, 2, 1, D_MIX)
    gate_w = pl.BlockSpec((None, 2, None, HEAD, HEAD), lambda i, h: (layer, 0, h, 0, 0))
    gate_b = pl.BlockSpec((None, 2, 1, HEAD), lambda i, h: (layer, 0, 0, h))
    slab = lambda off: pl.BlockSpec((TILE, HEAD), lambda i, h: (i, off // HEAD + h))
    return pl.pallas_call(
        functools.partial(_mixa_kernel, n_pt, seq),
        out_shape=(jax.ShapeDtypeStruct((m, D_MIX), BF16),
                   jax.ShapeDtypeStruct((nt, spt, 2, D_MIX), F32)),
        grid=(nt, N_HEADS),
        in_specs=[
            slab(OFF_XA), slab(OFF_YA),
            pl.BlockSpec((None, CONV_W, HEAD), lambda i, h: (layer, 0, h)),
            pl.BlockSpec((None, 1, HEAD), lambda i, h: (layer, 0, h)),
            gate_w, gate_b, gate_w, gate_b, gate_b,
            pl.BlockSpec((None, None, 2, HEAD), lambda i, h: (req(i), layer, 0, h)),
        ],
        out_specs=(pl.BlockSpec((TILE, HEAD), lambda i, h: (i, h)),
                   pl.BlockSpec((None, spt, 2, HEAD), lambda i, h: (i, 0, 0, h))),
        scratch_shapes=[pltpu.VMEM((TILE, HEAD), F32)] * 6,
        compiler_params=_cparams(("parallel", "parallel")),
        name="mixer_a",
    )(y_in, y_in, conv_w, conv_b.reshape(depth, 1, D_MIX), rg_wr, vec(rg_br), rg_wi, vec(rg_bi), vec(rg_lambda),
      state)


def _hgrn_matrices():
    c = CHUNK
    mats = []
    for reverse in (False, True):
        mm = np.zeros((2 + N_LEVELS, c, c), np.float32)
        for t in range(c):
            if not reverse:
                mm[0, t, :t + 1] = 1.0
                mm[1, t, t + 1:] = 1.0
            else:
                mm[0, t, t:] = 1.0
                mm[1, t, :t] = 1.0
            for lvl in range(N_LEVELS):
                m = c >> (lvl + 1)
                blk, p = divmod(t, 2 * m)
                if not reverse:
                    ref = blk * 2 * m + m - 1
                    if p >= m:
                        mm[2 + lvl, t, ref + 1:t + 1] = 1.0
                    else:
                        mm[2 + lvl, t, t + 1:ref + 1] = 1.0
                else:
                    ref = blk * 2 * m + m
                    if p < m:
                        mm[2 + lvl, t, t:ref] = 1.0
                    else:
                        mm[2 + lvl, t, ref:t] = 1.0
        mats.append(mm.reshape((2 + N_LEVELS) * c, c))
    return np.stack(mats)


def _hgrn_decays(g, mmat):
    g_cat = jnp.concatenate([g[n * CHUNK:(n + 1) * CHUNK] for n in range(HGRN_GROUP)], axis=1)
    gh = g_cat.astype(BF16)
    gl = (g_cat - gh.astype(F32)).astype(BF16)
    return _dot(mmat, gh) + _dot(mmat, gl)


def _hgrn_chunk_local(q, k, v, e_all, reverse):
    c = CHUNK
    e_in = jnp.exp(e_all[0:c])
    e_out = jnp.exp(e_all[c:2 * c])
    vb = v.astype(BF16)
    rowi = lax.broadcasted_iota(jnp.int32, (c, HEAD), 0)
    ti = lax.broadcasted_iota(jnp.int32, (c, c), 0)
    si = lax.broadcasted_iota(jnp.int32, (c, c), 1)
    scores = jnp.where(ti == si, _dot_nt(q.astype(BF16), k.astype(BF16)), 0.0)
    for lvl in range(N_LEVELS):
        m = c >> (lvl + 1)
        e = jnp.exp(e_all[(2 + lvl) * c:(3 + lvl) * c])
        late = (rowi & m) != 0
        q_side = late if not reverse else jnp.logical_not(late)
        qd = jnp.where(q_side, q * e, 0.0).astype(BF16)
        kd = jnp.where(q_side, 0.0, k * e).astype(BF16)
        scores = scores + jnp.where((ti ^ si) < 2 * m, _dot_nt(qd, kd), 0.0)
    ke = (k * e_out).astype(BF16)
    own = lax.dot_general(vb, ke, (((0,), (0,)), ((), ())), preferred_element_type=F32)
    d_end = e_in[c - 1:c] if not reverse else e_in[0:1]
    return (q * e_in).astype(BF16), scores.astype(BF16), vb, own, d_end


def _mixb_kernel(n_pt, seq, layer, qb_ref, fzf_ref, fzb_ref, ib_ref, gb_ref, lbl_ref, s0_ref, ng_ref, mm_ref,
                 y_ref, sfin_ref, of_ref, ob_ref):
    i = pl.program_id(0)
    depth = lbl_ref.shape[0]

    def lower_bound(d):
        lg = [lbl_ref[j, d] for j in range(depth)]
        mx = functools.reduce(jnp.maximum, lg)
        ex = [jnp.exp(v - mx) for v in lg]
        tot = functools.reduce(lambda a, b: a + b, ex)
        acc = jnp.zeros_like(tot)
        for j in range(1, layer + 1):
            acc = acc + ex[j] / tot
        return acc

    def body(seg_len, use_state):
        lbs = (lower_bound(0), lower_bound(1))
        n_grp = seg_len // HGRN_ROWS
        chunk_rows = lambda a, n: a[n * CHUNK:(n + 1) * CHUNK]

        def load(r0, d):
            rows = pl.ds(r0, HGRN_ROWS)
            q = _silu(qb_ref[rows, :].astype(F32))
            v = ib_ref[rows, :].astype(F32)
            fz = (fzf_ref if d == 0 else fzb_ref)[rows, :].astype(F32)
            f = lbs[d] + (1.0 - lbs[d]) / (1.0 + jnp.exp(-fz))
            return q, 1.0 - f, v, jnp.log(f)

        def group(r_f, r_b, st_f, st_b):
            qf, kf, vf, gf = load(r_f, 0)
            qb, kb, vb, gb = load(r_b, 1)
            ef = _hgrn_decays(gf, mm_ref[0])
            eb = _hgrn_decays(gb, mm_ref[1])
            loc_f = [_hgrn_chunk_local(chunk_rows(qf, n), chunk_rows(kf, n), chunk_rows(vf, n),
                                       ef[:, n * HEAD:(n + 1) * HEAD], False) for n in range(HGRN_GROUP)]
            loc_b = [_hgrn_chunk_local(chunk_rows(qb, n), chunk_rows(kb, n), chunk_rows(vb, n),
                                       eb[:, n * HEAD:(n + 1) * HEAD], True) for n in range(HGRN_GROUP)]
            for n in range(HGRN_GROUP):
                qe, sc, vals, own, d_end = loc_f[n]
                of_ref[pl.ds(r_f + n * CHUNK, CHUNK), :] = _dot_nt(qe, st_f.astype(BF16)) + _dot(sc, vals)
                st_f = st_f * d_end + own
            for n in reversed(range(HGRN_GROUP)):
                qe, sc, vals, own, d_end = loc_b[n]
                ob_ref[pl.ds(r_b + n * CHUNK, CHUNK), :] = _dot_nt(qe, st_b.astype(BF16)) + _dot(sc, vals)
                st_b = st_b * d_end + own
            return st_f, st_b

        if use_state:
            def step(j, carry):
                r_f = pl.multiple_of(j * HGRN_ROWS, HGRN_ROWS)
                r_b = pl.multiple_of((n_grp - 1 - j) * HGRN_ROWS, HGRN_ROWS)
                return group(r_f, r_b, *carry)

            st_f, st_b = lax.fori_loop(0, n_grp, step, (s0_ref[0].T, s0_ref[1].T))
            sfin_ref[0, 0] = st_f.T
            sfin_ref[0, 1] = st_b.T
            for s in range(1, TILE // seq):
                sfin_ref[s] = jnp.zeros((2, HEAD, HEAD), F32)
        else:
            def request(s, carry):
                r0 = pl.multiple_of(s * HGRN_ROWS, HGRN_ROWS)
                zero = jnp.zeros((HEAD, HEAD), F32)
                st_f, st_b = group(r0, r0, zero, zero)
                sfin_ref[s, 0] = st_f.T
                sfin_ref[s, 1] = st_b.T
                return carry

            lax.fori_loop(0, TILE // seg_len, request, 0)
        o = _rms(of_ref[...] + ob_ref[...], ng_ref[...]) * _silu(gb_ref[...].astype(F32))
        y_ref[...] = o.astype(BF16)

    @pl.when(i < n_pt)
    def _():
        body(seq, False)

    @pl.when(i >= n_pt)
    def _():
        body(TILE, True)


def _mixer_b(y_in, lb_logits, norm_g, state, mmats, layer, n_pt, seq):
    m = y_in.shape[0]
    depth = lb_logits.shape[0]
    nt = m // TILE
    spt = TILE // seq
    req = lambda i: jnp.maximum(i - n_pt, 0)
    slab = lambda off: pl.BlockSpec((TILE, HEAD), lambda i, h: (i, off // HEAD + h))
    return pl.pallas_call(
        functools.partial(_mixb_kernel, n_pt, seq, layer),
        out_shape=(jax.ShapeDtypeStruct((m, D_MIX), BF16),
                   jax.ShapeDtypeStruct((nt, spt, 2, N_HEADS, HEAD, HEAD), F32)),
        grid=(nt, N_HEADS),
        in_specs=[
            slab(OFF_QB), slab(OFF_FZF), slab(OFF_FZB), slab(OFF_IB), slab(OFF_GB),
            pl.BlockSpec((depth, 2, 1, HEAD), lambda i, h: (0, 0, 0, h)),
            pl.BlockSpec((None, None, 2, None, HEAD, HEAD), lambda i, h: (req(i), layer, 0, h, 0, 0)),
            pl.BlockSpec((None, 1, HEAD), lambda i, h: (layer, 0, 0)),
            pl.BlockSpec(mmats.shape, lambda i, h: (0, 0, 0)),
        ],
        out_specs=(pl.BlockSpec((TILE, HEAD), lambda i, h: (i, h)),
                   pl.BlockSpec((None, spt, 2, None, HEAD, HEAD), lambda i, h: (i, 0, 0, h, 0, 0))),
        scratch_shapes=[pltpu.VMEM((TILE, HEAD), F32)] * 2,
        compiler_params=_cparams(("parallel", "parallel")),
        name="mixer_b",
    )(y_in, y_in, y_in, y_in, y_in, lb_logits.reshape(depth, 2, 1, D_MIX), state,
      norm_g.reshape(depth, 1, HEAD), mmats)


def _rope(x, cos, sin_lo, sin_hi):
    return x * cos + pltpu.roll(x, HEAD - ROPE_FREQS, 1) * sin_lo + pltpu.roll(x, ROPE_FREQS, 1) * sin_hi


def _attn_kernel(n_pt, seq, q_ref, k_ref, v_ref, ck_ref, cv_ref, cos_ref, slo_ref, shi_ref, qg_ref, kg_ref,
                 y_ref, kn_ref, qs_ref, ks_ref, vs_ref):
    i = pl.program_id(0)
    scale = HEAD ** -0.5

    def body(seg_len, ctx):
        kn = _rms(k_ref[...], kg_ref[...])
        kn_ref[...] = kn
        if ctx:
            kn = _rope(kn, cos_ref[...], slo_ref[...], shi_ref[...])
        ks_ref[...] = kn.astype(BF16)
        vs_ref[...] = v_ref[...].astype(BF16)
        for g in range(GROUP):
            qn = _rms(q_ref[:, g * HEAD:(g + 1) * HEAD].astype(F32), qg_ref[...])
            if ctx:
                qn = _rope(qn, cos_ref[...], slo_ref[...], shi_ref[...])
            qs_ref[g] = (qn * scale).astype(BF16)
        if ctx:
            ckb = ck_ref[...].astype(BF16)
            cvb = cv_ref[...].astype(BF16)
        for s in range(TILE // seg_len):
            base = s * seg_len

            def qblocks(j, carry):
                keys = ks_ref[pl.ds(base, seg_len), :]
                vals = vs_ref[pl.ds(base, seg_len), :]
                rows = [pl.ds(pl.multiple_of(base + (j * Q_PAR + t) * Q_BLK, Q_BLK), Q_BLK) for t in range(Q_PAR)]
                qcat = [jnp.concatenate([qs_ref[g, r, :] for g in range(GROUP)], axis=0) for r in rows]
                sc = [_dot_nt(qc, keys) for qc in qcat]
                sc_c = [_dot_nt(qc, ckb) for qc in qcat] if ctx else None
                for t in range(Q_PAR):
                    mx = jnp.max(sc[t], axis=-1, keepdims=True)
                    if ctx:
                        mx = jnp.maximum(mx, jnp.max(sc_c[t], axis=-1, keepdims=True))
                    p = jnp.exp(sc[t] - mx)
                    den = jnp.sum(p, axis=-1, keepdims=True)
                    o = _dot(p.astype(BF16), vals)
                    if ctx:
                        pc = jnp.exp(sc_c[t] - mx)
                        den = den + jnp.sum(pc, axis=-1, keepdims=True)
                        o = o + _dot(pc.astype(BF16), cvb)
                    o = o / den
                    for g in range(GROUP):
                        y_ref[rows[t], g * HEAD:(g + 1) * HEAD] = o[g * Q_BLK:(g + 1) * Q_BLK].astype(BF16)
                return carry

            lax.fori_loop(0, seg_len // (Q_BLK * Q_PAR), qblocks, 0)

    @pl.when(i < n_pt)
    def _():
        body(seq, False)

    @pl.when(i >= n_pt)
    def _():
        body(TILE, True)


def _attention(y_in, kv, cache_k, cache_v, rope_tabs, q_norm_g, k_norm_g, layer, n_pt, seq):
    m = y_in.shape[0]
    depth = q_norm_g.shape[0]
    nt = m // TILE
    gw = GROUP * HEAD
    past = cache_k.shape[2]
    req = lambda i: jnp.maximum(i - n_pt, 0)
    ctx_spec = pl.BlockSpec((None, None, past, HEAD), lambda i, h: (req(i), layer, 0, h))
    tab = pl.BlockSpec((TILE, HEAD), lambda i, h: (0, 0))
    gspec = pl.BlockSpec((None, 1, HEAD), lambda i, h: (layer, 0, 0))
    return pl.pallas_call(
        functools.partial(_attn_kernel, n_pt, seq),
        out_shape=(jax.ShapeDtypeStruct((m, D_MIX), BF16),
                   jax.ShapeDtypeStruct((m, KV_DIM), F32)),
        grid=(nt, KV_HEADS),
        in_specs=[
            pl.BlockSpec((TILE, gw), lambda i, h: (i, OFF_QC // gw + h)),
            pl.BlockSpec((TILE, HEAD), lambda i, h: (i, h)),
            pl.BlockSpec((TILE, HEAD), lambda i, h: (i, KV_HEADS + h)),
            ctx_spec, ctx_spec, tab, tab, tab, gspec, gspec,
        ],
        out_specs=(pl.BlockSpec((TILE, gw), lambda i, h: (i, h)),
                   pl.BlockSpec((TILE, HEAD), lambda i, h: (i, h))),
        scratch_shapes=[pltpu.VMEM((GROUP, TILE, HEAD), BF16), pltpu.VMEM((TILE, HEAD), BF16),
                        pltpu.VMEM((TILE, HEAD), BF16)],
        compiler_params=_cparams(("parallel", "parallel")),
        name="attention",
    )(y_in, kv, kv, cache_k, cache_v, *rope_tabs, q_norm_g.reshape(depth, 1, HEAD),
      k_norm_g.reshape(depth, 1, HEAD))


def _rope_tables(n):
    rows = n // GRID_W
    r = jnp.repeat(jnp.arange(rows, dtype=F32), GRID_W)
    col = jnp.tile(jnp.arange(GRID_W, dtype=F32), rows)
    inv = ROPE_BASE ** (-jnp.arange(ROPE_FREQS, dtype=F32) / ROPE_FREQS)
    ar = r[:, None] * inv
    ac = col[:, None] * inv
    ang = jnp.concatenate([ar, ar, ac, ac], axis=-1)
    cos, sin = jnp.cos(ang), jnp.sin(ang)
    first_half = (jnp.arange(HEAD) // ROPE_FREQS) % 2 == 0
    return cos, jnp.where(first_half, -sin, 0.0), jnp.where(first_half, 0.0, sin)


def _merge_kernel(ya_ref, yb_ref, yc_ref, wa_ref, wb_ref, wc_ref, ga_ref, gb_ref, gc_ref, o_ref):
    acc = _sig(ga_ref[...].astype(F32)) * _dot(ya_ref[...], wa_ref[...].astype(BF16))
    acc = acc + _sig(gb_ref[...].astype(F32)) * _dot(yb_ref[...], wb_ref[...].astype(BF16))
    acc = acc + _sig(gc_ref[...].astype(F32)) * _dot(yc_ref[...], wc_ref[...].astype(BF16))
    o_ref[...] = acc.astype(BF16)


def _merge(y_a, y_b, y_c, y_in, w_a, w_b, w_c, layer):
    m = y_in.shape[0]
    tn = 256
    act = pl.BlockSpec((BIG_TILE, D_MIX), lambda i, j: (i, 0))
    wsp = pl.BlockSpec((None, D_MIX, tn), lambda i, j: (layer, 0, j))
    gate = lambda off: pl.BlockSpec((BIG_TILE, tn), lambda i, j: (i, off // tn + j))
    return pl.pallas_call(
        _merge_kernel,
        out_shape=jax.ShapeDtypeStruct((m, D_MODEL), BF16),
        grid=(m // BIG_TILE, D_MODEL // tn),
        in_specs=[act, act, act, wsp, wsp, wsp, gate(OFF_GA), gate(OFF_GB2), gate(OFF_GC)],
        out_specs=pl.BlockSpec((BIG_TILE, tn), lambda i, j: (i, j)),
        compiler_params=_cparams(("parallel", "parallel")),
        name="merge",
    )(y_a, y_b, y_c, w_a, w_b, w_c, y_in, y_in, y_in)


def _out_proj_kernel(mg_ref, w_ref, x_ref, g1_ref, ng_ref, sc_ref, sh_ref, wr_ref, xo_ref, u_ref, lg_ref):
    xn = x_ref[...] + g1_ref[...] * _dot(mg_ref[...], w_ref[...])
    xo_ref[...] = xn
    u = _rms(xn, ng_ref[...]) * (1.0 + sc_ref[...]) + sh_ref[...]
    u_ref[...] = u.astype(BF16)
    w = wr_ref[...]
    uh = u.astype(BF16)
    ul = (u - uh.astype(F32)).astype(BF16)
    wh = w.astype(BF16)
    wl = (w - wh.astype(F32)).astype(BF16)
    lg_ref[...] = _dot(uh, wh) + (_dot(uh, wl) + _dot(ul, wh))


def _out_proj(merged, w_out_bf16, x, mod, norm_g, w_router_pad, layer, n_pt):
    m = x.shape[0]
    tm = ROW_TILE
    row = _mod_row(tm, n_pt)
    return pl.pallas_call(
        _out_proj_kernel,
        out_shape=(jax.ShapeDtypeStruct((m, D_MODEL), F32),
                   jax.ShapeDtypeStruct((m, D_MODEL), BF16),
                   jax.ShapeDtypeStruct((m, E_PAD), F32)),
        grid=(m // tm,),
        in_specs=[
            pl.BlockSpec((tm, D_MODEL), lambda i: (i, 0)),
            pl.BlockSpec((None, D_MODEL, D_MODEL), lambda i: (layer, 0, 0)),
            pl.BlockSpec((tm, D_MODEL), lambda i: (i, 0)),
            pl.BlockSpec((None, None, 1, D_MODEL), lambda i: (row(i), 2, 0, 0)),
            pl.BlockSpec((None, 1, D_MODEL), lambda i: (layer, 0, 0)),
            *_mod_specs(row, 1),
            pl.BlockSpec((None, D_MODEL, E_PAD), lambda i: (layer, 0, 0)),
        ],
        out_specs=(pl.BlockSpec((tm, D_MODEL), lambda i: (i, 0)),
                   pl.BlockSpec((tm, D_MODEL), lambda i: (i, 0)),
                   pl.BlockSpec((tm, E_PAD), lambda i: (i, 0))),
        compiler_params=_cparams(("parallel",)),
        name="out_proj",
    )(merged, w_out_bf16, x, mod, norm_g, mod, mod, w_router_pad)


def _prefix_matrices(seq):
    t = np.arange(TILE)
    before = t[:, None] < t[None, :]
    same = (t[:, None] // seq) == (t[None, :] // seq)
    return np.stack([before & same, before]).astype(np.float32)


def _route_kernel(n_pt, seq, lg_ref, u_ref, tri_ref, xin_ref, pt_ref, gate_ref, p_ref):
    i = pl.program_id(0)

    def body(seg_len):
        n_seg = TILE // seg_len
        cap = EC_FACTOR * seg_len // N_EXPERTS
        lt = lg_ref[...].T[0:N_EXPERTS]
        ex = jnp.exp(lt - jnp.max(lt, axis=0, keepdims=True))
        aff = ex / jnp.sum(ex, axis=0, keepdims=True)
        bits = pltpu.bitcast(aff, jnp.int32)
        segs = [bits[:, s * seg_len:(s + 1) * seg_len] for s in range(n_seg)]

        def count_ge(seg, thr):
            return jnp.sum(jnp.where(seg >= thr, 1.0, 0.0), axis=1, keepdims=True)

        def bisect(_, carry):
            out = []
            for seg, (lo, hi) in zip(segs, carry):
                mid = lo + ((hi - lo + 1) >> 1)
                ok = count_ge(seg, mid) >= cap
                out.append((jnp.where(ok, mid, lo), jnp.where(ok, hi, mid - 1)))
            return tuple(out)

        init = tuple((jnp.zeros((N_EXPERTS, 1), jnp.int32),
                      jnp.full((N_EXPERTS, 1), F32_BITS_MAX_FINITE, jnp.int32)) for _ in segs)
        bounds = lax.fori_loop(0, 31, bisect, init)
        gt_l, eq_l, need_l = [], [], []
        for seg, (thr, _) in zip(segs, bounds):
            gt = seg > thr
            gt_l.append(jnp.where(gt, 1.0, 0.0))
            eq_l.append(jnp.where(seg == thr, 1.0, 0.0))
            n_gt = jnp.sum(gt_l[-1], axis=1, keepdims=True)
            need_l.append(jnp.broadcast_to(cap - n_gt, (N_EXPERTS, seg_len)))
        gt = jnp.concatenate(gt_l, axis=1)
        eq = jnp.concatenate(eq_l, axis=1)
        need = jnp.concatenate(need_l, axis=1)
        before = _dot(jnp.concatenate([gt, eq], axis=0).astype(BF16), tri_ref[...])
        gt_before, eq_before = before[0:N_EXPERTS], before[N_EXPERTS:]
        chosen = jnp.logical_or(gt > 0.0, jnp.logical_and(eq > 0.0, eq_before < need))
        tok = lax.broadcasted_iota(jnp.int32, (N_EXPERTS, TILE), 1)
        seg_off = ((tok // seg_len) * cap).astype(F32)
        slot = jnp.where(chosen, gt_before + jnp.minimum(eq_before, need) + seg_off, -1.0)
        slot_t = jnp.concatenate([slot, jnp.full((E_PAD - N_EXPERTS, TILE), -1.0, F32)], axis=0).T
        sub = lax.broadcasted_iota(jnp.int32, (SLOTS, TILE), 0).astype(F32)
        lane = lax.broadcasted_iota(jnp.int32, (TILE, SLOTS), 1).astype(F32)
        for e in range(N_EXPERTS):
            hit = slot[e:e + 1] == sub
            gate_ref[e] = jnp.sum(jnp.where(hit, aff[e:e + 1], 0.0), axis=1, keepdims=True)
            p_ref[e * SLOTS:(e + 1) * SLOTS, :] = jnp.where(hit, 1.0, 0.0).astype(BF16)
            hit_t = slot_t[:, e:e + 1] == lane
            pt_ref[:, e * SLOTS:(e + 1) * SLOTS] = jnp.where(hit_t, 1.0, 0.0).astype(BF16)
        tn = 512
        for n in range(D_MODEL // tn):
            rows = _dot(p_ref[...], u_ref[:, n * tn:(n + 1) * tn])
            xin_ref[:, :, n * tn:(n + 1) * tn] = rows.astype(BF16).reshape(N_EXPERTS, SLOTS, tn)

    @pl.when(i < n_pt)
    def _():
        body(seq)

    @pl.when(i >= n_pt)
    def _():
        body(TILE)


def _route(logits, u, tri, n_pt, seq):
    m = u.shape[0]
    nt = m // TILE
    return pl.pallas_call(
        functools.partial(_route_kernel, n_pt, seq),
        out_shape=(jax.ShapeDtypeStruct((N_EXPERTS, nt * SLOTS, D_MODEL), BF16),
                   jax.ShapeDtypeStruct((m, N_EXPERTS * SLOTS), BF16),
                   jax.ShapeDtypeStruct((N_EXPERTS, nt * SLOTS, 1), F32)),
        grid=(nt,),
        in_specs=[pl.BlockSpec((TILE, E_PAD), lambda i: (i, 0)),
                  pl.BlockSpec((TILE, D_MODEL), lambda i: (i, 0)),
                  pl.BlockSpec((None, TILE, TILE), lambda i: (jnp.where(i < n_pt, 0, 1), 0, 0))],
        out_specs=(pl.BlockSpec((N_EXPERTS, SLOTS, D_MODEL), lambda i: (0, i, 0)),
                   pl.BlockSpec((TILE, N_EXPERTS * SLOTS), lambda i: (i, 0)),
                   pl.BlockSpec((N_EXPERTS, SLOTS, 1), lambda i: (0, i, 0))),
        scratch_shapes=[pltpu.VMEM((N_EXPERTS * SLOTS, TILE), BF16)],
        compiler_params=_cparams(("parallel",)),
        name="route",
    )(logits, u, tri)


def _expert_kernel(n_f, x_ref, wg_ref, wu_ref, wd_ref, gate_ref, o_ref, h_ref):
    s = pl.program_id(1)
    tf = wg_ref.shape[-1]

    @pl.when(s < n_f)
    def _():
        x = x_ref[...]
        h = _silu(_dot(x, wg_ref[...].astype(BF16))) * _dot(x, wu_ref[...].astype(BF16))
        h_ref[s] = h.astype(BF16)

    @pl.when(s >= n_f)
    def _():
        acc = _dot(h_ref[0], wd_ref[0:tf, :].astype(BF16))
        for k in range(1, n_f):
            acc = acc + _dot(h_ref[k], wd_ref[k * tf:(k + 1) * tf, :].astype(BF16))
        o_ref[...] = (acc * gate_ref[...]).astype(BF16)


def _experts(xin, gates, w_gate, w_up, w_down, layer):
    _, rows, _ = xin.shape
    tf, tn = 256, 1024
    n_f, n_o = D_EXPERT // tf, D_MODEL // tn
    f_idx = lambda s: jnp.minimum(s, n_f - 1)
    o_idx = lambda s: jnp.maximum(s - n_f, 0)
    return pl.pallas_call(
        functools.partial(_expert_kernel, n_f),
        out_shape=jax.ShapeDtypeStruct((N_EXPERTS, rows, D_MODEL), BF16),
        grid=(N_EXPERTS, n_f + n_o),
        in_specs=[
            pl.BlockSpec((None, rows, D_MODEL), lambda e, s: (e, 0, 0)),
            pl.BlockSpec((None, None, D_MODEL, tf), lambda e, s: (layer, e, 0, f_idx(s))),
            pl.BlockSpec((None, None, D_MODEL, tf), lambda e, s: (layer, e, 0, f_idx(s))),
            pl.BlockSpec((None, None, D_EXPERT, tn), lambda e, s: (layer, e, 0, o_idx(s))),
            pl.BlockSpec((None, rows, 1), lambda e, s: (e, 0, 0)),
        ],
        out_specs=pl.BlockSpec((None, rows, tn), lambda e, s: (e, 0, o_idx(s))),
        scratch_shapes=[pltpu.VMEM((n_f, rows, tf), BF16)],
        compiler_params=pltpu.CompilerParams(dimension_semantics=("parallel", "arbitrary"),
                                             vmem_limit_bytes=EXPERT_VMEM_LIMIT),
        name="experts",
    )(xin, w_gate, w_up, w_down, gates)


def _scattered_rows(pt_ref, y_ref, x_ref, g2_ref):
    y = y_ref[...].reshape(N_EXPERTS * SLOTS, D_MODEL)
    return x_ref[...] + g2_ref[...] * _dot(pt_ref[...], y)


def _scatter_kernel(pt_ref, y_ref, x_ref, g2_ref, ng_ref, sc_ref, sh_ref, xo_ref, u_ref):
    xn = _scattered_rows(pt_ref, y_ref, x_ref, g2_ref)
    xo_ref[...] = xn
    u_ref[...] = (_rms(xn, ng_ref[...]) * (1.0 + sc_ref[...]) + sh_ref[...]).astype(BF16)


def _scatter_final_kernel(pt_ref, y_ref, x_ref, g2_ref, ng_ref, o_ref):
    o_ref[...] = _rms(_scattered_rows(pt_ref, y_ref, x_ref, g2_ref), ng_ref[...])


def _scatter(pt, y, x, mod, n_pt, next_norm=None, final_g=None):
    m = x.shape[0]
    tm = ROW_TILE
    row = _mod_row(tm, n_pt)
    in_specs = [
        pl.BlockSpec((tm, N_EXPERTS * SLOTS), lambda i: (i, 0)),
        pl.BlockSpec((N_EXPERTS, SLOTS, D_MODEL), lambda i: (0, i // (TILE // tm), 0)),
        pl.BlockSpec((tm, D_MODEL), lambda i: (i, 0)),
        pl.BlockSpec((None, None, 1, D_MODEL), lambda i: (row(i), 5, 0, 0)),
    ]
    rows = pl.BlockSpec((tm, D_MODEL), lambda i: (i, 0))
    if next_norm is None:
        return pl.pallas_call(
            _scatter_final_kernel,
            out_shape=jax.ShapeDtypeStruct((m, D_MODEL), F32),
            grid=(m // tm,),
            in_specs=in_specs + [pl.BlockSpec((1, D_MODEL), lambda i: (0, 0))],
            out_specs=rows,
            compiler_params=_cparams(("parallel",)),
            name="scatter_final",
        )(pt, y, x, mod, final_g.reshape(1, D_MODEL))
    norm_g, next_mod, next_layer = next_norm
    return pl.pallas_call(
        _scatter_kernel,
        out_shape=(jax.ShapeDtypeStruct((m, D_MODEL), F32), jax.ShapeDtypeStruct((m, D_MODEL), BF16)),
        grid=(m // tm,),
        in_specs=in_specs + [pl.BlockSpec((None, 1, D_MODEL), lambda i: (next_layer, 0, 0)),
                             *_mod_specs(row, 0)],
        out_specs=(rows, rows),
        compiler_params=_cparams(("parallel",)),
        name="scatter",
    )(pt, y, x, mod, norm_g, next_mod, next_mod)


def kernel(x_prompt, x_sample, cache_k, cache_v, state_rglru, state_hgrn, c, c_ctx, w_ada, b_ada, norm1_g, norm2_g, w_in, conv_w, conv_b, rg_wr, rg_br, rg_wi, rg_bi, rg_lambda, hgrn_lb_logits, hgrn_norm_g, q_norm_g, k_norm_g, w_branch_a, w_branch_b, w_branch_c, w_out, w_router, w_exp_gate, w_exp_up, w_exp_down, final_norm_g):
    bp, seq, _ = x_prompt.shape
    bs, dec_seq, _ = x_sample.shape
    depth = w_in.shape[0]
    past = cache_k.shape[2]
    m = bp * seq + bs * dec_seq
    assert dec_seq == TILE and TILE % seq == 0 and (bp * seq) % TILE == 0 and seq & (seq - 1) == 0
    assert seq % (Q_BLK * Q_PAR) == 0 and past == seq and m % BIG_TILE == 0 and seq == HGRN_ROWS
    n_pt = bp * seq // TILE

    x = jnp.concatenate([x_prompt.reshape(bp * seq, D_MODEL), x_sample.reshape(bs * dec_seq, D_MODEL)], axis=0)
    n_cond = -(-(1 + bs) // 8) * 8
    cond = jnp.concatenate([c_ctx[None], c, jnp.zeros((n_cond - 1 - bs, D_MODEL), F32)], axis=0)
    mod_all = _ada_mod(cond, w_ada, b_ada).reshape(depth, n_cond, N_SUB, 1, D_MODEL)
    rope_tabs = _rope_tables(dec_seq)
    mmats = jnp.asarray(_hgrn_matrices(), BF16)
    tri = jnp.asarray(_prefix_matrices(seq), BF16)
    w_router_pad = jnp.pad(w_router, ((0, 0), (0, 0), (0, E_PAD - N_EXPERTS)))
    ck = cache_k.reshape(bs, depth, past, KV_DIM)
    cv = cache_v.reshape(bs, depth, past, KV_DIM)
    n1 = norm1_g.reshape(depth, 1, D_MODEL)
    n2 = norm2_g.reshape(depth, 1, D_MODEL)
    w_out_bf16 = w_out.astype(BF16)

    k_list, v_list, ra_list, hb_list = [], [], [], []
    u1 = _norm_mod(x, n1, mod_all[0], 0, n_pt)
    for l in range(depth):
        mod = mod_all[l]
        y_in, kv = _in_proj(u1, w_in, l)
        y_a, fin_a = _mixer_a(y_in, conv_w, conv_b, rg_wr, rg_br, rg_wi, rg_bi, rg_lambda, state_rglru, l, n_pt, seq)
        y_b, fin_b = _mixer_b(y_in, hgrn_lb_logits, hgrn_norm_g, state_hgrn, mmats, l, n_pt, seq)
        y_c, k_n = _attention(y_in, kv, ck, cv, rope_tabs, q_norm_g, k_norm_g, l, n_pt, seq)
        merged = _merge(y_a, y_b, y_c, y_in, w_branch_a, w_branch_b, w_branch_c, l)
        x, u2, logits = _out_proj(merged, w_out_bf16, x, mod, n2, w_router_pad, l, n_pt)
        xin, pt, gates = _route(logits, u2, tri, n_pt, seq)
        y_e = _experts(xin, gates, w_exp_gate, w_exp_up, w_exp_down, l)
        if l + 1 < depth:
            x, u1 = _scatter(pt, y_e, x, mod, n_pt, next_norm=(n1, mod_all[l + 1], l + 1))
        else:
            y_all = _scatter(pt, y_e, x, mod, n_pt, final_g=final_norm_g)
        k_list.append(k_n[:bp * seq].reshape(bp, seq, KV_HEADS, HEAD))
        v_list.append(kv[:bp * seq, KV_DIM:].reshape(bp, seq, KV_HEADS, HEAD))
        ra_list.append(fin_a[:n_pt].reshape(bp, 2, D_MIX))
        hb_list.append(fin_b[:n_pt].reshape(bp, 2, N_HEADS, HEAD, HEAD))

    y_prompt = y_all[:bp * seq].reshape(bp, seq, D_MODEL)
    y_sample = y_all[bp * seq:].reshape(bs, dec_seq, D_MODEL)
    return (y_prompt, y_sample, jnp.stack(k_list, axis=1), jnp.stack(v_list, axis=1),
            jnp.stack(ra_list, axis=1), jnp.stack(hb_list, axis=1))
```

```python
import functools

import numpy as np
import jax
import jax.numpy as jnp
from jax import lax
from jax.experimental import pallas as pl
from jax.experimental.pallas import tpu as pltpu

F32 = jnp.float32
BF16 = jnp.bfloat16

D_MODEL = 2048
D_MIX = D_MODEL // 2
HEAD = 128
N_HEADS = D_MIX // HEAD
CONV_W = 4
RG_C = 8.0
CHUNK = 64
N_LEVELS = 6
KV_HEADS = 2
KV_DIM = KV_HEADS * HEAD
GROUP = N_HEADS // KV_HEADS
ROPE_FREQS = HEAD // 4
ROPE_BASE = 10000.0
GRID_W = 64
N_EXPERTS = 16
EC_FACTOR = 2
D_EXPERT = D_MODEL // 2
N_SUB = 6
EPS = 1e-6
TILE = 1024
BIG_TILE = 2 * TILE
SLOTS = TILE * EC_FACTOR // N_EXPERTS
E_PAD = 128
Q_BLK = 128
Q_PAR = 2
HGRN_GROUP = 4
HGRN_ROWS = HGRN_GROUP * CHUNK
F32_BITS_MAX_FINITE = 0x7F7FFFFF
ROW_TILE = 256
VMEM_LIMIT = 48 * 1024 * 1024
EXPERT_VMEM_LIMIT = 56 * 1024 * 1024

OFF_XA, OFF_YA, OFF_QB, OFF_FZF, OFF_FZB, OFF_IB, OFF_GB, OFF_QC = [i * D_MIX for i in range(8)]
OFF_KC = 8 * D_MIX
OFF_VC = OFF_KC + KV_DIM
OFF_GA = OFF_VC + KV_DIM
OFF_GB2 = OFF_GA + D_MODEL
OFF_GC = OFF_GB2 + D_MODEL
N_IN = OFF_GC + D_MODEL
IN_TN = 2 * KV_DIM


def _cparams(sem):
    return pltpu.CompilerParams(dimension_semantics=sem, vmem_limit_bytes=VMEM_LIMIT)


def _sig(x):
    return 0.5 * jnp.tanh(0.5 * x) + 0.5


def _silu(x):
    return x * _sig(x)


def _dot(a, b):
    return jnp.dot(a, b, preferred_element_type=F32)


def _dot_nt(a, b):
    return lax.dot_general(a, b, (((1,), (1,)), ((), ())), preferred_element_type=F32)


def _rms(x, g):
    ms = jnp.mean(x * x, axis=-1, keepdims=True)
    return x * lax.rsqrt(ms + EPS) * g


def _mod_row(tile_rows, n_pt):
    per = TILE // tile_rows if tile_rows <= TILE else None
    if per is None:
        mult = tile_rows // TILE
        return lambda i: jnp.maximum(i * mult - n_pt + 1, 0)
    return lambda i: jnp.maximum(i // per - n_pt + 1, 0)


def _ada_kernel(c_ref, w_ref, b_ref, o_ref):
    s = _silu(c_ref[...]).astype(BF16)
    o_ref[...] = _dot(s, w_ref[...].astype(BF16)) + b_ref[...]


def _ada_mod(cond, w_ada, b_ada):
    depth, _, n = w_ada.shape
    rows = cond.shape[0]
    tn = 1024
    return pl.pallas_call(
        _ada_kernel,
        out_shape=jax.ShapeDtypeStruct((depth, rows, n), F32),
        grid=(depth, n // tn),
        in_specs=[
            pl.BlockSpec((rows, D_MODEL), lambda l, j: (0, 0)),
            pl.BlockSpec((None, D_MODEL, tn), lambda l, j: (l, 0, j)),
            pl.BlockSpec((None, 1, tn), lambda l, j: (l, 0, j)),
        ],
        out_specs=pl.BlockSpec((None, rows, tn), lambda l, j: (l, 0, j)),
        compiler_params=_cparams(("parallel", "parallel")),
        name="ada_mod",
    )(cond, w_ada, b_ada.reshape(depth, 1, n))


def _norm_mod_kernel(x_ref, g_ref, sc_ref, sh_ref, u_ref):
    u_ref[...] = (_rms(x_ref[...], g_ref[...]) * (1.0 + sc_ref[...]) + sh_ref[...]).astype(BF16)


def _mod_specs(mod_row, sub):
    mspec = lambda c: pl.BlockSpec((None, None, 1, D_MODEL), lambda i, *_: (mod_row(i), c, 0, 0))
    return mspec(3 * sub + 1), mspec(3 * sub)


def _norm_mod(x, norm_g, mod, layer, n_pt):
    m = x.shape[0]
    tm = 512
    return pl.pallas_call(
        _norm_mod_kernel,
        out_shape=jax.ShapeDtypeStruct((m, D_MODEL), BF16),
        grid=(m // tm,),
        in_specs=[pl.BlockSpec((tm, D_MODEL), lambda i: (i, 0)),
                  pl.BlockSpec((None, 1, D_MODEL), lambda i: (layer, 0, 0)),
                  *_mod_specs(_mod_row(tm, n_pt), 0)],
        out_specs=pl.BlockSpec((tm, D_MODEL), lambda i: (i, 0)),
        compiler_params=_cparams(("parallel",)), name="norm_mod",
    )(x, norm_g, mod, mod)


def _in_proj_kernel(kv_tile, u_ref, w_ref, o_ref, kv_ref):
    acc = _dot(u_ref[...], w_ref[...].astype(BF16))
    o_ref[...] = acc.astype(BF16)

    @pl.when(pl.program_id(1) == kv_tile)
    def _():
        kv_ref[...] = acc


def _in_proj(u, w_in, layer):
    m = u.shape[0]
    assert OFF_KC % IN_TN == 0 and N_IN % IN_TN == 0
    return pl.pallas_call(
        functools.partial(_in_proj_kernel, OFF_KC // IN_TN),
        out_shape=(jax.ShapeDtypeStruct((m, N_IN), BF16), jax.ShapeDtypeStruct((m, IN_TN), F32)),
        grid=(m // BIG_TILE, N_IN // IN_TN),
        in_specs=[
            pl.BlockSpec((BIG_TILE, D_MODEL), lambda i, j: (i, 0)),
            pl.BlockSpec((None, D_MODEL, IN_TN), lambda i, j: (layer, 0, j)),
        ],
        out_specs=(pl.BlockSpec((BIG_TILE, IN_TN), lambda i, j: (i, j)),
                   pl.BlockSpec((BIG_TILE, IN_TN), lambda i, j: (i, 0))),
        compiler_params=_cparams(("parallel", "arbitrary")),
        name="in_proj",
    )(u, w_in)


def _softplus(z):
    return jnp.maximum(z, 0.0) + jnp.log1p(jnp.exp(-jnp.abs(z)))


def _gelu_tanh(x):
    return 0.5 * x * (1.0 + jnp.tanh(np.sqrt(2.0 / np.pi).astype(np.float32) * (x + 0.044715 * (x * x * x))))


def _mixa_kernel(n_pt, seq, xa_ref, ya_ref, cw_ref, cb_ref, wr_ref, br_ref, wi_ref, bi_ref, lam_ref, h0_ref,
                 y_ref, fin_ref, af_ref, bf_ref, ab_ref, bb_ref, hf_ref, hb_ref):
    i = pl.program_id(0)

    def body(seg_len, use_state):
        x = xa_ref[...].astype(F32)
        pos = lax.broadcasted_iota(jnp.int32, (TILE, HEAD), 0) & (seg_len - 1)
        cw = cw_ref[...]
        xc = (jnp.where(pos >= 1, pltpu.roll(x, 1, 0), 0.0) * cw[0:1]
              + x * cw[1:2]
              + jnp.where(pos < seg_len - 1, pltpu.roll(x, TILE - 1, 0), 0.0) * cw[2:3]
              + jnp.where(pos < seg_len - 2, pltpu.roll(x, TILE - 2, 0), 0.0) * cw[3:4]
              + cb_ref[...])
        xcb = xc.astype(BF16)
        for d, (a_ref, b_ref) in enumerate(((af_ref, bf_ref), (ab_ref, bb_ref))):
            r = _sig(_dot(xcb, wr_ref[d].astype(BF16)) + br_ref[d])
            ig = _sig(_dot(xcb, wi_ref[d].astype(BF16)) + bi_ref[d])
            log_a = (-RG_C) * r * _softplus(-lam_ref[d])
            a = jnp.exp(log_a)
            a_ref[...] = a
            b_ref[...] = jnp.sqrt(-jnp.tanh(log_a) * (a * a + 1.0)) * (ig * xc)

        nb = seg_len // 8
        rowi = lax.broadcasted_iota(jnp.int32, (8, HEAD), 0)
        n_seg = TILE // seg_len
        for s in range(n_seg):
            base = s * seg_len
            if use_state:
                c0 = (h0_ref[0:1, :], h0_ref[1:2, :])
            else:
                c0 = (jnp.zeros((1, HEAD), F32), jnp.zeros((1, HEAD), F32))

            def step(j, carry):
                cf, cb = carry
                of = pl.multiple_of(base + j * 8, 8)
                ob = pl.multiple_of(base + (nb - 1 - j) * 8, 8)
                a = af_ref[pl.ds(of, 8), :]
                b = bf_ref[pl.ds(of, 8), :]
                for d in (1, 2, 4):
                    m = rowi >= d
                    b = jnp.where(m, a * pltpu.roll(b, d, 0) + b, b)
                    a = jnp.where(m, a * pltpu.roll(a, d, 0), a)
                h = b + a * cf
                hf_ref[pl.ds(of, 8), :] = h
                cf = h[7:8, :]
                a = ab_ref[pl.ds(ob, 8), :]
                b = bb_ref[pl.ds(ob, 8), :]
                for d in (1, 2, 4):
                    m = rowi < 8 - d
                    b = jnp.where(m, a * pltpu.roll(b, 8 - d, 0) + b, b)
                    a = jnp.where(m, a * pltpu.roll(a, 8 - d, 0), a)
                h = b + a * cb
                hb_ref[pl.ds(ob, 8), :] = h
                cb = h[0:1, :]
                return cf, cb

            cf, cb = lax.fori_loop(0, nb, step, c0, unroll=4)
            fin_ref[s, 0:1, :] = cf
            fin_ref[s, 1:2, :] = cb
        for s in range(n_seg, TILE // seq):
            fin_ref[s] = jnp.zeros((2, HEAD), F32)
        y_ref[...] = ((hf_ref[...] + hb_ref[...]) * _gelu_tanh(ya_ref[...].astype(F32))).astype(BF16)

    @pl.when(i < n_pt)
    def _():
        body(seq, False)

    @pl.when(i >= n_pt)
    def _():
        body(TILE, True)


def _mixer_a(y_in, conv_w, conv_b, rg_wr, rg_br, rg_wi, rg_bi, rg_lambda, state, layer, n_pt, seq):
    m = y_in.shape[0]
    depth = conv_w.shape[0]
    nt = m // TILE
    spt = TILE // seq
    req = lambda i: jnp.maximum(i - n_pt, 0)
    vec = lambda a: a.reshape(depth, 2, 1, D_MIX)
    gate_w = pl.BlockSpec((None, 2, None, HEAD, HEAD), lambda i, h: (layer, 0, h, 0, 0))
    gate_b = pl.BlockSpec((None, 2, 1, HEAD), lambda i, h: (layer, 0, 0, h))
    slab = lambda off: pl.BlockSpec((TILE, HEAD), lambda i, h: (i, off // HEAD + h))
    return pl.pallas_call(
        functools.partial(_mixa_kernel, n_pt, seq),
        out_shape=(jax.ShapeDtypeStruct((m, D_MIX), BF16),
                   jax.ShapeDtypeStruct((nt, spt, 2, D_MIX), F32)),
        grid=(nt, N_HEADS),
        in_specs=[
            slab(OFF_XA), slab(OFF_YA),
            pl.BlockSpec((None, CONV_W, HEAD), lambda i, h: (layer, 0, h)),
            pl.BlockSpec((None, 1, HEAD), lambda i, h: (layer, 0, h)),
            gate_w, gate_b, gate_w, gate_b, gate_b,
            pl.BlockSpec((None, None, 2, HEAD), lambda i, h: (req(i), layer, 0, h)),
        ],
        out_specs=(pl.BlockSpec((TILE, HEAD), lambda i, h: (i, h)),
                   pl.BlockSpec((None, spt, 2, HEAD), lambda i, h: (i, 0, 0, h))),
        scratch_shapes=[pltpu.VMEM((TILE, HEAD), F32)] * 6,
        compiler_params=_cparams(("parallel", "parallel")),
        name="mixer_a",
    )(y_in, y_in, conv_w, conv_b.reshape(depth, 1, D_MIX), rg_wr, vec(rg_br), rg_wi, vec(rg_bi), vec(rg_lambda),
      state)


def _hgrn_matrices():
    t = np.arange(CHUNK)
    return np.stack([t[None, :] <= t[:, None], t[None, :] >= t[:, None]]).astype(np.float32)


def _hgrn_decays(g, tri, reverse):
    c = CHUNK
    g_cat = jnp.concatenate([g[n * c:(n + 1) * c] for n in range(HGRN_GROUP)], axis=1)
    gh = g_cat.astype(BF16)
    gl = (g_cat - gh.astype(F32)).astype(BF16)
    cum = _dot(tri, gh) + _dot(tri, gl)
    end = cum[c - 1:c] if not reverse else cum[0:1]
    out = [cum, end - cum]
    rowi = lax.broadcasted_iota(jnp.int32, g_cat.shape, 0)
    m = c // 2
    while m >= 4:
        mids = [blk * 2 * m + (m if reverse else m - 1) for blk in range(c // (2 * m))]
        mid = jnp.concatenate([jnp.broadcast_to(cum[r:r + 1], (2 * m, g_cat.shape[1])) for r in mids], axis=0)
        out.append(-jnp.abs(cum - mid))
        m //= 2
    g_next = pltpu.roll(g_cat, c - 1, 0)
    g_prev = pltpu.roll(g_cat, 1, 0)
    p = rowi & 3
    if not reverse:
        out.append(jnp.where(p == 0, g_next, jnp.where(p == 1, 0.0, jnp.where(p == 2, g_cat, g_cat + g_prev))))
        out.append(jnp.where((rowi & 1) != 0, g_cat, 0.0))
    else:
        out.append(jnp.where(p == 0, g_cat + g_next, jnp.where(p == 1, g_cat, jnp.where(p == 2, 0.0, g_prev))))
        out.append(jnp.where((rowi & 1) != 0, 0.0, g_cat))
    return out


def _hgrn_side_masks(reverse):
    rowi = lax.broadcasted_iota(jnp.int32, (CHUNK, HEAD), 0)
    masks = []
    for lvl in range(N_LEVELS):
        q_side = ((rowi & (CHUNK >> (lvl + 1))) != 0) != reverse
        on = jnp.where(q_side, 1.0, 0.0)
        masks.append((on.astype(BF16), (1.0 - on).astype(BF16)))
    return masks


def _hgrn_pair_masks():
    ti = lax.broadcasted_iota(jnp.int32, (CHUNK, CHUNK), 0)
    si = lax.broadcasted_iota(jnp.int32, (CHUNK, CHUNK), 1)
    return ti == si, [(ti ^ si) < (CHUNK >> lvl) for lvl in range(N_LEVELS)]


def _hgrn_chunk_local(q, k, v, decays, masks, pair_masks, reverse):
    c = CHUNK
    e_in = jnp.exp2(decays[0])
    e_out = jnp.exp2(decays[1])
    vb = v.astype(BF16)
    qb = q.astype(BF16)
    kb = k.astype(BF16)
    diag, same_block = pair_masks
    scores = None
    for lvl in range(N_LEVELS):
        eb = jnp.exp2(decays[2 + lvl]).astype(BF16)
        mq, mk = masks[lvl]
        part = _dot_nt(qb * eb * mq, kb * eb * mk)
        scores = part if scores is None else jnp.where(same_block[lvl], part, scores)
    scores = jnp.where(diag, _dot_nt(qb, kb), scores)
    ke = (k * e_out).astype(BF16)
    own = lax.dot_general(vb, ke, (((0,), (0,)), ((), ())), preferred_element_type=F32)
    d_end = e_in[c - 1:c] if not reverse else e_in[0:1]
    return (q * e_in).astype(BF16), scores.astype(BF16), vb, own, d_end


def _mixb_kernel(n_pt, seq, layer, qb_ref, fzf_ref, fzb_ref, ib_ref, gb_ref, lbl_ref, s0_ref, ng_ref, mm_ref,
                 y_ref, sfin_ref, of_ref, ob_ref):
    i = pl.program_id(0)
    depth = lbl_ref.shape[0]

    def lower_bound(d):
        lg = [lbl_ref[j, d] for j in range(depth)]
        mx = functools.reduce(jnp.maximum, lg)
        ex = [jnp.exp(v - mx) for v in lg]
        tot = functools.reduce(lambda a, b: a + b, ex)
        acc = jnp.zeros_like(tot)
        for j in range(1, layer + 1):
            acc = acc + ex[j] / tot
        return acc

    def body(seg_len, use_state):
        lbs = (lower_bound(0), lower_bound(1))
        n_grp = seg_len // HGRN_ROWS
        chunk_rows = lambda a, n: a[n * CHUNK:(n + 1) * CHUNK]

        def load(r0, d):
            rows = pl.ds(r0, HGRN_ROWS)
            q = _silu(qb_ref[rows, :].astype(F32))
            v = ib_ref[rows, :].astype(F32)
            fz = (fzf_ref if d == 0 else fzb_ref)[rows, :].astype(F32)
            f = lbs[d] + (1.0 - lbs[d]) / (1.0 + jnp.exp(-fz))
            return q, 1.0 - f, v, jnp.log2(f)

        def group(r_f, r_b, st_f, st_b):
            qf, kf, vf, gf = load(r_f, 0)
            qb, kb, vb, gb = load(r_b, 1)
            pair_masks = _hgrn_pair_masks()
            masks_f = _hgrn_side_masks(False)
            masks_b = _hgrn_side_masks(True)
            ef = _hgrn_decays(gf, mm_ref[0], False)
            eb = _hgrn_decays(gb, mm_ref[1], True)
            lanes = lambda e, n: [a[:, n * HEAD:(n + 1) * HEAD] for a in e]
            loc_f = [_hgrn_chunk_local(chunk_rows(qf, n), chunk_rows(kf, n), chunk_rows(vf, n),
                                       lanes(ef, n), masks_f, pair_masks, False) for n in range(HGRN_GROUP)]
            loc_b = [_hgrn_chunk_local(chunk_rows(qb, n), chunk_rows(kb, n), chunk_rows(vb, n),
                                       lanes(eb, n), masks_b, pair_masks, True) for n in range(HGRN_GROUP)]
            for n in range(HGRN_GROUP):
                qe, sc, vals, own, d_end = loc_f[n]
                of_ref[pl.ds(r_f + n * CHUNK, CHUNK), :] = _dot_nt(qe, st_f.astype(BF16)) + _dot(sc, vals)
                st_f = st_f * d_end + own
            for n in reversed(range(HGRN_GROUP)):
                qe, sc, vals, own, d_end = loc_b[n]
                ob_ref[pl.ds(r_b + n * CHUNK, CHUNK), :] = _dot_nt(qe, st_b.astype(BF16)) + _dot(sc, vals)
                st_b = st_b * d_end + own
            return st_f, st_b

        if use_state:
            def step(j, carry):
                r_f = pl.multiple_of(j * HGRN_ROWS, HGRN_ROWS)
                r_b = pl.multiple_of((n_grp - 1 - j) * HGRN_ROWS, HGRN_ROWS)
                return group(r_f, r_b, *carry)

            st_f, st_b = lax.fori_loop(0, n_grp, step, (s0_ref[0].T, s0_ref[1].T))
            sfin_ref[0, 0] = st_f.T
            sfin_ref[0, 1] = st_b.T
            for s in range(1, TILE // seq):
                sfin_ref[s] = jnp.zeros((2, HEAD, HEAD), F32)
        else:
            def request(s, carry):
                r0 = pl.multiple_of(s * HGRN_ROWS, HGRN_ROWS)
                zero = jnp.zeros((HEAD, HEAD), F32)
                st_f, st_b = group(r0, r0, zero, zero)
                sfin_ref[s, 0] = st_f.T
                sfin_ref[s, 1] = st_b.T
                return carry

            lax.fori_loop(0, TILE // seg_len, request, 0)
        o = _rms(of_ref[...] + ob_ref[...], ng_ref[...]) * _silu(gb_ref[...].astype(F32))
        y_ref[...] = o.astype(BF16)

    @pl.when(i < n_pt)
    def _():
        body(seq, False)

    @pl.when(i >= n_pt)
    def _():
        body(TILE, True)


def _mixer_b(y_in, lb_logits, norm_g, state, mmats, layer, n_pt, seq):
    m = y_in.shape[0]
    depth = lb_logits.shape[0]
    nt = m // TILE
    spt = TILE // seq
    req = lambda i: jnp.maximum(i - n_pt, 0)
    slab = lambda off: pl.BlockSpec((TILE, HEAD), lambda i, h: (i, off // HEAD + h))
    return pl.pallas_call(
        functools.partial(_mixb_kernel, n_pt, seq, layer),
        out_shape=(jax.ShapeDtypeStruct((m, D_MIX), BF16),
                   jax.ShapeDtypeStruct((nt, spt, 2, N_HEADS, HEAD, HEAD), F32)),
        grid=(nt, N_HEADS),
        in_specs=[
            slab(OFF_QB), slab(OFF_FZF), slab(OFF_FZB), slab(OFF_IB), slab(OFF_GB),
            pl.BlockSpec((depth, 2, 1, HEAD), lambda i, h: (0, 0, 0, h)),
            pl.BlockSpec((None, None, 2, None, HEAD, HEAD), lambda i, h: (req(i), layer, 0, h, 0, 0)),
            pl.BlockSpec((None, 1, HEAD), lambda i, h: (layer, 0, 0)),
            pl.BlockSpec(mmats.shape, lambda i, h: (0, 0, 0)),
        ],
        out_specs=(pl.BlockSpec((TILE, HEAD), lambda i, h: (i, h)),
                   pl.BlockSpec((None, spt, 2, None, HEAD, HEAD), lambda i, h: (i, 0, 0, h, 0, 0))),
        scratch_shapes=[pltpu.VMEM((TILE, HEAD), F32)] * 2,
        compiler_params=_cparams(("parallel", "parallel")),
        name="mixer_b",
    )(y_in, y_in, y_in, y_in, y_in, lb_logits.reshape(depth, 2, 1, D_MIX), state,
      norm_g.reshape(depth, 1, HEAD), mmats)


def _rope(x, cos, sin_lo, sin_hi):
    return x * cos + pltpu.roll(x, HEAD - ROPE_FREQS, 1) * sin_lo + pltpu.roll(x, ROPE_FREQS, 1) * sin_hi


def _attn_kernel(n_pt, seq, q_ref, k_ref, v_ref, ck_ref, cv_ref, cos_ref, slo_ref, shi_ref, qg_ref, kg_ref,
                 y_ref, kn_ref, qs_ref, ks_ref, vs_ref):
    i = pl.program_id(0)
    scale = HEAD ** -0.5

    def body(seg_len, ctx):
        kn = _rms(k_ref[...], kg_ref[...])
        kn_ref[...] = kn
        if ctx:
            kn = _rope(kn, cos_ref[...], slo_ref[...], shi_ref[...])
        ks_ref[...] = kn.astype(BF16)
        vs_ref[...] = v_ref[...].astype(BF16)
        for g in range(GROUP):
            qn = _rms(q_ref[:, g * HEAD:(g + 1) * HEAD].astype(F32), qg_ref[...])
            if ctx:
                qn = _rope(qn, cos_ref[...], slo_ref[...], shi_ref[...])
            qs_ref[g] = (qn * scale).astype(BF16)
        if ctx:
            ckb = ck_ref[...].astype(BF16)
            cvb = cv_ref[...].astype(BF16)
        for s in range(TILE // seg_len):
            base = s * seg_len

            def qblocks(j, carry):
                keys = ks_ref[pl.ds(base, seg_len), :]
                vals = vs_ref[pl.ds(base, seg_len), :]
                rows = [pl.ds(pl.multiple_of(base + (j * Q_PAR + t) * Q_BLK, Q_BLK), Q_BLK) for t in range(Q_PAR)]
                qcat = [jnp.concatenate([qs_ref[g, r, :] for g in range(GROUP)], axis=0) for r in rows]
                sc = [_dot_nt(qc, keys) for qc in qcat]
                sc_c = [_dot_nt(qc, ckb) for qc in qcat] if ctx else None
                for t in range(Q_PAR):
                    mx = jnp.max(sc[t], axis=-1, keepdims=True)
                    if ctx:
                        mx = jnp.maximum(mx, jnp.max(sc_c[t], axis=-1, keepdims=True))
                    p = jnp.exp(sc[t] - mx)
                    den = jnp.sum(p, axis=-1, keepdims=True)
                    o = _dot(p.astype(BF16), vals)
                    if ctx:
                        pc = jnp.exp(sc_c[t] - mx)
                        den = den + jnp.sum(pc, axis=-1, keepdims=True)
                        o = o + _dot(pc.astype(BF16), cvb)
                    o = o / den
                    for g in range(GROUP):
                        y_ref[rows[t], g * HEAD:(g + 1) * HEAD] = o[g * Q_BLK:(g + 1) * Q_BLK].astype(BF16)
                return carry

            lax.fori_loop(0, seg_len // (Q_BLK * Q_PAR), qblocks, 0)

    @pl.when(i < n_pt)
    def _():
        body(seq, False)

    @pl.when(i >= n_pt)
    def _():
        body(TILE, True)


def _attention(y_in, kv, cache_k, cache_v, rope_tabs, q_norm_g, k_norm_g, layer, n_pt, seq):
    m = y_in.shape[0]
    depth = q_norm_g.shape[0]
    nt = m // TILE
    gw = GROUP * HEAD
    past = cache_k.shape[2]
    req = lambda i: jnp.maximum(i - n_pt, 0)
    ctx_spec = pl.BlockSpec((None, None, past, HEAD), lambda i, h: (req(i), layer, 0, h))
    tab = pl.BlockSpec((TILE, HEAD), lambda i, h: (0, 0))
    gspec = pl.BlockSpec((None, 1, HEAD), lambda i, h: (layer, 0, 0))
    return pl.pallas_call(
        functools.partial(_attn_kernel, n_pt, seq),
        out_shape=(jax.ShapeDtypeStruct((m, D_MIX), BF16),
                   jax.ShapeDtypeStruct((m, KV_DIM), F32)),
        grid=(nt, KV_HEADS),
        in_specs=[
            pl.BlockSpec((TILE, gw), lambda i, h: (i, OFF_QC // gw + h)),
            pl.BlockSpec((TILE, HEAD), lambda i, h: (i, h)),
            pl.BlockSpec((TILE, HEAD), lambda i, h: (i, KV_HEADS + h)),
            ctx_spec, ctx_spec, tab, tab, tab, gspec, gspec,
        ],
        out_specs=(pl.BlockSpec((TILE, gw), lambda i, h: (i, h)),
                   pl.BlockSpec((TILE, HEAD), lambda i, h: (i, h))),
        scratch_shapes=[pltpu.VMEM((GROUP, TILE, HEAD), BF16), pltpu.VMEM((TILE, HEAD), BF16),
                        pltpu.VMEM((TILE, HEAD), BF16)],
        compiler_params=_cparams(("parallel", "parallel")),
        name="attention",
    )(y_in, kv, kv, cache_k, cache_v, *rope_tabs, q_norm_g.reshape(depth, 1, HEAD),
      k_norm_g.reshape(depth, 1, HEAD))


def _rope_tables(n):
    rows = n // GRID_W
    r = jnp.repeat(jnp.arange(rows, dtype=F32), GRID_W)
    col = jnp.tile(jnp.arange(GRID_W, dtype=F32), rows)
    inv = ROPE_BASE ** (-jnp.arange(ROPE_FREQS, dtype=F32) / ROPE_FREQS)
    ar = r[:, None] * inv
    ac = col[:, None] * inv
    ang = jnp.concatenate([ar, ar, ac, ac], axis=-1)
    cos, sin = jnp.cos(ang), jnp.sin(ang)
    first_half = (jnp.arange(HEAD) // ROPE_FREQS) % 2 == 0
    return cos, jnp.where(first_half, -sin, 0.0), jnp.where(first_half, 0.0, sin)


def _merge_kernel(ya_ref, yb_ref, yc_ref, wa_ref, wb_ref, wc_ref, ga_ref, gb_ref, gc_ref, o_ref):
    acc = _sig(ga_ref[...].astype(F32)) * _dot(ya_ref[...], wa_ref[...].astype(BF16))
    acc = acc + _sig(gb_ref[...].astype(F32)) * _dot(yb_ref[...], wb_ref[...].astype(BF16))
    acc = acc + _sig(gc_ref[...].astype(F32)) * _dot(yc_ref[...], wc_ref[...].astype(BF16))
    o_ref[...] = acc.astype(BF16)


def _merge(y_a, y_b, y_c, y_in, w_a, w_b, w_c, layer):
    m = y_in.shape[0]
    tn = 256
    act = pl.BlockSpec((BIG_TILE, D_MIX), lambda i, j: (i, 0))
    wsp = pl.BlockSpec((None, D_MIX, tn), lambda i, j: (layer, 0, j))
    gate = lambda off: pl.BlockSpec((BIG_TILE, tn), lambda i, j: (i, off // tn + j))
    return pl.pallas_call(
        _merge_kernel,
        out_shape=jax.ShapeDtypeStruct((m, D_MODEL), BF16),
        grid=(m // BIG_TILE, D_MODEL // tn),
        in_specs=[act, act, act, wsp, wsp, wsp, gate(OFF_GA), gate(OFF_GB2), gate(OFF_GC)],
        out_specs=pl.BlockSpec((BIG_TILE, tn), lambda i, j: (i, j)),
        compiler_params=_cparams(("parallel", "parallel")),
        name="merge",
    )(y_a, y_b, y_c, w_a, w_b, w_c, y_in, y_in, y_in)


def _out_proj_kernel(mg_ref, w_ref, x_ref, g1_ref, ng_ref, sc_ref, sh_ref, wr_ref, xo_ref, u_ref, lg_ref):
    xn = x_ref[...] + g1_ref[...] * _dot(mg_ref[...], w_ref[...])
    xo_ref[...] = xn
    u = _rms(xn, ng_ref[...]) * (1.0 + sc_ref[...]) + sh_ref[...]
    u_ref[...] = u.astype(BF16)
    w = wr_ref[...]
    uh = u.astype(BF16)
    ul = (u - uh.astype(F32)).astype(BF16)
    wh = w.astype(BF16)
    wl = (w - wh.astype(F32)).astype(BF16)
    lg_ref[...] = _dot(uh, wh) + (_dot(uh, wl) + _dot(ul, wh))


def _out_proj(merged, w_out_bf16, x, mod, norm_g, w_router_pad, layer, n_pt):
    m = x.shape[0]
    tm = ROW_TILE
    row = _mod_row(tm, n_pt)
    return pl.pallas_call(
        _out_proj_kernel,
        out_shape=(jax.ShapeDtypeStruct((m, D_MODEL), F32),
                   jax.ShapeDtypeStruct((m, D_MODEL), BF16),
                   jax.ShapeDtypeStruct((m, E_PAD), F32)),
        grid=(m // tm,),
        in_specs=[
            pl.BlockSpec((tm, D_MODEL), lambda i: (i, 0)),
            pl.BlockSpec((None, D_MODEL, D_MODEL), lambda i: (layer, 0, 0)),
            pl.BlockSpec((tm, D_MODEL), lambda i: (i, 0)),
            pl.BlockSpec((None, None, 1, D_MODEL), lambda i: (row(i), 2, 0, 0)),
            pl.BlockSpec((None, 1, D_MODEL), lambda i: (layer, 0, 0)),
            *_mod_specs(row, 1),
            pl.BlockSpec((None, D_MODEL, E_PAD), lambda i: (layer, 0, 0)),
        ],
        out_specs=(pl.BlockSpec((tm, D_MODEL), lambda i: (i, 0)),
                   pl.BlockSpec((tm, D_MODEL), lambda i: (i, 0)),
                   pl.BlockSpec((tm, E_PAD), lambda i: (i, 0))),
        compiler_params=_cparams(("parallel",)),
        name="out_proj",
    )(merged, w_out_bf16, x, mod, norm_g, mod, mod, w_router_pad)


def _prefix_matrices(seq):
    t = np.arange(TILE)
    before = t[:, None] < t[None, :]
    same = (t[:, None] // seq) == (t[None, :] // seq)
    return np.stack([before & same, before]).astype(np.float32)


def _route_kernel(n_pt, seq, lg_ref, u_ref, tri_ref, xin_ref, pt_ref, gate_ref, p_ref):
    i = pl.program_id(0)

    def body(seg_len):
        n_seg = TILE // seg_len
        cap = EC_FACTOR * seg_len // N_EXPERTS
        lt = lg_ref[...].T[0:N_EXPERTS]
        ex = jnp.exp(lt - jnp.max(lt, axis=0, keepdims=True))
        aff = ex / jnp.sum(ex, axis=0, keepdims=True)
        bits = pltpu.bitcast(aff, jnp.int32)
        segs = [bits[:, s * seg_len:(s + 1) * seg_len] for s in range(n_seg)]

        def count_ge(seg, thr):
            return jnp.sum(jnp.where(seg >= thr, 1.0, 0.0), axis=1, keepdims=True)

        def bisect(_, carry):
            out = []
            for seg, (lo, hi) in zip(segs, carry):
                mid = lo + ((hi - lo + 1) >> 1)
                ok = count_ge(seg, mid) >= cap
                out.append((jnp.where(ok, mid, lo), jnp.where(ok, hi, mid - 1)))
            return tuple(out)

        init = tuple((jnp.zeros((N_EXPERTS, 1), jnp.int32),
                      jnp.full((N_EXPERTS, 1), F32_BITS_MAX_FINITE, jnp.int32)) for _ in segs)
        bounds = lax.fori_loop(0, 31, bisect, init)
        gt_l, eq_l, need_l = [], [], []
        for seg, (thr, _) in zip(segs, bounds):
            gt = seg > thr
            gt_l.append(jnp.where(gt, 1.0, 0.0))
            eq_l.append(jnp.where(seg == thr, 1.0, 0.0))
            n_gt = jnp.sum(gt_l[-1], axis=1, keepdims=True)
            need_l.append(jnp.broadcast_to(cap - n_gt, (N_EXPERTS, seg_len)))
        gt = jnp.concatenate(gt_l, axis=1)
        eq = jnp.concatenate(eq_l, axis=1)
        need = jnp.concatenate(need_l, axis=1)
        before = _dot(jnp.concatenate([gt, eq], axis=0).astype(BF16), tri_ref[...])
        gt_before, eq_before = before[0:N_EXPERTS], before[N_EXPERTS:]
        chosen = jnp.logical_or(gt > 0.0, jnp.logical_and(eq > 0.0, eq_before < need))
        tok = lax.broadcasted_iota(jnp.int32, (N_EXPERTS, TILE), 1)
        seg_off = ((tok // seg_len) * cap).astype(F32)
        slot = jnp.where(chosen, gt_before + jnp.minimum(eq_before, need) + seg_off, -1.0)
        slot_t = jnp.concatenate([slot, jnp.full((E_PAD - N_EXPERTS, TILE), -1.0, F32)], axis=0).T
        sub = lax.broadcasted_iota(jnp.int32, (SLOTS, TILE), 0).astype(F32)
        lane = lax.broadcasted_iota(jnp.int32, (TILE, SLOTS), 1).astype(F32)
        for e in range(N_EXPERTS):
            hit = slot[e:e + 1] == sub
            gate_ref[e] = jnp.sum(jnp.where(hit, aff[e:e + 1], 0.0), axis=1, keepdims=True)
            p_ref[e * SLOTS:(e + 1) * SLOTS, :] = jnp.where(hit, 1.0, 0.0).astype(BF16)
            hit_t = slot_t[:, e:e + 1] == lane
            pt_ref[:, e * SLOTS:(e + 1) * SLOTS] = jnp.where(hit_t, 1.0, 0.0).astype(BF16)
        tn = 512
        for n in range(D_MODEL // tn):
            rows = _dot(p_ref[...], u_ref[:, n * tn:(n + 1) * tn])
            xin_ref[:, :, n * tn:(n + 1) * tn] = rows.astype(BF16).reshape(N_EXPERTS, SLOTS, tn)

    @pl.when(i < n_pt)
    def _():
        body(seq)

    @pl.when(i >= n_pt)
    def _():
        body(TILE)


def _route(logits, u, tri, n_pt, seq):
    m = u.shape[0]
    nt = m // TILE
    return pl.pallas_call(
        functools.partial(_route_kernel, n_pt, seq),
        out_shape=(jax.ShapeDtypeStruct((N_EXPERTS, nt * SLOTS, D_MODEL), BF16),
                   jax.ShapeDtypeStruct((m, N_EXPERTS * SLOTS), BF16),
                   jax.ShapeDtypeStruct((N_EXPERTS, nt * SLOTS, 1), F32)),
        grid=(nt,),
        in_specs=[pl.BlockSpec((TILE, E_PAD), lambda i: (i, 0)),
                  pl.BlockSpec((TILE, D_MODEL), lambda i: (i, 0)),
                  pl.BlockSpec((None, TILE, TILE), lambda i: (jnp.where(i < n_pt, 0, 1), 0, 0))],
        out_specs=(pl.BlockSpec((N_EXPERTS, SLOTS, D_MODEL), lambda i: (0, i, 0)),
                   pl.BlockSpec((TILE, N_EXPERTS * SLOTS), lambda i: (i, 0)),
                   pl.BlockSpec((N_EXPERTS, SLOTS, 1), lambda i: (0, i, 0))),
        scratch_shapes=[pltpu.VMEM((N_EXPERTS * SLOTS, TILE), BF16)],
        compiler_params=_cparams(("parallel",)),
        name="route",
    )(logits, u, tri)


def _expert_kernel(n_f, x_ref, wg_ref, wu_ref, wd_ref, gate_ref, o_ref, h_ref):
    s = pl.program_id(1)
    tf = wg_ref.shape[-1]

    @pl.when(s < n_f)
    def _():
        x = x_ref[...]
        h = _silu(_dot(x, wg_ref[...].astype(BF16))) * _dot(x, wu_ref[...].astype(BF16))
        h_ref[s] = h.astype(BF16)

    @pl.when(s >= n_f)
    def _():
        acc = _dot(h_ref[0], wd_ref[0:tf, :].astype(BF16))
        for k in range(1, n_f):
            acc = acc + _dot(h_ref[k], wd_ref[k * tf:(k + 1) * tf, :].astype(BF16))
        o_ref[...] = (acc * gate_ref[...]).astype(BF16)


def _experts(xin, gates, w_gate, w_up, w_down, layer):
    _, rows, _ = xin.shape
    tf, tn = 256, 1024
    n_f, n_o = D_EXPERT // tf, D_MODEL // tn
    f_idx = lambda s: jnp.minimum(s, n_f - 1)
    o_idx = lambda s: jnp.maximum(s - n_f, 0)
    x_idx = lambda e, s: jnp.minimum(e + jnp.where(s >= n_f, 1, 0), N_EXPERTS - 1)
    return pl.pallas_call(
        functools.partial(_expert_kernel, n_f),
        out_shape=jax.ShapeDtypeStruct((N_EXPERTS, rows, D_MODEL), BF16),
        grid=(N_EXPERTS, n_f + n_o),
        in_specs=[
            pl.BlockSpec((None, rows, D_MODEL), lambda e, s: (x_idx(e, s), 0, 0)),
            pl.BlockSpec((None, None, D_MODEL, tf), lambda e, s: (layer, e, 0, f_idx(s))),
            pl.BlockSpec((None, None, D_MODEL, tf), lambda e, s: (layer, e, 0, f_idx(s))),
            pl.BlockSpec((None, None, D_EXPERT, tn), lambda e, s: (layer, e, 0, o_idx(s))),
            pl.BlockSpec((None, rows, 1), lambda e, s: (e, 0, 0)),
        ],
        out_specs=pl.BlockSpec((None, rows, tn), lambda e, s: (e, 0, o_idx(s))),
        scratch_shapes=[pltpu.VMEM((n_f, rows, tf), BF16)],
        compiler_params=pltpu.CompilerParams(dimension_semantics=("parallel", "arbitrary"),
                                             vmem_limit_bytes=EXPERT_VMEM_LIMIT),
        name="experts",
    )(xin, w_gate, w_up, w_down, gates)


def _scattered_rows(pt_ref, y_ref, x_ref, g2_ref):
    y = y_ref[...].reshape(N_EXPERTS * SLOTS, D_MODEL)
    return x_ref[...] + g2_ref[...] * _dot(pt_ref[...], y)


def _scatter_kernel(pt_ref, y_ref, x_ref, g2_ref, ng_ref, sc_ref, sh_ref, xo_ref, u_ref):
    xn = _scattered_rows(pt_ref, y_ref, x_ref, g2_ref)
    xo_ref[...] = xn
    u_ref[...] = (_rms(xn, ng_ref[...]) * (1.0 + sc_ref[...]) + sh_ref[...]).astype(BF16)


def _scatter_final_kernel(pt_ref, y_ref, x_ref, g2_ref, ng_ref, o_ref):
    o_ref[...] = _rms(_scattered_rows(pt_ref, y_ref, x_ref, g2_ref), ng_ref[...])


def _scatter(pt, y, x, mod, n_pt, next_norm=None, final_g=None):
    m = x.shape[0]
    tm = ROW_TILE
    row = _mod_row(tm, n_pt)
    in_specs = [
        pl.BlockSpec((tm, N_EXPERTS * SLOTS), lambda i: (i, 0)),
        pl.BlockSpec((N_EXPERTS, SLOTS, D_MODEL), lambda i: (0, i // (TILE // tm), 0)),
        pl.BlockSpec((tm, D_MODEL), lambda i: (i, 0)),
        pl.BlockSpec((None, None, 1, D_MODEL), lambda i: (row(i), 5, 0, 0)),
    ]
    rows = pl.BlockSpec((tm, D_MODEL), lambda i: (i, 0))
    if next_norm is None:
        return pl.pallas_call(
            _scatter_final_kernel,
            out_shape=jax.ShapeDtypeStruct((m, D_MODEL), F32),
            grid=(m // tm,),
            in_specs=in_specs + [pl.BlockSpec((1, D_MODEL), lambda i: (0, 0))],
            out_specs=rows,
            compiler_params=_cparams(("parallel",)),
            name="scatter_final",
        )(pt, y, x, mod, final_g.reshape(1, D_MODEL))
    norm_g, next_mod, next_layer = next_norm
    return pl.pallas_call(
        _scatter_kernel,
        out_shape=(jax.ShapeDtypeStruct((m, D_MODEL), F32), jax.ShapeDtypeStruct((m, D_MODEL), BF16)),
        grid=(m // tm,),
        in_specs=in_specs + [pl.BlockSpec((None, 1, D_MODEL), lambda i: (next_layer, 0, 0)),
                             *_mod_specs(row, 0)],
        out_specs=(rows, rows),
        compiler_params=_cparams(("parallel",)),
        name="scatter",
    )(pt, y, x, mod, norm_g, next_mod, next_mod)


def kernel(x_prompt, x_sample, cache_k, cache_v, state_rglru, state_hgrn, c, c_ctx, w_ada, b_ada, norm1_g, norm2_g, w_in, conv_w, conv_b, rg_wr, rg_br, rg_wi, rg_bi, rg_lambda, hgrn_lb_logits, hgrn_norm_g, q_norm_g, k_norm_g, w_branch_a, w_branch_b, w_branch_c, w_out, w_router, w_exp_gate, w_exp_up, w_exp_down, final_norm_g):
    bp, seq, _ = x_prompt.shape
    bs, dec_seq, _ = x_sample.shape
    depth = w_in.shape[0]
    past = cache_k.shape[2]
    m = bp * seq + bs * dec_seq
    assert dec_seq == TILE and TILE % seq == 0 and (bp * seq) % TILE == 0 and seq & (seq - 1) == 0
    assert seq % (Q_BLK * Q_PAR) == 0 and past == seq and m % BIG_TILE == 0 and seq == HGRN_ROWS
    n_pt = bp * seq // TILE

    x = jnp.concatenate([x_prompt.reshape(bp * seq, D_MODEL), x_sample.reshape(bs * dec_seq, D_MODEL)], axis=0)
    n_cond = -(-(1 + bs) // 8) * 8
    cond = jnp.concatenate([c_ctx[None], c, jnp.zeros((n_cond - 1 - bs, D_MODEL), F32)], axis=0)
    mod_all = _ada_mod(cond, w_ada, b_ada).reshape(depth, n_cond, N_SUB, 1, D_MODEL)
    rope_tabs = _rope_tables(dec_seq)
    mmats = jnp.asarray(_hgrn_matrices(), BF16)
    tri = jnp.asarray(_prefix_matrices(seq), BF16)
    w_router_pad = jnp.pad(w_router, ((0, 0), (0, 0), (0, E_PAD - N_EXPERTS)))
    ck = cache_k.reshape(bs, depth, past, KV_DIM)
    cv = cache_v.reshape(bs, depth, past, KV_DIM)
    n1 = norm1_g.reshape(depth, 1, D_MODEL)
    n2 = norm2_g.reshape(depth, 1, D_MODEL)
    w_out_bf16 = w_out.astype(BF16)

    k_list, v_list, ra_list, hb_list = [], [], [], []
    u1 = _norm_mod(x, n1, mod_all[0], 0, n_pt)
    for l in range(depth):
        mod = mod_all[l]
        y_in, kv = _in_proj(u1, w_in, l)
        y_a, fin_a = _mixer_a(y_in, conv_w, conv_b, rg_wr, rg_br, rg_wi, rg_bi, rg_lambda, state_rglru, l, n_pt, seq)
        y_b, fin_b = _mixer_b(y_in, hgrn_lb_logits, hgrn_norm_g, state_hgrn, mmats, l, n_pt, seq)
        y_c, k_n = _attention(y_in, kv, ck, cv, rope_tabs, q_norm_g, k_norm_g, l, n_pt, seq)
        merged = _merge(y_a, y_b, y_c, y_in, w_branch_a, w_branch_b, w_branch_c, l)
        x, u2, logits = _out_proj(merged, w_out_bf16, x, mod, n2, w_router_pad, l, n_pt)
        xin, pt, gates = _route(logits, u2, tri, n_pt, seq)
        y_e = _experts(xin, gates, w_exp_gate, w_exp_up, w_exp_down, l)
        if l + 1 < depth:
            x, u1 = _scatter(pt, y_e, x, mod, n_pt, next_norm=(n1, mod_all[l + 1], l + 1))
        else:
            y_all = _scatter(pt, y_e, x, mod, n_pt, final_g=final_norm_g)
        k_list.append(k_n[:bp * seq].reshape(bp, seq, KV_HEADS, HEAD))
        v_list.append(kv[:bp * seq, KV_DIM:].reshape(bp, seq, KV_HEADS, HEAD))
        ra_list.append(fin_a[:n_pt].reshape(bp, 2, D_MIX))
        hb_list.append(fin_b[:n_pt].reshape(bp, 2, N_HEADS, HEAD, HEAD))

    y_prompt = y_all[:bp * seq].reshape(bp, seq, D_MODEL)
    y_sample = y_all[bp * seq:].reshape(bs, dec_seq, D_MODEL)
    return (y_prompt, y_sample, jnp.stack(k_list, axis=1), jnp.stack(v_list, axis=1),
            jnp.stack(ra_list, axis=1), jnp.stack(hb_list, axis=1))
```

```python
import functools

import numpy as np
import jax
import jax.numpy as jnp
from jax import lax
from jax.experimental import pallas as pl
from jax.experimental.pallas import tpu as pltpu

F32 = jnp.float32
BF16 = jnp.bfloat16

D_MODEL = 2048
D_MIX = D_MODEL // 2
HEAD = 128
N_HEADS = D_MIX // HEAD
CONV_W = 4
RG_C = 8.0
CHUNK = 64
N_LEVELS = 6
KV_HEADS = 2
KV_DIM = KV_HEADS * HEAD
GROUP = N_HEADS // KV_HEADS
ROPE_FREQS = HEAD // 4
ROPE_BASE = 10000.0
GRID_W = 64
N_EXPERTS = 16
EC_FACTOR = 2
D_EXPERT = D_MODEL // 2
N_SUB = 6
EPS = 1e-6
TILE = 1024
BIG_TILE = 2 * TILE
SLOTS = TILE * EC_FACTOR // N_EXPERTS
E_PAD = 128
Q_BLK = 128
Q_PAR = 2
HGRN_GROUP = 4
HGRN_ROWS = HGRN_GROUP * CHUNK
F32_BITS_MAX_FINITE = 0x7F7FFFFF
ROW_TILE = 256
VMEM_LIMIT = 48 * 1024 * 1024
EXPERT_VMEM_LIMIT = 56 * 1024 * 1024

OFF_XA, OFF_YA, OFF_QB, OFF_FZF, OFF_FZB, OFF_IB, OFF_GB, OFF_QC = [i * D_MIX for i in range(8)]
OFF_KC = 8 * D_MIX
OFF_VC = OFF_KC + KV_DIM
OFF_GA = OFF_VC + KV_DIM
OFF_GB2 = OFF_GA + D_MODEL
OFF_GC = OFF_GB2 + D_MODEL
N_IN = OFF_GC + D_MODEL
IN_TN = 2 * KV_DIM


def _cparams(sem):
    return pltpu.CompilerParams(dimension_semantics=sem, vmem_limit_bytes=VMEM_LIMIT)


def _sig(x):
    return 0.5 * jnp.tanh(0.5 * x) + 0.5


def _silu(x):
    return x * _sig(x)


def _dot(a, b):
    return jnp.dot(a, b, preferred_element_type=F32)


def _dot_nt(a, b):
    return lax.dot_general(a, b, (((1,), (1,)), ((), ())), preferred_element_type=F32)


def _rms(x, g):
    ms = jnp.mean(x * x, axis=-1, keepdims=True)
    return x * lax.rsqrt(ms + EPS) * g


def _mod_row(tile_rows, n_pt):
    per = TILE // tile_rows if tile_rows <= TILE else None
    if per is None:
        mult = tile_rows // TILE
        return lambda i: jnp.maximum(i * mult - n_pt + 1, 0)
    return lambda i: jnp.maximum(i // per - n_pt + 1, 0)


def _ada_kernel(c_ref, w_ref, b_ref, o_ref):
    s = _silu(c_ref[...]).astype(BF16)
    o_ref[...] = _dot(s, w_ref[...].astype(BF16)) + b_ref[...]


def _ada_mod(cond, w_ada, b_ada):
    depth, _, n = w_ada.shape
    rows = cond.shape[0]
    tn = 1024
    return pl.pallas_call(
        _ada_kernel,
        out_shape=jax.ShapeDtypeStruct((depth, rows, n), F32),
        grid=(depth, n // tn),
        in_specs=[
            pl.BlockSpec((rows, D_MODEL), lambda l, j: (0, 0)),
            pl.BlockSpec((None, D_MODEL, tn), lambda l, j: (l, 0, j)),
            pl.BlockSpec((None, 1, tn), lambda l, j: (l, 0, j)),
        ],
        out_specs=pl.BlockSpec((None, rows, tn), lambda l, j: (l, 0, j)),
        compiler_params=_cparams(("parallel", "parallel")),
        name="ada_mod",
    )(cond, w_ada, b_ada.reshape(depth, 1, n))


def _norm_mod_kernel(p_steps, xp_ref, xs_ref, g_ref, sc_ref, sh_ref, x_ref, u_ref):
    i = pl.program_id(0)

    def emit(src_ref):
        x = src_ref[...]
        x_ref[...] = x
        u_ref[...] = (_rms(x, g_ref[...]) * (1.0 + sc_ref[...]) + sh_ref[...]).astype(BF16)

    @pl.when(i < p_steps)
    def _():
        emit(xp_ref)

    @pl.when(i >= p_steps)
    def _():
        emit(xs_ref)


def _mod_specs(mod_row, sub):
    mspec = lambda c: pl.BlockSpec((None, None, 1, D_MODEL), lambda i, *_: (mod_row(i), c, 0, 0))
    return mspec(3 * sub + 1), mspec(3 * sub)


def _norm_mod(x_prompt, x_sample, norm_g, mod, layer, n_pt):
    tm = 512
    p_steps = x_prompt.shape[0] // tm
    m = x_prompt.shape[0] + x_sample.shape[0]
    rows = pl.BlockSpec((tm, D_MODEL), lambda i: (i, 0))
    return pl.pallas_call(
        functools.partial(_norm_mod_kernel, p_steps),
        out_shape=(jax.ShapeDtypeStruct((m, D_MODEL), F32), jax.ShapeDtypeStruct((m, D_MODEL), BF16)),
        grid=(m // tm,),
        in_specs=[pl.BlockSpec((tm, D_MODEL), lambda i: (jnp.minimum(i, p_steps - 1), 0)),
                  pl.BlockSpec((tm, D_MODEL), lambda i: (jnp.maximum(i - p_steps, 0), 0)),
                  pl.BlockSpec((None, 1, D_MODEL), lambda i: (layer, 0, 0)),
                  *_mod_specs(_mod_row(tm, n_pt), 0)],
        out_specs=(rows, rows),
        compiler_params=_cparams(("parallel",)), name="norm_mod",
    )(x_prompt, x_sample, norm_g, mod, mod)


def _in_proj_kernel(kv_tile, u_ref, w_ref, o_ref, kv_ref):
    acc = _dot(u_ref[...], w_ref[...].astype(BF16))
    o_ref[...] = acc.astype(BF16)

    @pl.when(pl.program_id(1) == kv_tile)
    def _():
        kv_ref[...] = acc


def _in_proj(u, w_in, layer):
    m = u.shape[0]
    assert OFF_KC % IN_TN == 0 and N_IN % IN_TN == 0
    return pl.pallas_call(
        functools.partial(_in_proj_kernel, OFF_KC // IN_TN),
        out_shape=(jax.ShapeDtypeStruct((m, N_IN), BF16), jax.ShapeDtypeStruct((m, IN_TN), F32)),
        grid=(m // BIG_TILE, N_IN // IN_TN),
        in_specs=[
            pl.BlockSpec((BIG_TILE, D_MODEL), lambda i, j: (i, 0)),
            pl.BlockSpec((None, D_MODEL, IN_TN), lambda i, j: (layer, 0, j)),
        ],
        out_specs=(pl.BlockSpec((BIG_TILE, IN_TN), lambda i, j: (i, j)),
                   pl.BlockSpec((BIG_TILE, IN_TN), lambda i, j: (i, 0))),
        compiler_params=_cparams(("parallel", "arbitrary")),
        name="in_proj",
    )(u, w_in)


def _softplus(z):
    return jnp.maximum(z, 0.0) + jnp.log1p(jnp.exp(-jnp.abs(z)))


def _gelu_tanh(x):
    return 0.5 * x * (1.0 + jnp.tanh(np.sqrt(2.0 / np.pi).astype(np.float32) * (x + 0.044715 * (x * x * x))))


def _mixa_kernel(n_pt, seq, xa_ref, ya_ref, cw_ref, cb_ref, wr_ref, br_ref, wi_ref, bi_ref, lam_ref, h0_ref,
                 y_ref, fin_ref, af_ref, bf_ref, ab_ref, bb_ref, hf_ref, hb_ref):
    i = pl.program_id(0)

    def body(seg_len, use_state):
        x = xa_ref[...].astype(F32)
        pos = lax.broadcasted_iota(jnp.int32, (TILE, HEAD), 0) & (seg_len - 1)
        cw = cw_ref[...]
        xc = (jnp.where(pos >= 1, pltpu.roll(x, 1, 0), 0.0) * cw[0:1]
              + x * cw[1:2]
              + jnp.where(pos < seg_len - 1, pltpu.roll(x, TILE - 1, 0), 0.0) * cw[2:3]
              + jnp.where(pos < seg_len - 2, pltpu.roll(x, TILE - 2, 0), 0.0) * cw[3:4]
              + cb_ref[...])
        xcb = xc.astype(BF16)
        for d, (a_ref, b_ref) in enumerate(((af_ref, bf_ref), (ab_ref, bb_ref))):
            r = _sig(_dot(xcb, wr_ref[d].astype(BF16)) + br_ref[d])
            ig = _sig(_dot(xcb, wi_ref[d].astype(BF16)) + bi_ref[d])
            log_a = (-RG_C) * r * _softplus(-lam_ref[d])
            a = jnp.exp(log_a)
            a_ref[...] = a
            b_ref[...] = jnp.sqrt(-jnp.tanh(log_a) * (a * a + 1.0)) * (ig * xc)

        nb = seg_len // 8
        rowi = lax.broadcasted_iota(jnp.int32, (8, HEAD), 0)
        n_seg = TILE // seg_len
        for s in range(n_seg):
            base = s * seg_len
            if use_state:
                c0 = (h0_ref[0:1, :], h0_ref[1:2, :])
            else:
                c0 = (jnp.zeros((1, HEAD), F32), jnp.zeros((1, HEAD), F32))

            def step(j, carry):
                cf, cb = carry
                of = pl.multiple_of(base + j * 8, 8)
                ob = pl.multiple_of(base + (nb - 1 - j) * 8, 8)
                a = af_ref[pl.ds(of, 8), :]
                b = bf_ref[pl.ds(of, 8), :]
                for d in (1, 2, 4):
                    m = rowi >= d
                    b = jnp.where(m, a * pltpu.roll(b, d, 0) + b, b)
                    a = jnp.where(m, a * pltpu.roll(a, d, 0), a)
                h = b + a * cf
                hf_ref[pl.ds(of, 8), :] = h
                cf = h[7:8, :]
                a = ab_ref[pl.ds(ob, 8), :]
                b = bb_ref[pl.ds(ob, 8), :]
                for d in (1, 2, 4):
                    m = rowi < 8 - d
                    b = jnp.where(m, a * pltpu.roll(b, 8 - d, 0) + b, b)
                    a = jnp.where(m, a * pltpu.roll(a, 8 - d, 0), a)
                h = b + a * cb
                hb_ref[pl.ds(ob, 8), :] = h
                cb = h[0:1, :]
                return cf, cb

            cf, cb = lax.fori_loop(0, nb, step, c0, unroll=4)
            fin_ref[s, 0:1, :] = cf
            fin_ref[s, 1:2, :] = cb
        for s in range(n_seg, TILE // seq):
            fin_ref[s] = jnp.zeros((2, HEAD), F32)
        y_ref[...] = ((hf_ref[...] + hb_ref[...]) * _gelu_tanh(ya_ref[...].astype(F32))).astype(BF16)

    @pl.when(i < n_pt)
    def _():
        body(seq, False)

    @pl.when(i >= n_pt)
    def _():
        body(TILE, True)


def _mixer_a(y_in, conv_w, conv_b, rg_wr, rg_br, rg_wi, rg_bi, rg_lambda, state, layer, n_pt, seq):
    m = y_in.shape[0]
    depth = conv_w.shape[0]
    nt = m // TILE
    spt = TILE // seq
    req = lambda i: jnp.maximum(i - n_pt, 0)
    vec = lambda a: a.reshape(depth, 2, 1, D_MIX)
    gate_w = pl.BlockSpec((None, 2, None, HEAD, HEAD), lambda i, h: (layer, 0, h, 0, 0))
    gate_b = pl.BlockSpec((None, 2, 1, HEAD), lambda i, h: (layer, 0, 0, h))
    slab = lambda off: pl.BlockSpec((TILE, HEAD), lambda i, h: (i, off // HEAD + h))
    return pl.pallas_call(
        functools.partial(_mixa_kernel, n_pt, seq),
        out_shape=(jax.ShapeDtypeStruct((m, D_MIX), BF16),
                   jax.ShapeDtypeStruct((nt, spt, 2, D_MIX), F32)),
        grid=(nt, N_HEADS),
        in_specs=[
            slab(OFF_XA), slab(OFF_YA),
            pl.BlockSpec((None, CONV_W, HEAD), lambda i, h: (layer, 0, h)),
            pl.BlockSpec((None, 1, HEAD), lambda i, h: (layer, 0, h)),
            gate_w, gate_b, gate_w, gate_b, gate_b,
            pl.BlockSpec((None, None, 2, HEAD), lambda i, h: (req(i), layer, 0, h)),
        ],
        out_specs=(pl.BlockSpec((TILE, HEAD), lambda i, h: (i, h)),
                   pl.BlockSpec((None, spt, 2, HEAD), lambda i, h: (i, 0, 0, h))),
        scratch_shapes=[pltpu.VMEM((TILE, HEAD), F32)] * 6,
        compiler_params=_cparams(("parallel", "parallel")),
        name="mixer_a",
    )(y_in, y_in, conv_w, conv_b.reshape(depth, 1, D_MIX), rg_wr, vec(rg_br), rg_wi, vec(rg_bi), vec(rg_lambda),
      state)


def _hgrn_matrices():
    t = np.arange(CHUNK)
    return np.stack([t[None, :] <= t[:, None], t[None, :] >= t[:, None]]).astype(np.float32)


def _hgrn_decays(g, tri, reverse):
    c = CHUNK
    g_cat = jnp.concatenate([g[n * c:(n + 1) * c] for n in range(HGRN_GROUP)], axis=1)
    gh = g_cat.astype(BF16)
    gl = (g_cat - gh.astype(F32)).astype(BF16)
    cum = _dot(tri, gh) + _dot(tri, gl)
    end = cum[c - 1:c] if not reverse else cum[0:1]
    out = [cum, end - cum]
    rowi = lax.broadcasted_iota(jnp.int32, g_cat.shape, 0)
    m = c // 2
    while m >= 4:
        mids = [blk * 2 * m + (m if reverse else m - 1) for blk in range(c // (2 * m))]
        mid = jnp.concatenate([jnp.broadcast_to(cum[r:r + 1], (2 * m, g_cat.shape[1])) for r in mids], axis=0)
        out.append(pltpu.bitcast(pltpu.bitcast(cum - mid, jnp.uint32) | jnp.uint32(0x80000000), F32))
        m //= 2
    g_next = pltpu.roll(g_cat, c - 1, 0)
    g_prev = pltpu.roll(g_cat, 1, 0)
    p = rowi & 3
    if not reverse:
        out.append(jnp.where(p == 0, g_next, jnp.where(p == 1, 0.0, jnp.where(p == 2, g_cat, g_cat + g_prev))))
        out.append(jnp.where((rowi & 1) != 0, g_cat, 0.0))
    else:
        out.append(jnp.where(p == 0, g_cat + g_next, jnp.where(p == 1, g_cat, jnp.where(p == 2, 0.0, g_prev))))
        out.append(jnp.where((rowi & 1) != 0, 0.0, g_cat))
    return out


def _hgrn_side_masks(reverse):
    rowi = lax.broadcasted_iota(jnp.int32, (CHUNK, HEAD), 0)
    masks = []
    for lvl in range(N_LEVELS):
        q_side = ((rowi & (CHUNK >> (lvl + 1))) != 0) != reverse
        on = jnp.where(q_side, 1.0, 0.0)
        masks.append((on.astype(BF16), (1.0 - on).astype(BF16)))
    return masks


def _hgrn_pair_masks():
    ti = lax.broadcasted_iota(jnp.int32, (CHUNK, CHUNK), 0)
    si = lax.broadcasted_iota(jnp.int32, (CHUNK, CHUNK), 1)
    return ti == si, [(ti ^ si) < (CHUNK >> lvl) for lvl in range(N_LEVELS)]


def _hgrn_chunk_local(q, k, v, decays, masks, pair_masks, reverse):
    c = CHUNK
    e_in = jnp.exp2(decays[0])
    e_out = jnp.exp2(decays[1])
    vb = v.astype(BF16)
    qb = q.astype(BF16)
    kb = k.astype(BF16)
    diag, same_block = pair_masks
    scores = None
    for lvl in range(N_LEVELS):
        eb = jnp.exp2(decays[2 + lvl]).astype(BF16)
        mq, mk = masks[lvl]
        part = _dot_nt(qb * eb * mq, kb * eb * mk)
        scores = part if scores is None else jnp.where(same_block[lvl], part, scores)
    scores = jnp.where(diag, _dot_nt(qb, kb), scores)
    ke = (k * e_out).astype(BF16)
    own = lax.dot_general(vb, ke, (((0,), (0,)), ((), ())), preferred_element_type=F32)
    d_end = e_in[c - 1:c] if not reverse else e_in[0:1]
    return (q * e_in).astype(BF16), scores.astype(BF16), vb, own, d_end


def _mixb_kernel(n_pt, seq, layer, qb_ref, fzf_ref, fzb_ref, ib_ref, gb_ref, lbl_ref, s0_ref, ng_ref, mm_ref,
                 y_ref, sfin_ref, of_ref, ob_ref):
    i = pl.program_id(0)
    depth = lbl_ref.shape[0]

    def lower_bound(d):
        lg = [lbl_ref[j, d] for j in range(depth)]
        mx = functools.reduce(jnp.maximum, lg)
        ex = [jnp.exp(v - mx) for v in lg]
        tot = functools.reduce(lambda a, b: a + b, ex)
        acc = jnp.zeros_like(tot)
        for j in range(1, layer + 1):
            acc = acc + ex[j] / tot
        return acc

    def body(seg_len, use_state):
        lbs = (lower_bound(0), lower_bound(1))
        n_grp = seg_len // HGRN_ROWS
        chunk_rows = lambda a, n: a[n * CHUNK:(n + 1) * CHUNK]

        def load(r0, d):
            rows = pl.ds(r0, HGRN_ROWS)
            q = _silu(qb_ref[rows, :].astype(F32))
            v = ib_ref[rows, :].astype(F32)
            fz = (fzf_ref if d == 0 else fzb_ref)[rows, :].astype(F32)
            f = lbs[d] + (1.0 - lbs[d]) / (1.0 + jnp.exp(-fz))
            return q, 1.0 - f, v, jnp.log2(f)

        def group(r_f, r_b, st_f, st_b):
            qf, kf, vf, gf = load(r_f, 0)
            qb, kb, vb, gb = load(r_b, 1)
            pair_masks = _hgrn_pair_masks()
            masks_f = _hgrn_side_masks(False)
            masks_b = _hgrn_side_masks(True)
            ef = _hgrn_decays(gf, mm_ref[0], False)
            eb = _hgrn_decays(gb, mm_ref[1], True)
            lanes = lambda e, n: [a[:, n * HEAD:(n + 1) * HEAD] for a in e]
            loc_f = [_hgrn_chunk_local(chunk_rows(qf, n), chunk_rows(kf, n), chunk_rows(vf, n),
                                       lanes(ef, n), masks_f, pair_masks, False) for n in range(HGRN_GROUP)]
            loc_b = [_hgrn_chunk_local(chunk_rows(qb, n), chunk_rows(kb, n), chunk_rows(vb, n),
                                       lanes(eb, n), masks_b, pair_masks, True) for n in range(HGRN_GROUP)]
            for n in range(HGRN_GROUP):
                qe, sc, vals, own, d_end = loc_f[n]
                of_ref[pl.ds(r_f + n * CHUNK, CHUNK), :] = _dot_nt(qe, st_f.astype(BF16)) + _dot(sc, vals)
                st_f = st_f * d_end + own
            for n in reversed(range(HGRN_GROUP)):
                qe, sc, vals, own, d_end = loc_b[n]
                ob_ref[pl.ds(r_b + n * CHUNK, CHUNK), :] = _dot_nt(qe, st_b.astype(BF16)) + _dot(sc, vals)
                st_b = st_b * d_end + own
            return st_f, st_b

        if use_state:
            def step(j, carry):
                r_f = pl.multiple_of(j * HGRN_ROWS, HGRN_ROWS)
                r_b = pl.multiple_of((n_grp - 1 - j) * HGRN_ROWS, HGRN_ROWS)
                return group(r_f, r_b, *carry)

            st_f, st_b = lax.fori_loop(0, n_grp, step, (s0_ref[0].T, s0_ref[1].T))
            sfin_ref[0, 0] = st_f.T
            sfin_ref[0, 1] = st_b.T
            for s in range(1, TILE // seq):
                sfin_ref[s] = jnp.zeros((2, HEAD, HEAD), F32)
        else:
            def request(s, carry):
                r0 = pl.multiple_of(s * HGRN_ROWS, HGRN_ROWS)
                zero = jnp.zeros((HEAD, HEAD), F32)
                st_f, st_b = group(r0, r0, zero, zero)
                sfin_ref[s, 0] = st_f.T
                sfin_ref[s, 1] = st_b.T
                return carry

            lax.fori_loop(0, TILE // seg_len, request, 0)
        o = _rms(of_ref[...] + ob_ref[...], ng_ref[...]) * _silu(gb_ref[...].astype(F32))
        y_ref[...] = o.astype(BF16)

    @pl.when(i < n_pt)
    def _():
        body(seq, False)

    @pl.when(i >= n_pt)
    def _():
        body(TILE, True)


def _mixer_b(y_in, lb_logits, norm_g, state, mmats, layer, n_pt, seq):
    m = y_in.shape[0]
    depth = lb_logits.shape[0]
    nt = m // TILE
    spt = TILE // seq
    req = lambda i: jnp.maximum(i - n_pt, 0)
    slab = lambda off: pl.BlockSpec((TILE, HEAD), lambda i, h: (i, off // HEAD + h))
    return pl.pallas_call(
        functools.partial(_mixb_kernel, n_pt, seq, layer),
        out_shape=(jax.ShapeDtypeStruct((m, D_MIX), BF16),
                   jax.ShapeDtypeStruct((nt, spt, 2, N_HEADS, HEAD, HEAD), F32)),
        grid=(nt, N_HEADS),
        in_specs=[
            slab(OFF_QB), slab(OFF_FZF), slab(OFF_FZB), slab(OFF_IB), slab(OFF_GB),
            pl.BlockSpec((depth, 2, 1, HEAD), lambda i, h: (0, 0, 0, h)),
            pl.BlockSpec((None, None, 2, None, HEAD, HEAD), lambda i, h: (req(i), layer, 0, h, 0, 0)),
            pl.BlockSpec((None, 1, HEAD), lambda i, h: (layer, 0, 0)),
            pl.BlockSpec(mmats.shape, lambda i, h: (0, 0, 0)),
        ],
        out_specs=(pl.BlockSpec((TILE, HEAD), lambda i, h: (i, h)),
                   pl.BlockSpec((None, spt, 2, None, HEAD, HEAD), lambda i, h: (i, 0, 0, h, 0, 0))),
        scratch_shapes=[pltpu.VMEM((TILE, HEAD), F32)] * 2,
        compiler_params=_cparams(("parallel", "parallel")),
        name="mixer_b",
    )(y_in, y_in, y_in, y_in, y_in, lb_logits.reshape(depth, 2, 1, D_MIX), state,
      norm_g.reshape(depth, 1, HEAD), mmats)


def _rope(x, cos, sin_lo, sin_hi):
    return x * cos + pltpu.roll(x, HEAD - ROPE_FREQS, 1) * sin_lo + pltpu.roll(x, ROPE_FREQS, 1) * sin_hi


def _attn_kernel(n_pt, seq, q_ref, k_ref, v_ref, ck_ref, cv_ref, cos_ref, slo_ref, shi_ref, qg_ref, kg_ref,
                 y_ref, kn_ref, qs_ref, ks_ref, vs_ref):
    i = pl.program_id(0)
    scale = HEAD ** -0.5

    def body(seg_len, ctx):
        kn = _rms(k_ref[...], kg_ref[...])
        kn_ref[...] = kn
        if ctx:
            kn = _rope(kn, cos_ref[...], slo_ref[...], shi_ref[...])
        ks_ref[...] = kn.astype(BF16)
        vs_ref[...] = v_ref[...].astype(BF16)
        for g in range(GROUP):
            qn = _rms(q_ref[:, g * HEAD:(g + 1) * HEAD].astype(F32), qg_ref[...])
            if ctx:
                qn = _rope(qn, cos_ref[...], slo_ref[...], shi_ref[...])
            qs_ref[g] = (qn * scale).astype(BF16)
        if ctx:
            ckb = ck_ref[...].astype(BF16)
            cvb = cv_ref[...].astype(BF16)
        for s in range(TILE // seg_len):
            base = s * seg_len

            def qblocks(j, carry):
                keys = ks_ref[pl.ds(base, seg_len), :]
                vals = vs_ref[pl.ds(base, seg_len), :]
                rows = [pl.ds(pl.multiple_of(base + (j * Q_PAR + t) * Q_BLK, Q_BLK), Q_BLK) for t in range(Q_PAR)]
                qcat = [jnp.concatenate([qs_ref[g, r, :] for g in range(GROUP)], axis=0) for r in rows]
                sc = [_dot_nt(qc, keys) for qc in qcat]
                sc_c = [_dot_nt(qc, ckb) for qc in qcat] if ctx else None
                for t in range(Q_PAR):
                    mx = jnp.max(sc[t], axis=-1, keepdims=True)
                    if ctx:
                        mx = jnp.maximum(mx, jnp.max(sc_c[t], axis=-1, keepdims=True))
                    p = jnp.exp(sc[t] - mx)
                    den = jnp.sum(p, axis=-1, keepdims=True)
                    o = _dot(p.astype(BF16), vals)
                    if ctx:
                        pc = jnp.exp(sc_c[t] - mx)
                        den = den + jnp.sum(pc, axis=-1, keepdims=True)
                        o = o + _dot(pc.astype(BF16), cvb)
                    o = o / den
                    for g in range(GROUP):
                        y_ref[rows[t], g * HEAD:(g + 1) * HEAD] = o[g * Q_BLK:(g + 1) * Q_BLK].astype(BF16)
                return carry

            lax.fori_loop(0, seg_len // (Q_BLK * Q_PAR), qblocks, 0)

    @pl.when(i < n_pt)
    def _():
        body(seq, False)

    @pl.when(i >= n_pt)
    def _():
        body(TILE, True)


def _attention(y_in, kv, cache_k, cache_v, rope_tabs, q_norm_g, k_norm_g, layer, n_pt, seq):
    m = y_in.shape[0]
    depth = q_norm_g.shape[0]
    nt = m // TILE
    gw = GROUP * HEAD
    past = cache_k.shape[2]
    req = lambda i: jnp.maximum(i - n_pt, 0)
    ctx_spec = pl.BlockSpec((None, None, past, HEAD), lambda i, h: (req(i), layer, 0, h))
    tab = pl.BlockSpec((TILE, HEAD), lambda i, h: (0, 0))
    gspec = pl.BlockSpec((None, 1, HEAD), lambda i, h: (layer, 0, 0))
    return pl.pallas_call(
        functools.partial(_attn_kernel, n_pt, seq),
        out_shape=(jax.ShapeDtypeStruct((m, D_MIX), BF16),
                   jax.ShapeDtypeStruct((m, KV_DIM), F32)),
        grid=(nt, KV_HEADS),
        in_specs=[
            pl.BlockSpec((TILE, gw), lambda i, h: (i, OFF_QC // gw + h)),
            pl.BlockSpec((TILE, HEAD), lambda i, h: (i, h)),
            pl.BlockSpec((TILE, HEAD), lambda i, h: (i, KV_HEADS + h)),
            ctx_spec, ctx_spec, tab, tab, tab, gspec, gspec,
        ],
        out_specs=(pl.BlockSpec((TILE, gw), lambda i, h: (i, h)),
                   pl.BlockSpec((TILE, HEAD), lambda i, h: (i, h))),
        scratch_shapes=[pltpu.VMEM((GROUP, TILE, HEAD), BF16), pltpu.VMEM((TILE, HEAD), BF16),
                        pltpu.VMEM((TILE, HEAD), BF16)],
        compiler_params=_cparams(("parallel", "parallel")),
        name="attention",
    )(y_in, kv, kv, cache_k, cache_v, *rope_tabs, q_norm_g.reshape(depth, 1, HEAD),
      k_norm_g.reshape(depth, 1, HEAD))


def _rope_tables(n):
    rows = n // GRID_W
    r = jnp.repeat(jnp.arange(rows, dtype=F32), GRID_W)
    col = jnp.tile(jnp.arange(GRID_W, dtype=F32), rows)
    inv = ROPE_BASE ** (-jnp.arange(ROPE_FREQS, dtype=F32) / ROPE_FREQS)
    ar = r[:, None] * inv
    ac = col[:, None] * inv
    ang = jnp.concatenate([ar, ar, ac, ac], axis=-1)
    cos, sin = jnp.cos(ang), jnp.sin(ang)
    first_half = (jnp.arange(HEAD) // ROPE_FREQS) % 2 == 0
    return cos, jnp.where(first_half, -sin, 0.0), jnp.where(first_half, 0.0, sin)


def _merge_kernel(ya_ref, yb_ref, yc_ref, wa_ref, wb_ref, wc_ref, ga_ref, gb_ref, gc_ref, o_ref):
    acc = _sig(ga_ref[...].astype(F32)) * _dot(ya_ref[...], wa_ref[...].astype(BF16))
    acc = acc + _sig(gb_ref[...].astype(F32)) * _dot(yb_ref[...], wb_ref[...].astype(BF16))
    acc = acc + _sig(gc_ref[...].astype(F32)) * _dot(yc_ref[...], wc_ref[...].astype(BF16))
    o_ref[...] = acc.astype(BF16)


def _merge(y_a, y_b, y_c, y_in, w_a, w_b, w_c, layer):
    m = y_in.shape[0]
    tn = 256
    act = pl.BlockSpec((BIG_TILE, D_MIX), lambda i, j: (i, 0))
    wsp = pl.BlockSpec((None, D_MIX, tn), lambda i, j: (layer, 0, j))
    gate = lambda off: pl.BlockSpec((BIG_TILE, tn), lambda i, j: (i, off // tn + j))
    return pl.pallas_call(
        _merge_kernel,
        out_shape=jax.ShapeDtypeStruct((m, D_MODEL), BF16),
        grid=(m // BIG_TILE, D_MODEL // tn),
        in_specs=[act, act, act, wsp, wsp, wsp, gate(OFF_GA), gate(OFF_GB2), gate(OFF_GC)],
        out_specs=pl.BlockSpec((BIG_TILE, tn), lambda i, j: (i, j)),
        compiler_params=_cparams(("parallel", "parallel")),
        name="merge",
    )(y_a, y_b, y_c, w_a, w_b, w_c, y_in, y_in, y_in)


def _out_proj_kernel(mg_ref, w_ref, x_ref, g1_ref, ng_ref, sc_ref, sh_ref, wr_ref, xo_ref, u_ref, lg_ref):
    xn = x_ref[...] + g1_ref[...] * _dot(mg_ref[...], w_ref[...])
    xo_ref[...] = xn
    u = _rms(xn, ng_ref[...]) * (1.0 + sc_ref[...]) + sh_ref[...]
    u_ref[...] = u.astype(BF16)
    w = wr_ref[...]
    uh = u.astype(BF16)
    ul = (u - uh.astype(F32)).astype(BF16)
    wh = w.astype(BF16)
    wl = (w - wh.astype(F32)).astype(BF16)
    lg_ref[...] = _dot(uh, wh) + (_dot(uh, wl) + _dot(ul, wh))


def _out_proj(merged, w_out_bf16, x, mod, norm_g, w_router_pad, layer, n_pt):
    m = x.shape[0]
    tm = ROW_TILE
    row = _mod_row(tm, n_pt)
    return pl.pallas_call(
        _out_proj_kernel,
        out_shape=(jax.ShapeDtypeStruct((m, D_MODEL), F32),
                   jax.ShapeDtypeStruct((m, D_MODEL), BF16),
                   jax.ShapeDtypeStruct((m, E_PAD), F32)),
        grid=(m // tm,),
        in_specs=[
            pl.BlockSpec((tm, D_MODEL), lambda i: (i, 0)),
            pl.BlockSpec((None, D_MODEL, D_MODEL), lambda i: (layer, 0, 0)),
            pl.BlockSpec((tm, D_MODEL), lambda i: (i, 0)),
            pl.BlockSpec((None, None, 1, D_MODEL), lambda i: (row(i), 2, 0, 0)),
            pl.BlockSpec((None, 1, D_MODEL), lambda i: (layer, 0, 0)),
            *_mod_specs(row, 1),
            pl.BlockSpec((None, D_MODEL, E_PAD), lambda i: (layer, 0, 0)),
        ],
        out_specs=(pl.BlockSpec((tm, D_MODEL), lambda i: (i, 0)),
                   pl.BlockSpec((tm, D_MODEL), lambda i: (i, 0)),
                   pl.BlockSpec((tm, E_PAD), lambda i: (i, 0))),
        compiler_params=_cparams(("parallel",)),
        name="out_proj",
    )(merged, w_out_bf16, x, mod, norm_g, mod, mod, w_router_pad)


def _prefix_matrices(seq):
    t = np.arange(TILE)
    before = t[:, None] < t[None, :]
    same = (t[:, None] // seq) == (t[None, :] // seq)
    return np.stack([before & same, before]).astype(np.float32)


def _route_kernel(n_pt, seq, lg_ref, u_ref, tri_ref, xin_ref, pt_ref, gate_ref, p_ref):
    i = pl.program_id(0)

    def body(seg_len):
        n_seg = TILE // seg_len
        cap = EC_FACTOR * seg_len // N_EXPERTS
        lt = lg_ref[...].T[0:N_EXPERTS]
        ex = jnp.exp(lt - jnp.max(lt, axis=0, keepdims=True))
        aff = ex / jnp.sum(ex, axis=0, keepdims=True)
        bits = pltpu.bitcast(aff, jnp.int32)
        segs = [bits[:, s * seg_len:(s + 1) * seg_len] for s in range(n_seg)]

        def count_ge(seg, thr):
            return jnp.sum(jnp.where(seg >= thr, 1.0, 0.0), axis=1, keepdims=True)

        def bisect(_, carry):
            out = []
            for seg, (lo, hi) in zip(segs, carry):
                mid = lo + ((hi - lo + 1) >> 1)
                ok = count_ge(seg, mid) >= cap
                out.append((jnp.where(ok, mid, lo), jnp.where(ok, hi, mid - 1)))
            return tuple(out)

        init = tuple((jnp.zeros((N_EXPERTS, 1), jnp.int32),
                      jnp.full((N_EXPERTS, 1), F32_BITS_MAX_FINITE, jnp.int32)) for _ in segs)
        bounds = lax.fori_loop(0, 31, bisect, init)
        gt_l, eq_l, need_l = [], [], []
        for seg, (thr, _) in zip(segs, bounds):
            gt = seg > thr
            gt_l.append(jnp.where(gt, 1.0, 0.0))
            eq_l.append(jnp.where(seg == thr, 1.0, 0.0))
            n_gt = jnp.sum(gt_l[-1], axis=1, keepdims=True)
            need_l.append(jnp.broadcast_to(cap - n_gt, (N_EXPERTS, seg_len)))
        gt = jnp.concatenate(gt_l, axis=1)
        eq = jnp.concatenate(eq_l, axis=1)
        need = jnp.concatenate(need_l, axis=1)
        before = _dot(jnp.concatenate([gt, eq], axis=0).astype(BF16), tri_ref[...])
        gt_before, eq_before = before[0:N_EXPERTS], before[N_EXPERTS:]
        chosen = jnp.logical_or(gt > 0.0, jnp.logical_and(eq > 0.0, eq_before < need))
        tok = lax.broadcasted_iota(jnp.int32, (N_EXPERTS, TILE), 1)
        seg_off = ((tok // seg_len) * cap).astype(F32)
        slot = jnp.where(chosen, gt_before + jnp.minimum(eq_before, need) + seg_off, -1.0)
        slot_t = jnp.concatenate([slot, jnp.full((E_PAD - N_EXPERTS, TILE), -1.0, F32)], axis=0).T
        sub = lax.broadcasted_iota(jnp.int32, (SLOTS, TILE), 0).astype(F32)
        lane = lax.broadcasted_iota(jnp.int32, (TILE, SLOTS), 1).astype(F32)
        for e in range(N_EXPERTS):
            hit = slot[e:e + 1] == sub
            gate_ref[e] = jnp.sum(jnp.where(hit, aff[e:e + 1], 0.0), axis=1, keepdims=True)
            p_ref[e * SLOTS:(e + 1) * SLOTS, :] = jnp.where(hit, 1.0, 0.0).astype(BF16)
            hit_t = slot_t[:, e:e + 1] == lane
            pt_ref[:, e * SLOTS:(e + 1) * SLOTS] = jnp.where(hit_t, 1.0, 0.0).astype(BF16)
        tn = 512
        for n in range(D_MODEL // tn):
            rows = _dot(p_ref[...], u_ref[:, n * tn:(n + 1) * tn])
            xin_ref[:, :, n * tn:(n + 1) * tn] = rows.astype(BF16).reshape(N_EXPERTS, SLOTS, tn)

    @pl.when(i < n_pt)
    def _():
        body(seq)

    @pl.when(i >= n_pt)
    def _():
        body(TILE)


def _route(logits, u, tri, n_pt, seq):
    m = u.shape[0]
    nt = m // TILE
    return pl.pallas_call(
        functools.partial(_route_kernel, n_pt, seq),
        out_shape=(jax.ShapeDtypeStruct((N_EXPERTS, nt * SLOTS, D_MODEL), BF16),
                   jax.ShapeDtypeStruct((m, N_EXPERTS * SLOTS), BF16),
                   jax.ShapeDtypeStruct((N_EXPERTS, nt * SLOTS, 1), F32)),
        grid=(nt,),
        in_specs=[pl.BlockSpec((TILE, E_PAD), lambda i: (i, 0)),
                  pl.BlockSpec((TILE, D_MODEL), lambda i: (i, 0)),
                  pl.BlockSpec((None, TILE, TILE), lambda i: (jnp.where(i < n_pt, 0, 1), 0, 0))],
        out_specs=(pl.BlockSpec((N_EXPERTS, SLOTS, D_MODEL), lambda i: (0, i, 0)),
                   pl.BlockSpec((TILE, N_EXPERTS * SLOTS), lambda i: (i, 0)),
                   pl.BlockSpec((N_EXPERTS, SLOTS, 1), lambda i: (0, i, 0))),
        scratch_shapes=[pltpu.VMEM((N_EXPERTS * SLOTS, TILE), BF16)],
        compiler_params=_cparams(("parallel",)),
        name="route",
    )(logits, u, tri)


def _expert_kernel(n_f, x_ref, wg_ref, wu_ref, wd_ref, gate_ref, o_ref, h_ref):
    s = pl.program_id(1)
    tf = wg_ref.shape[-1]

    @pl.when(s < n_f)
    def _():
        x = x_ref[...]
        h = _silu(_dot(x, wg_ref[...].astype(BF16))) * _dot(x, wu_ref[...].astype(BF16))
        h_ref[s] = h.astype(BF16)

    @pl.when(s >= n_f)
    def _():
        acc = _dot(h_ref[0], wd_ref[0:tf, :].astype(BF16))
        for k in range(1, n_f):
            acc = acc + _dot(h_ref[k], wd_ref[k * tf:(k + 1) * tf, :].astype(BF16))
        o_ref[...] = (acc * gate_ref[...]).astype(BF16)


def _experts(xin, gates, w_gate, w_up, w_down, layer):
    _, rows, _ = xin.shape
    tf, tn = 256, 1024
    n_f, n_o = D_EXPERT // tf, D_MODEL // tn
    f_idx = lambda s: jnp.minimum(s, n_f - 1)
    o_idx = lambda s: jnp.maximum(s - n_f, 0)
    x_idx = lambda e, s: jnp.minimum(e + jnp.where(s >= n_f, 1, 0), N_EXPERTS - 1)
    return pl.pallas_call(
        functools.partial(_expert_kernel, n_f),
        out_shape=jax.ShapeDtypeStruct((N_EXPERTS, rows, D_MODEL), BF16),
        grid=(N_EXPERTS, n_f + n_o),
        in_specs=[
            pl.BlockSpec((None, rows, D_MODEL), lambda e, s: (x_idx(e, s), 0, 0)),
            pl.BlockSpec((None, None, D_MODEL, tf), lambda e, s: (layer, e, 0, f_idx(s))),
            pl.BlockSpec((None, None, D_MODEL, tf), lambda e, s: (layer, e, 0, f_idx(s))),
            pl.BlockSpec((None, None, D_EXPERT, tn), lambda e, s: (layer, e, 0, o_idx(s))),
            pl.BlockSpec((None, rows, 1), lambda e, s: (e, 0, 0)),
        ],
        out_specs=pl.BlockSpec((None, rows, tn), lambda e, s: (e, 0, o_idx(s))),
        scratch_shapes=[pltpu.VMEM((n_f, rows, tf), BF16)],
        compiler_params=pltpu.CompilerParams(dimension_semantics=("parallel", "arbitrary"),
                                             vmem_limit_bytes=EXPERT_VMEM_LIMIT),
        name="experts",
    )(xin, w_gate, w_up, w_down, gates)


def _scattered_rows(pt_ref, y_ref, x_ref, g2_ref):
    y = y_ref[...].reshape(N_EXPERTS * SLOTS, D_MODEL)
    return x_ref[...] + g2_ref[...] * _dot(pt_ref[...], y)


def _scatter_kernel(pt_ref, y_ref, x_ref, g2_ref, ng_ref, sc_ref, sh_ref, xo_ref, u_ref):
    xn = _scattered_rows(pt_ref, y_ref, x_ref, g2_ref)
    xo_ref[...] = xn
    u_ref[...] = (_rms(xn, ng_ref[...]) * (1.0 + sc_ref[...]) + sh_ref[...]).astype(BF16)


def _scatter_final_kernel(pt_ref, y_ref, x_ref, g2_ref, ng_ref, o_ref):
    o_ref[...] = _rms(_scattered_rows(pt_ref, y_ref, x_ref, g2_ref), ng_ref[...])


def _scatter(pt, y, x, mod, n_pt, next_norm=None, final_g=None):
    m = x.shape[0]
    tm = ROW_TILE
    row = _mod_row(tm, n_pt)

    def in_specs(first):
        return [
            pl.BlockSpec((tm, N_EXPERTS * SLOTS), lambda i: (i + first, 0)),
            pl.BlockSpec((N_EXPERTS, SLOTS, D_MODEL), lambda i: (0, (i + first) // (TILE // tm), 0)),
            pl.BlockSpec((tm, D_MODEL), lambda i: (i + first, 0)),
            pl.BlockSpec((None, None, 1, D_MODEL), lambda i: (row(i + first), 5, 0, 0)),
        ]

    rows = pl.BlockSpec((tm, D_MODEL), lambda i: (i, 0))
    if next_norm is None:
        def part(first, steps):
            return pl.pallas_call(
                _scatter_final_kernel,
                out_shape=jax.ShapeDtypeStruct((steps * tm, D_MODEL), F32),
                grid=(steps,),
                in_specs=in_specs(first) + [pl.BlockSpec((1, D_MODEL), lambda i: (0, 0))],
                out_specs=rows,
                compiler_params=_cparams(("parallel",)),
                name="scatter_final",
            )(pt, y, x, mod, final_g.reshape(1, D_MODEL))

        p_steps = n_pt * (TILE // tm)
        return part(0, p_steps), part(p_steps, m // tm - p_steps)
    norm_g, next_mod, next_layer = next_norm
    return pl.pallas_call(
        _scatter_kernel,
        out_shape=(jax.ShapeDtypeStruct((m, D_MODEL), F32), jax.ShapeDtypeStruct((m, D_MODEL), BF16)),
        grid=(m // tm,),
        in_specs=in_specs(0) + [pl.BlockSpec((None, 1, D_MODEL), lambda i: (next_layer, 0, 0)),
                                *_mod_specs(row, 0)],
        out_specs=(rows, rows),
        compiler_params=_cparams(("parallel",)),
        name="scatter",
    )(pt, y, x, mod, norm_g, next_mod, next_mod)


def kernel(x_prompt, x_sample, cache_k, cache_v, state_rglru, state_hgrn, c, c_ctx, w_ada, b_ada, norm1_g, norm2_g, w_in, conv_w, conv_b, rg_wr, rg_br, rg_wi, rg_bi, rg_lambda, hgrn_lb_logits, hgrn_norm_g, q_norm_g, k_norm_g, w_branch_a, w_branch_b, w_branch_c, w_out, w_router, w_exp_gate, w_exp_up, w_exp_down, final_norm_g):
    bp, seq, _ = x_prompt.shape
    bs, dec_seq, _ = x_sample.shape
    depth = w_in.shape[0]
    past = cache_k.shape[2]
    m = bp * seq + bs * dec_seq
    assert dec_seq == TILE and TILE % seq == 0 and (bp * seq) % TILE == 0 and seq & (seq - 1) == 0
    assert seq % (Q_BLK * Q_PAR) == 0 and past == seq and m % BIG_TILE == 0 and seq == HGRN_ROWS
    n_pt = bp * seq // TILE

    n_cond = -(-(1 + bs) // 8) * 8
    cond = jnp.concatenate([c_ctx[None], c, jnp.zeros((n_cond - 1 - bs, D_MODEL), F32)], axis=0)
    mod_all = _ada_mod(cond, w_ada, b_ada).reshape(depth, n_cond, N_SUB, 1, D_MODEL)
    rope_tabs = _rope_tables(dec_seq)
    mmats = jnp.asarray(_hgrn_matrices(), BF16)
    tri = jnp.asarray(_prefix_matrices(seq), BF16)
    w_router_pad = jnp.pad(w_router, ((0, 0), (0, 0), (0, E_PAD - N_EXPERTS)))
    ck = cache_k.reshape(bs, depth, past, KV_DIM)
    cv = cache_v.reshape(bs, depth, past, KV_DIM)
    n1 = norm1_g.reshape(depth, 1, D_MODEL)
    n2 = norm2_g.reshape(depth, 1, D_MODEL)
    w_out_bf16 = w_out.astype(BF16)

    k_list, v_list, ra_list, hb_list = [], [], [], []
    x, u1 = _norm_mod(x_prompt.reshape(bp * seq, D_MODEL), x_sample.reshape(bs * dec_seq, D_MODEL),
                      n1, mod_all[0], 0, n_pt)
    for l in range(depth):
        mod = mod_all[l]
        y_in, kv = _in_proj(u1, w_in, l)
        y_a, fin_a = _mixer_a(y_in, conv_w, conv_b, rg_wr, rg_br, rg_wi, rg_bi, rg_lambda, state_rglru, l, n_pt, seq)
        y_b, fin_b = _mixer_b(y_in, hgrn_lb_logits, hgrn_norm_g, state_hgrn, mmats, l, n_pt, seq)
        y_c, k_n = _attention(y_in, kv, ck, cv, rope_tabs, q_norm_g, k_norm_g, l, n_pt, seq)
        merged = _merge(y_a, y_b, y_c, y_in, w_branch_a, w_branch_b, w_branch_c, l)
        x, u2, logits = _out_proj(merged, w_out_bf16, x, mod, n2, w_router_pad, l, n_pt)
        xin, pt, gates = _route(logits, u2, tri, n_pt, seq)
        y_e = _experts(xin, gates, w_exp_gate, w_exp_up, w_exp_down, l)
        if l + 1 < depth:
            x, u1 = _scatter(pt, y_e, x, mod, n_pt, next_norm=(n1, mod_all[l + 1], l + 1))
        else:
            y_prompt, y_sample = _scatter(pt, y_e, x, mod, n_pt, final_g=final_norm_g)
        k_list.append(k_n[:bp * seq].reshape(bp, seq, KV_HEADS, HEAD))
        v_list.append(kv[:bp * seq, KV_DIM:].reshape(bp, seq, KV_HEADS, HEAD))
        ra_list.append(fin_a[:n_pt].reshape(bp, 2, D_MIX))
        hb_list.append(fin_b[:n_pt].reshape(bp, 2, N_HEADS, HEAD, HEAD))

    y_prompt = y_prompt.reshape(bp, seq, D_MODEL)
    y_sample = y_sample.reshape(bs, dec_seq, D_MODEL)
    return (y_prompt, y_sample, jnp.stack(k_list, axis=1), jnp.stack(v_list, axis=1),
            jnp.stack(ra_list, axis=1), jnp.stack(hb_list, axis=1))
```

```python
import functools

import numpy as np
import jax
import jax.numpy as jnp
from jax import lax
from jax.experimental import pallas as pl
from jax.experimental.pallas import tpu as pltpu

F32 = jnp.float32
BF16 = jnp.bfloat16

D_MODEL = 2048
D_MIX = D_MODEL // 2
HEAD = 128
N_HEADS = D_MIX // HEAD
CONV_W = 4
RG_C = 8.0
CHUNK = 64
N_LEVELS = 6
KV_HEADS = 2
KV_DIM = KV_HEADS * HEAD
GROUP = N_HEADS // KV_HEADS
ROPE_FREQS = HEAD // 4
ROPE_BASE = 10000.0
GRID_W = 64
N_EXPERTS = 16
EC_FACTOR = 2
D_EXPERT = D_MODEL // 2
N_SUB = 6
EPS = 1e-6
TILE = 1024
BIG_TILE = 2 * TILE
SLOTS = TILE * EC_FACTOR // N_EXPERTS
E_PAD = 128
Q_BLK = 128
Q_PAR = 2
HGRN_GROUP = 4
HGRN_ROWS = HGRN_GROUP * CHUNK
F32_BITS_MAX_FINITE = 0x7F7FFFFF
LOG2_E = float(np.log2(np.e))
ROW_TILE = 256
VMEM_LIMIT = 48 * 1024 * 1024
EXPERT_VMEM_LIMIT = 56 * 1024 * 1024

OFF_XA, OFF_YA, OFF_QB, OFF_FZF, OFF_FZB, OFF_IB, OFF_GB, OFF_QC = [i * D_MIX for i in range(8)]
OFF_KC = 8 * D_MIX
OFF_VC = OFF_KC + KV_DIM
OFF_GA = OFF_VC + KV_DIM
OFF_GB2 = OFF_GA + D_MODEL
OFF_GC = OFF_GB2 + D_MODEL
N_IN = OFF_GC + D_MODEL
IN_TN = 2 * KV_DIM


def _cparams(sem):
    return pltpu.CompilerParams(dimension_semantics=sem, vmem_limit_bytes=VMEM_LIMIT)


def _sig(x):
    return 0.5 * jnp.tanh(0.5 * x) + 0.5


def _silu(x):
    return x * _sig(x)


def _dot(a, b):
    return jnp.dot(a, b, preferred_element_type=F32)


def _dot_nt(a, b):
    return lax.dot_general(a, b, (((1,), (1,)), ((), ())), preferred_element_type=F32)


def _rms(x, g):
    ms = jnp.mean(x * x, axis=-1, keepdims=True)
    return x * lax.rsqrt(ms + EPS) * g


def _mod_row(tile_rows, n_pt):
    per = TILE // tile_rows if tile_rows <= TILE else None
    if per is None:
        mult = tile_rows // TILE
        return lambda i: jnp.maximum(i * mult - n_pt + 1, 0)
    return lambda i: jnp.maximum(i // per - n_pt + 1, 0)


def _ada_kernel(c_ref, w_ref, b_ref, o_ref):
    s = _silu(c_ref[...]).astype(BF16)
    o_ref[...] = _dot(s, w_ref[...].astype(BF16)) + b_ref[...]


def _ada_mod(cond, w_ada, b_ada):
    depth, _, n = w_ada.shape
    rows = cond.shape[0]
    tn = 1024
    return pl.pallas_call(
        _ada_kernel,
        out_shape=jax.ShapeDtypeStruct((depth, rows, n), F32),
        grid=(depth, n // tn),
        in_specs=[
            pl.BlockSpec((rows, D_MODEL), lambda l, j: (0, 0)),
            pl.BlockSpec((None, D_MODEL, tn), lambda l, j: (l, 0, j)),
            pl.BlockSpec((None, 1, tn), lambda l, j: (l, 0, j)),
        ],
        out_specs=pl.BlockSpec((None, rows, tn), lambda l, j: (l, 0, j)),
        compiler_params=_cparams(("parallel", "parallel")),
        name="ada_mod",
    )(cond, w_ada, b_ada.reshape(depth, 1, n))


def _norm_mod_kernel(p_steps, xp_ref, xs_ref, g_ref, sc_ref, sh_ref, x_ref, u_ref):
    i = pl.program_id(0)

    def emit(src_ref):
        x = src_ref[...]
        x_ref[...] = x
        u_ref[...] = (_rms(x, g_ref[...]) * (1.0 + sc_ref[...]) + sh_ref[...]).astype(BF16)

    @pl.when(i < p_steps)
    def _():
        emit(xp_ref)

    @pl.when(i >= p_steps)
    def _():
        emit(xs_ref)


def _mod_specs(mod_row, sub):
    mspec = lambda c: pl.BlockSpec((None, None, 1, D_MODEL), lambda i, *_: (mod_row(i), c, 0, 0))
    return mspec(3 * sub + 1), mspec(3 * sub)


def _norm_mod(x_prompt, x_sample, norm_g, mod, layer, n_pt):
    tm = 512
    p_steps = x_prompt.shape[0] // tm
    m = x_prompt.shape[0] + x_sample.shape[0]
    rows = pl.BlockSpec((tm, D_MODEL), lambda i: (i, 0))
    return pl.pallas_call(
        functools.partial(_norm_mod_kernel, p_steps),
        out_shape=(jax.ShapeDtypeStruct((m, D_MODEL), F32), jax.ShapeDtypeStruct((m, D_MODEL), BF16)),
        grid=(m // tm,),
        in_specs=[pl.BlockSpec((tm, D_MODEL), lambda i: (jnp.minimum(i, p_steps - 1), 0)),
                  pl.BlockSpec((tm, D_MODEL), lambda i: (jnp.maximum(i - p_steps, 0), 0)),
                  pl.BlockSpec((None, 1, D_MODEL), lambda i: (layer, 0, 0)),
                  *_mod_specs(_mod_row(tm, n_pt), 0)],
        out_specs=(rows, rows),
        compiler_params=_cparams(("parallel",)), name="norm_mod",
    )(x_prompt, x_sample, norm_g, mod, mod)


def _in_proj_kernel(kv_tile, u_ref, w_ref, o_ref, kv_ref):
    acc = _dot(u_ref[...], w_ref[...].astype(BF16))
    o_ref[...] = acc.astype(BF16)

    @pl.when(pl.program_id(1) == kv_tile)
    def _():
        kv_ref[...] = acc


def _in_proj(u, w_in, layer):
    m = u.shape[0]
    assert OFF_KC % IN_TN == 0 and N_IN % IN_TN == 0
    return pl.pallas_call(
        functools.partial(_in_proj_kernel, OFF_KC // IN_TN),
        out_shape=(jax.ShapeDtypeStruct((m, N_IN), BF16), jax.ShapeDtypeStruct((m, IN_TN), F32)),
        grid=(m // BIG_TILE, N_IN // IN_TN),
        in_specs=[
            pl.BlockSpec((BIG_TILE, D_MODEL), lambda i, j: (i, 0)),
            pl.BlockSpec((None, D_MODEL, IN_TN), lambda i, j: (layer, 0, j)),
        ],
        out_specs=(pl.BlockSpec((BIG_TILE, IN_TN), lambda i, j: (i, j)),
                   pl.BlockSpec((BIG_TILE, IN_TN), lambda i, j: (i, 0))),
        compiler_params=_cparams(("parallel", "arbitrary")),
        name="in_proj",
    )(u, w_in)


def _softplus(z):
    return jnp.maximum(z, 0.0) + jnp.log1p(jnp.exp(-jnp.abs(z)))


def _gelu_tanh(x):
    c = float(np.sqrt(2.0 / np.pi))
    return (0.5 * x) * (1.0 + jnp.tanh(x * (c + (c * 0.044715) * (x * x))))


def _mixa_kernel(n_pt, seq, xa_ref, ya_ref, cw_ref, cb_ref, wr_ref, br_ref, wi_ref, bi_ref, lam_ref, h0_ref,
                 y_ref, fin_ref, af_ref, bf_ref, ab_ref, bb_ref, hf_ref, hb_ref, xc_ref):
    i = pl.program_id(0)

    def body(seg_len, use_state):
        x = xa_ref[...].astype(F32)
        cw = cw_ref[...]
        taps = (pltpu.roll(x, 1, 0), x, pltpu.roll(x, TILE - 1, 0), pltpu.roll(x, TILE - 2, 0))
        conv = lambda t: t[0] * cw[0:1] + t[1] * cw[1:2] + t[2] * cw[2:3] + t[3] * cw[3:4] + cb_ref[...]
        xc_ref[...] = conv(taps)
        row8 = lax.broadcasted_iota(jnp.int32, (8, HEAD), 0)
        for s in range(TILE // seg_len):
            lo, hi = s * seg_len, (s + 1) * seg_len - 8
            first = [t[lo:lo + 8] for t in taps]
            first[0] = jnp.where(row8 >= 1, first[0], 0.0)
            xc_ref[lo:lo + 8, :] = conv(first)
            last = [t[hi:hi + 8] for t in taps]
            last[2] = jnp.where(row8 < 7, last[2], 0.0)
            last[3] = jnp.where(row8 < 6, last[3], 0.0)
            xc_ref[hi:hi + 8, :] = conv(last)
        xc = xc_ref[...]
        xcb = xc.astype(BF16)
        half_xc = 0.5 * xc
        for d, (a_ref, b_ref) in enumerate(((af_ref, bf_ref), (ab_ref, bb_ref))):
            t_r = jnp.tanh(_dot(xcb, (0.5 * wr_ref[d]).astype(BF16)) + 0.5 * br_ref[d])
            t_i = jnp.tanh(_dot(xcb, (0.5 * wi_ref[d]).astype(BF16)) + 0.5 * bi_ref[d])
            c4 = (0.5 * RG_C) * _softplus(-lam_ref[d])
            nla = c4 * t_r + c4
            a = jnp.exp2(nla * (-LOG2_E))
            a_ref[...] = a
            b_ref[...] = jnp.sqrt(jnp.tanh(nla) * (a * a + 1.0)) * ((t_i + 1.0) * half_xc)

        nb = seg_len // 8
        rowi = lax.broadcasted_iota(jnp.int32, (8, HEAD), 0)
        n_seg = TILE // seg_len
        for s in range(n_seg):
            base = s * seg_len
            if use_state:
                c0 = (h0_ref[0:1, :], h0_ref[1:2, :])
            else:
                c0 = (jnp.zeros((1, HEAD), F32), jnp.zeros((1, HEAD), F32))

            def step(j, carry):
                cf, cb = carry
                of = pl.multiple_of(base + j * 8, 8)
                ob = pl.multiple_of(base + (nb - 1 - j) * 8, 8)
                a = af_ref[pl.ds(of, 8), :]
                b = bf_ref[pl.ds(of, 8), :]
                for d in (1, 2, 4):
                    m = rowi >= d
                    b = jnp.where(m, a * pltpu.roll(b, d, 0) + b, b)
                    a = jnp.where(m, a * pltpu.roll(a, d, 0), a)
                h = b + a * cf
                hf_ref[pl.ds(of, 8), :] = h
                cf = h[7:8, :]
                a = ab_ref[pl.ds(ob, 8), :]
                b = bb_ref[pl.ds(ob, 8), :]
                for d in (1, 2, 4):
                    m = rowi < 8 - d
                    b = jnp.where(m, a * pltpu.roll(b, 8 - d, 0) + b, b)
                    a = jnp.where(m, a * pltpu.roll(a, 8 - d, 0), a)
                h = b + a * cb
                hb_ref[pl.ds(ob, 8), :] = h
                cb = h[0:1, :]
                return cf, cb

            cf, cb = lax.fori_loop(0, nb, step, c0, unroll=4)
            fin_ref[s, 0:1, :] = cf
            fin_ref[s, 1:2, :] = cb
        for s in range(n_seg, TILE // seq):
            fin_ref[s] = jnp.zeros((2, HEAD), F32)
        y_ref[...] = ((hf_ref[...] + hb_ref[...]) * _gelu_tanh(ya_ref[...].astype(F32))).astype(BF16)

    @pl.when(i < n_pt)
    def _():
        body(seq, False)

    @pl.when(i >= n_pt)
    def _():
        body(TILE, True)


def _mixer_a(y_in, conv_w, conv_b, rg_wr, rg_br, rg_wi, rg_bi, rg_lambda, state, layer, n_pt, seq):
    m = y_in.shape[0]
    depth = conv_w.shape[0]
    nt = m // TILE
    spt = TILE // seq
    req = lambda i: jnp.maximum(i - n_pt, 0)
    vec = lambda a: a.reshape(depth, 2, 1, D_MIX)
    gate_w = pl.BlockSpec((None, 2, None, HEAD, HEAD), lambda i, h: (layer, 0, h, 0, 0))
    gate_b = pl.BlockSpec((None, 2, 1, HEAD), lambda i, h: (layer, 0, 0, h))
    slab = lambda off: pl.BlockSpec((TILE, HEAD), lambda i, h: (i, off // HEAD + h))
    return pl.pallas_call(
        functools.partial(_mixa_kernel, n_pt, seq),
        out_shape=(jax.ShapeDtypeStruct((m, D_MIX), BF16),
                   jax.ShapeDtypeStruct((nt, spt, 2, D_MIX), F32)),
        grid=(nt, N_HEADS),
        in_specs=[
            slab(OFF_XA), slab(OFF_YA),
            pl.BlockSpec((None, CONV_W, HEAD), lambda i, h: (layer, 0, h)),
            pl.BlockSpec((None, 1, HEAD), lambda i, h: (layer, 0, h)),
            gate_w, gate_b, gate_w, gate_b, gate_b,
            pl.BlockSpec((None, None, 2, HEAD), lambda i, h: (req(i), layer, 0, h)),
        ],
        out_specs=(pl.BlockSpec((TILE, HEAD), lambda i, h: (i, h)),
                   pl.BlockSpec((None, spt, 2, HEAD), lambda i, h: (i, 0, 0, h))),
        scratch_shapes=[pltpu.VMEM((TILE, HEAD), F32)] * 7,
        compiler_params=_cparams(("parallel", "parallel")),
        name="mixer_a",
    )(y_in, y_in, conv_w, conv_b.reshape(depth, 1, D_MIX), rg_wr, vec(rg_br), rg_wi, vec(rg_bi), vec(rg_lambda),
      state)


def _hgrn_matrices():
    t = np.arange(CHUNK)
    return np.stack([t[None, :] <= t[:, None], t[None, :] >= t[:, None]]).astype(np.float32)


def _hgrn_decays(g, tri, reverse):
    c = CHUNK
    g_cat = jnp.concatenate([g[n * c:(n + 1) * c] for n in range(HGRN_GROUP)], axis=1)
    gh = g_cat.astype(BF16)
    gl = (g_cat - gh.astype(F32)).astype(BF16)
    cum = _dot(tri, gh) + _dot(tri, gl)
    end = cum[c - 1:c] if not reverse else cum[0:1]
    out = [cum, end - cum]
    rowi = lax.broadcasted_iota(jnp.int32, g_cat.shape, 0)
    m = c // 2
    while m >= 4:
        mids = [blk * 2 * m + (m if reverse else m - 1) for blk in range(c // (2 * m))]
        mid = jnp.concatenate([jnp.broadcast_to(cum[r:r + 1], (2 * m, g_cat.shape[1])) for r in mids], axis=0)
        out.append(pltpu.bitcast(pltpu.bitcast(cum - mid, jnp.uint32) | jnp.uint32(0x80000000), F32))
        m //= 2
    g_next = pltpu.roll(g_cat, c - 1, 0)
    g_prev = pltpu.roll(g_cat, 1, 0)
    p = rowi & 3
    if not reverse:
        out.append(jnp.where(p == 0, g_next, jnp.where(p == 1, 0.0, jnp.where(p == 2, g_cat, g_cat + g_prev))))
        out.append(jnp.where((rowi & 1) != 0, g_cat, 0.0))
    else:
        out.append(jnp.where(p == 0, g_cat + g_next, jnp.where(p == 1, g_cat, jnp.where(p == 2, 0.0, g_prev))))
        out.append(jnp.where((rowi & 1) != 0, 0.0, g_cat))
    return out


def _hgrn_side_masks(reverse):
    rowi = lax.broadcasted_iota(jnp.int32, (CHUNK, HEAD), 0)
    masks = []
    for lvl in range(N_LEVELS):
        q_side = ((rowi & (CHUNK >> (lvl + 1))) != 0) != reverse
        on = jnp.where(q_side, 1.0, 0.0)
        masks.append((on.astype(BF16), (1.0 - on).astype(BF16)))
    return masks


def _hgrn_pair_masks():
    ti = lax.broadcasted_iota(jnp.int32, (CHUNK, CHUNK), 0)
    si = lax.broadcasted_iota(jnp.int32, (CHUNK, CHUNK), 1)
    return ti == si, [(ti ^ si) < (CHUNK >> lvl) for lvl in range(N_LEVELS)]


def _hgrn_chunk_local(q, k, v, decays, masks, pair_masks, reverse):
    c = CHUNK
    e_in = jnp.exp2(decays[0])
    e_out = jnp.exp2(decays[1])
    vb = v.astype(BF16)
    qb = q.astype(BF16)
    kb = k.astype(BF16)
    diag, same_block = pair_masks
    scores = None
    for lvl in range(N_LEVELS):
        eb = jnp.exp2(decays[2 + lvl]).astype(BF16)
        mq, mk = masks[lvl]
        part = _dot_nt(qb * eb * mq, kb * eb * mk)
        scores = part if scores is None else jnp.where(same_block[lvl], part, scores)
    scores = jnp.where(diag, _dot_nt(qb, kb), scores)
    ke = (k * e_out).astype(BF16)
    own = lax.dot_general(vb, ke, (((0,), (0,)), ((), ())), preferred_element_type=F32)
    d_end = e_in[c - 1:c] if not reverse else e_in[0:1]
    return (q * e_in).astype(BF16), scores.astype(BF16), vb, own, d_end


def _mixb_kernel(n_pt, seq, layer, qb_ref, fzf_ref, fzb_ref, ib_ref, gb_ref, lbl_ref, s0_ref, ng_ref, mm_ref,
                 y_ref, sfin_ref, of_ref, ob_ref):
    i = pl.program_id(0)
    depth = lbl_ref.shape[0]

    def lower_bound(d):
        lg = [lbl_ref[j, d] for j in range(depth)]
        mx = functools.reduce(jnp.maximum, lg)
        ex = [jnp.exp(v - mx) for v in lg]
        tot = functools.reduce(lambda a, b: a + b, ex)
        acc = jnp.zeros_like(tot)
        for j in range(1, layer + 1):
            acc = acc + ex[j] / tot
        return acc

    def body(seg_len, use_state):
        lbs = (lower_bound(0), lower_bound(1))
        n_grp = seg_len // HGRN_ROWS
        chunk_rows = lambda a, n: a[n * CHUNK:(n + 1) * CHUNK]

        def load(r0, d):
            rows = pl.ds(r0, HGRN_ROWS)
            q = _silu(qb_ref[rows, :].astype(F32))
            v = ib_ref[rows, :].astype(F32)
            fz = (fzf_ref if d == 0 else fzb_ref)[rows, :].astype(F32)
            f = lbs[d] + (1.0 - lbs[d]) / (1.0 + jnp.exp(-fz))
            return q, 1.0 - f, v, jnp.log2(f)

        def group(r_f, r_b, st_f, st_b):
            qf, kf, vf, gf = load(r_f, 0)
            qb, kb, vb, gb = load(r_b, 1)
            pair_masks = _hgrn_pair_masks()
            masks_f = _hgrn_side_masks(False)
            masks_b = _hgrn_side_masks(True)
            ef = _hgrn_decays(gf, mm_ref[0], False)
            eb = _hgrn_decays(gb, mm_ref[1], True)
            lanes = lambda e, n: [a[:, n * HEAD:(n + 1) * HEAD] for a in e]
            loc_f = [_hgrn_chunk_local(chunk_rows(qf, n), chunk_rows(kf, n), chunk_rows(vf, n),
                                       lanes(ef, n), masks_f, pair_masks, False) for n in range(HGRN_GROUP)]
            loc_b = [_hgrn_chunk_local(chunk_rows(qb, n), chunk_rows(kb, n), chunk_rows(vb, n),
                                       lanes(eb, n), masks_b, pair_masks, True) for n in range(HGRN_GROUP)]
            for n in range(HGRN_GROUP):
                qe, sc, vals, own, d_end = loc_f[n]
                of_ref[pl.ds(r_f + n * CHUNK, CHUNK), :] = _dot_nt(qe, st_f.astype(BF16)) + _dot(sc, vals)
                st_f = st_f * d_end + own
            for n in reversed(range(HGRN_GROUP)):
                qe, sc, vals, own, d_end = loc_b[n]
                ob_ref[pl.ds(r_b + n * CHUNK, CHUNK), :] = _dot_nt(qe, st_b.astype(BF16)) + _dot(sc, vals)
                st_b = st_b * d_end + own
            return st_f, st_b

        if use_state:
            def step(j, carry):
                r_f = pl.multiple_of(j * HGRN_ROWS, HGRN_ROWS)
                r_b = pl.multiple_of((n_grp - 1 - j) * HGRN_ROWS, HGRN_ROWS)
                return group(r_f, r_b, *carry)

            st_f, st_b = lax.fori_loop(0, n_grp, step, (s0_ref[0].T, s0_ref[1].T))
            sfin_ref[0, 0] = st_f.T
            sfin_ref[0, 1] = st_b.T
            for s in range(1, TILE // seq):
                sfin_ref[s] = jnp.zeros((2, HEAD, HEAD), F32)
        else:
            def request(s, carry):
                r0 = pl.multiple_of(s * HGRN_ROWS, HGRN_ROWS)
                zero = jnp.zeros((HEAD, HEAD), F32)
                st_f, st_b = group(r0, r0, zero, zero)
                sfin_ref[s, 0] = st_f.T
                sfin_ref[s, 1] = st_b.T
                return carry

            lax.fori_loop(0, TILE // seg_len, request, 0)
        o = _rms(of_ref[...] + ob_ref[...], ng_ref[...]) * _silu(gb_ref[...].astype(F32))
        y_ref[...] = o.astype(BF16)

    @pl.when(i < n_pt)
    def _():
        body(seq, False)

    @pl.when(i >= n_pt)
    def _():
        body(TILE, True)


def _mixer_b(y_in, lb_logits, norm_g, state, mmats, layer, n_pt, seq):
    m = y_in.shape[0]
    depth = lb_logits.shape[0]
    nt = m // TILE
    spt = TILE // seq
    req = lambda i: jnp.maximum(i - n_pt, 0)
    slab = lambda off: pl.BlockSpec((TILE, HEAD), lambda i, h: (i, off // HEAD + h))
    return pl.pallas_call(
        functools.partial(_mixb_kernel, n_pt, seq, layer),
        out_shape=(jax.ShapeDtypeStruct((m, D_MIX), BF16),
                   jax.ShapeDtypeStruct((nt, spt, 2, N_HEADS, HEAD, HEAD), F32)),
        grid=(nt, N_HEADS),
        in_specs=[
            slab(OFF_QB), slab(OFF_FZF), slab(OFF_FZB), slab(OFF_IB), slab(OFF_GB),
            pl.BlockSpec((depth, 2, 1, HEAD), lambda i, h: (0, 0, 0, h)),
            pl.BlockSpec((None, None, 2, None, HEAD, HEAD), lambda i, h: (req(i), layer, 0, h, 0, 0)),
            pl.BlockSpec((None, 1, HEAD), lambda i, h: (layer, 0, 0)),
            pl.BlockSpec(mmats.shape, lambda i, h: (0, 0, 0)),
        ],
        out_specs=(pl.BlockSpec((TILE, HEAD), lambda i, h: (i, h)),
                   pl.BlockSpec((None, spt, 2, None, HEAD, HEAD), lambda i, h: (i, 0, 0, h, 0, 0))),
        scratch_shapes=[pltpu.VMEM((TILE, HEAD), F32)] * 2,
        compiler_params=_cparams(("parallel", "parallel")),
        name="mixer_b",
    )(y_in, y_in, y_in, y_in, y_in, lb_logits.reshape(depth, 2, 1, D_MIX), state,
      norm_g.reshape(depth, 1, HEAD), mmats)


def _rope(x, cos, sin_lo, sin_hi):
    return x * cos + pltpu.roll(x, HEAD - ROPE_FREQS, 1) * sin_lo + pltpu.roll(x, ROPE_FREQS, 1) * sin_hi


def _attn_kernel(n_pt, seq, q_ref, k_ref, v_ref, ck_ref, cv_ref, cos_ref, slo_ref, shi_ref, qg_ref, kg_ref,
                 y_ref, kn_ref, qs_ref, ks_ref, vs_ref):
    i = pl.program_id(0)
    scale = HEAD ** -0.5

    def body(seg_len, ctx):
        kn = _rms(k_ref[...], kg_ref[...])
        kn_ref[...] = kn
        if ctx:
            kn = _rope(kn, cos_ref[...], slo_ref[...], shi_ref[...])
        ks_ref[...] = kn.astype(BF16)
        vs_ref[...] = v_ref[...].astype(BF16)
        for g in range(GROUP):
            qn = _rms(q_ref[:, g * HEAD:(g + 1) * HEAD].astype(F32), qg_ref[...])
            if ctx:
                qn = _rope(qn, cos_ref[...], slo_ref[...], shi_ref[...])
            qs_ref[g] = (qn * scale).astype(BF16)
        if ctx:
            ckb = ck_ref[...].astype(BF16)
            cvb = cv_ref[...].astype(BF16)
        for s in range(TILE // seg_len):
            base = s * seg_len

            def qblocks(j, carry):
                keys = ks_ref[pl.ds(base, seg_len), :]
                vals = vs_ref[pl.ds(base, seg_len), :]
                rows = [pl.ds(pl.multiple_of(base + (j * Q_PAR + t) * Q_BLK, Q_BLK), Q_BLK) for t in range(Q_PAR)]
                qcat = [jnp.concatenate([qs_ref[g, r, :] for g in range(GROUP)], axis=0) for r in rows]
                sc = [_dot_nt(qc, keys) for qc in qcat]
                sc_c = [_dot_nt(qc, ckb) for qc in qcat] if ctx else None
                for t in range(Q_PAR):
                    mx = jnp.max(sc[t], axis=-1, keepdims=True)
                    if ctx:
                        mx = jnp.maximum(mx, jnp.max(sc_c[t], axis=-1, keepdims=True))
                    p = jnp.exp(sc[t] - mx)
                    den = jnp.sum(p, axis=-1, keepdims=True)
                    o = _dot(p.astype(BF16), vals)
                    if ctx:
                        pc = jnp.exp(sc_c[t] - mx)
                        den = den + jnp.sum(pc, axis=-1, keepdims=True)
                        o = o + _dot(pc.astype(BF16), cvb)
                    o = o / den
                    for g in range(GROUP):
                        y_ref[rows[t], g * HEAD:(g + 1) * HEAD] = o[g * Q_BLK:(g + 1) * Q_BLK].astype(BF16)
                return carry

            lax.fori_loop(0, seg_len // (Q_BLK * Q_PAR), qblocks, 0)

    @pl.when(i < n_pt)
    def _():
        body(seq, False)

    @pl.when(i >= n_pt)
    def _():
        body(TILE, True)


def _attention(y_in, kv, cache_k, cache_v, rope_tabs, q_norm_g, k_norm_g, layer, n_pt, seq):
    m = y_in.shape[0]
    depth = q_norm_g.shape[0]
    nt = m // TILE
    gw = GROUP * HEAD
    past = cache_k.shape[2]
    req = lambda i: jnp.maximum(i - n_pt, 0)
    ctx_spec = pl.BlockSpec((None, None, past, HEAD), lambda i, h: (req(i), layer, 0, h))
    tab = pl.BlockSpec((TILE, HEAD), lambda i, h: (0, 0))
    gspec = pl.BlockSpec((None, 1, HEAD), lambda i, h: (layer, 0, 0))
    return pl.pallas_call(
        functools.partial(_attn_kernel, n_pt, seq),
        out_shape=(jax.ShapeDtypeStruct((m, D_MIX), BF16),
                   jax.ShapeDtypeStruct((m, KV_DIM), F32)),
        grid=(nt, KV_HEADS),
        in_specs=[
            pl.BlockSpec((TILE, gw), lambda i, h: (i, OFF_QC // gw + h)),
            pl.BlockSpec((TILE, HEAD), lambda i, h: (i, h)),
            pl.BlockSpec((TILE, HEAD), lambda i, h: (i, KV_HEADS + h)),
            ctx_spec, ctx_spec, tab, tab, tab, gspec, gspec,
        ],
        out_specs=(pl.BlockSpec((TILE, gw), lambda i, h: (i, h)),
                   pl.BlockSpec((TILE, HEAD), lambda i, h: (i, h))),
        scratch_shapes=[pltpu.VMEM((GROUP, TILE, HEAD), BF16), pltpu.VMEM((TILE, HEAD), BF16),
                        pltpu.VMEM((TILE, HEAD), BF16)],
        compiler_params=_cparams(("parallel", "parallel")),
        name="attention",
    )(y_in, kv, kv, cache_k, cache_v, *rope_tabs, q_norm_g.reshape(depth, 1, HEAD),
      k_norm_g.reshape(depth, 1, HEAD))


def _rope_tables(n):
    rows = n // GRID_W
    r = jnp.repeat(jnp.arange(rows, dtype=F32), GRID_W)
    col = jnp.tile(jnp.arange(GRID_W, dtype=F32), rows)
    inv = ROPE_BASE ** (-jnp.arange(ROPE_FREQS, dtype=F32) / ROPE_FREQS)
    ar = r[:, None] * inv
    ac = col[:, None] * inv
    ang = jnp.concatenate([ar, ar, ac, ac], axis=-1)
    cos, sin = jnp.cos(ang), jnp.sin(ang)
    first_half = (jnp.arange(HEAD) // ROPE_FREQS) % 2 == 0
    return cos, jnp.where(first_half, -sin, 0.0), jnp.where(first_half, 0.0, sin)


def _merge_kernel(ya_ref, yb_ref, yc_ref, wa_ref, wb_ref, wc_ref, ga_ref, gb_ref, gc_ref, o_ref):
    acc = _sig(ga_ref[...].astype(F32)) * _dot(ya_ref[...], wa_ref[...].astype(BF16))
    acc = acc + _sig(gb_ref[...].astype(F32)) * _dot(yb_ref[...], wb_ref[...].astype(BF16))
    acc = acc + _sig(gc_ref[...].astype(F32)) * _dot(yc_ref[...], wc_ref[...].astype(BF16))
    o_ref[...] = acc.astype(BF16)


def _merge(y_a, y_b, y_c, y_in, w_a, w_b, w_c, layer):
    m = y_in.shape[0]
    tn = 256
    act = pl.BlockSpec((BIG_TILE, D_MIX), lambda i, j: (i, 0))
    wsp = pl.BlockSpec((None, D_MIX, tn), lambda i, j: (layer, 0, j))
    gate = lambda off: pl.BlockSpec((BIG_TILE, tn), lambda i, j: (i, off // tn + j))
    return pl.pallas_call(
        _merge_kernel,
        out_shape=jax.ShapeDtypeStruct((m, D_MODEL), BF16),
        grid=(m // BIG_TILE, D_MODEL // tn),
        in_specs=[act, act, act, wsp, wsp, wsp, gate(OFF_GA), gate(OFF_GB2), gate(OFF_GC)],
        out_specs=pl.BlockSpec((BIG_TILE, tn), lambda i, j: (i, j)),
        compiler_params=_cparams(("parallel", "parallel")),
        name="merge",
    )(y_a, y_b, y_c, w_a, w_b, w_c, y_in, y_in, y_in)


def _out_proj_kernel(mg_ref, w_ref, x_ref, g1_ref, ng_ref, sc_ref, sh_ref, wr_ref, xo_ref, u_ref, lg_ref):
    xn = x_ref[...] + g1_ref[...] * _dot(mg_ref[...], w_ref[...])
    xo_ref[...] = xn
    u = _rms(xn, ng_ref[...]) * (1.0 + sc_ref[...]) + sh_ref[...]
    u_ref[...] = u.astype(BF16)
    w = wr_ref[...]
    uh = u.astype(BF16)
    ul = (u - uh.astype(F32)).astype(BF16)
    wh = w.astype(BF16)
    wl = (w - wh.astype(F32)).astype(BF16)
    lg_ref[...] = _dot(uh, wh) + (_dot(uh, wl) + _dot(ul, wh))


def _out_proj(merged, w_out_bf16, x, mod, norm_g, w_router_pad, layer, n_pt):
    m = x.shape[0]
    tm = ROW_TILE
    row = _mod_row(tm, n_pt)
    return pl.pallas_call(
        _out_proj_kernel,
        out_shape=(jax.ShapeDtypeStruct((m, D_MODEL), F32),
                   jax.ShapeDtypeStruct((m, D_MODEL), BF16),
                   jax.ShapeDtypeStruct((m, E_PAD), F32)),
        grid=(m // tm,),
        in_specs=[
            pl.BlockSpec((tm, D_MODEL), lambda i: (i, 0)),
            pl.BlockSpec((None, D_MODEL, D_MODEL), lambda i: (layer, 0, 0)),
            pl.BlockSpec((tm, D_MODEL), lambda i: (i, 0)),
            pl.BlockSpec((None, None, 1, D_MODEL), lambda i: (row(i), 2, 0, 0)),
            pl.BlockSpec((None, 1, D_MODEL), lambda i: (layer, 0, 0)),
            *_mod_specs(row, 1),
            pl.BlockSpec((None, D_MODEL, E_PAD), lambda i: (layer, 0, 0)),
        ],
        out_specs=(pl.BlockSpec((tm, D_MODEL), lambda i: (i, 0)),
                   pl.BlockSpec((tm, D_MODEL), lambda i: (i, 0)),
                   pl.BlockSpec((tm, E_PAD), lambda i: (i, 0))),
        compiler_params=_cparams(("parallel",)),
        name="out_proj",
    )(merged, w_out_bf16, x, mod, norm_g, mod, mod, w_router_pad)


def _prefix_matrices(seq):
    t = np.arange(TILE)
    before = t[:, None] < t[None, :]
    same = (t[:, None] // seq) == (t[None, :] // seq)
    return np.stack([before & same, before]).astype(np.float32)


def _route_kernel(n_pt, seq, lg_ref, u_ref, tri_ref, xin_ref, pt_ref, gate_ref, p_ref):
    i = pl.program_id(0)

    def body(seg_len):
        n_seg = TILE // seg_len
        cap = EC_FACTOR * seg_len // N_EXPERTS
        lt = lg_ref[...].T[0:N_EXPERTS]
        ex = jnp.exp(lt - jnp.max(lt, axis=0, keepdims=True))
        aff = ex / jnp.sum(ex, axis=0, keepdims=True)
        bits = pltpu.bitcast(aff, jnp.int32)
        segs = [bits[:, s * seg_len:(s + 1) * seg_len] for s in range(n_seg)]

        def count_ge(seg, thr):
            return jnp.sum(jnp.where(seg >= thr, 1.0, 0.0), axis=1, keepdims=True)

        def bisect(_, carry):
            out = []
            for seg, (lo, hi) in zip(segs, carry):
                mid = lo + ((hi - lo + 1) >> 1)
                ok = count_ge(seg, mid) >= cap
                out.append((jnp.where(ok, mid, lo), jnp.where(ok, hi, mid - 1)))
            return tuple(out)

        init = tuple((jnp.zeros((N_EXPERTS, 1), jnp.int32),
                      jnp.full((N_EXPERTS, 1), F32_BITS_MAX_FINITE, jnp.int32)) for _ in segs)
        bounds = lax.fori_loop(0, 31, bisect, init)
        gt_l, eq_l, need_l = [], [], []
        for seg, (thr, _) in zip(segs, bounds):
            gt = seg > thr
            gt_l.append(jnp.where(gt, 1.0, 0.0))
            eq_l.append(jnp.where(seg == thr, 1.0, 0.0))
            n_gt = jnp.sum(gt_l[-1], axis=1, keepdims=True)
            need_l.append(jnp.broadcast_to(cap - n_gt, (N_EXPERTS, seg_len)))
        gt = jnp.concatenate(gt_l, axis=1)
        eq = jnp.concatenate(eq_l, axis=1)
        need = jnp.concatenate(need_l, axis=1)
        before = _dot(jnp.concatenate([gt, eq], axis=0).astype(BF16), tri_ref[...])
        gt_before, eq_before = before[0:N_EXPERTS], before[N_EXPERTS:]
        chosen = jnp.logical_or(gt > 0.0, jnp.logical_and(eq > 0.0, eq_before < need))
        tok = lax.broadcasted_iota(jnp.int32, (N_EXPERTS, TILE), 1)
        seg_off = ((tok // seg_len) * cap).astype(F32)
        slot = jnp.where(chosen, gt_before + jnp.minimum(eq_before, need) + seg_off, -1.0)
        slot_t = jnp.concatenate([slot, jnp.full((E_PAD - N_EXPERTS, TILE), -1.0, F32)], axis=0).T
        sub = lax.broadcasted_iota(jnp.int32, (SLOTS, TILE), 0).astype(F32)
        lane = lax.broadcasted_iota(jnp.int32, (TILE, SLOTS), 1).astype(F32)
        for e in range(N_EXPERTS):
            hit = slot[e:e + 1] == sub
            gate_ref[e] = jnp.sum(jnp.where(hit, aff[e:e + 1], 0.0), axis=1, keepdims=True)
            p_ref[e * SLOTS:(e + 1) * SLOTS, :] = jnp.where(hit, 1.0, 0.0).astype(BF16)
            hit_t = slot_t[:, e:e + 1] == lane
            pt_ref[:, e * SLOTS:(e + 1) * SLOTS] = jnp.where(hit_t, 1.0, 0.0).astype(BF16)
        tn = 512
        for n in range(D_MODEL // tn):
            rows = _dot(p_ref[...], u_ref[:, n * tn:(n + 1) * tn])
            xin_ref[:, :, n * tn:(n + 1) * tn] = rows.astype(BF16).reshape(N_EXPERTS, SLOTS, tn)

    @pl.when(i < n_pt)
    def _():
        body(seq)

    @pl.when(i >= n_pt)
    def _():
        body(TILE)


def _route(logits, u, tri, n_pt, seq):
    m = u.shape[0]
    nt = m // TILE
    return pl.pallas_call(
        functools.partial(_route_kernel, n_pt, seq),
        out_shape=(jax.ShapeDtypeStruct((N_EXPERTS, nt * SLOTS, D_MODEL), BF16),
                   jax.ShapeDtypeStruct((m, N_EXPERTS * SLOTS), BF16),
                   jax.ShapeDtypeStruct((N_EXPERTS, nt * SLOTS, 1), F32)),
        grid=(nt,),
        in_specs=[pl.BlockSpec((TILE, E_PAD), lambda i: (i, 0)),
                  pl.BlockSpec((TILE, D_MODEL), lambda i: (i, 0)),
                  pl.BlockSpec((None, TILE, TILE), lambda i: (jnp.where(i < n_pt, 0, 1), 0, 0))],
        out_specs=(pl.BlockSpec((N_EXPERTS, SLOTS, D_MODEL), lambda i: (0, i, 0)),
                   pl.BlockSpec((TILE, N_EXPERTS * SLOTS), lambda i: (i, 0)),
                   pl.BlockSpec((N_EXPERTS, SLOTS, 1), lambda i: (0, i, 0))),
        scratch_shapes=[pltpu.VMEM((N_EXPERTS * SLOTS, TILE), BF16)],
        compiler_params=_cparams(("parallel",)),
        name="route",
    )(logits, u, tri)


def _expert_kernel(n_f, x_ref, wg_ref, wu_ref, wd_ref, gate_ref, o_ref, h_ref):
    s = pl.program_id(1)
    tf = wg_ref.shape[-1]

    @pl.when(s < n_f)
    def _():
        x = x_ref[...]
        h = _silu(_dot(x, wg_ref[...].astype(BF16))) * _dot(x, wu_ref[...].astype(BF16))
        h_ref[s] = h.astype(BF16)

    @pl.when(s >= n_f)
    def _():
        acc = _dot(h_ref[0], wd_ref[0:tf, :].astype(BF16))
        for k in range(1, n_f):
            acc = acc + _dot(h_ref[k], wd_ref[k * tf:(k + 1) * tf, :].astype(BF16))
        o_ref[...] = (acc * gate_ref[...]).astype(BF16)


def _experts(xin, gates, w_gate, w_up, w_down, layer):
    _, rows, _ = xin.shape
    tf, tn = 256, 1024
    n_f, n_o = D_EXPERT // tf, D_MODEL // tn
    f_idx = lambda s: jnp.minimum(s, n_f - 1)
    o_idx = lambda s: jnp.maximum(s - n_f, 0)
    x_idx = lambda e, s: jnp.minimum(e + jnp.where(s >= n_f, 1, 0), N_EXPERTS - 1)
    return pl.pallas_call(
        functools.partial(_expert_kernel, n_f),
        out_shape=jax.ShapeDtypeStruct((N_EXPERTS, rows, D_MODEL), BF16),
        grid=(N_EXPERTS, n_f + n_o),
        in_specs=[
            pl.BlockSpec((None, rows, D_MODEL), lambda e, s: (x_idx(e, s), 0, 0)),
            pl.BlockSpec((None, None, D_MODEL, tf), lambda e, s: (layer, e, 0, f_idx(s))),
            pl.BlockSpec((None, None, D_MODEL, tf), lambda e, s: (layer, e, 0, f_idx(s))),
            pl.BlockSpec((None, None, D_EXPERT, tn), lambda e, s: (layer, e, 0, o_idx(s))),
            pl.BlockSpec((None, rows, 1), lambda e, s: (e, 0, 0)),
        ],
        out_specs=pl.BlockSpec((None, rows, tn), lambda e, s: (e, 0, o_idx(s))),
        scratch_shapes=[pltpu.VMEM((n_f, rows, tf), BF16)],
        compiler_params=pltpu.CompilerParams(dimension_semantics=("parallel", "arbitrary"),
                                             vmem_limit_bytes=EXPERT_VMEM_LIMIT),
        name="experts",
    )(xin, w_gate, w_up, w_down, gates)


def _scattered_rows(pt_ref, y_ref, x_ref, g2_ref):
    y = y_ref[...].reshape(N_EXPERTS * SLOTS, D_MODEL)
    return x_ref[...] + g2_ref[...] * _dot(pt_ref[...], y)


def _scatter_kernel(pt_ref, y_ref, x_ref, g2_ref, ng_ref, sc_ref, sh_ref, xo_ref, u_ref):
    xn = _scattered_rows(pt_ref, y_ref, x_ref, g2_ref)
    xo_ref[...] = xn
    u_ref[...] = (_rms(xn, ng_ref[...]) * (1.0 + sc_ref[...]) + sh_ref[...]).astype(BF16)


def _scatter_final_kernel(pt_ref, y_ref, x_ref, g2_ref, ng_ref, o_ref):
    o_ref[...] = _rms(_scattered_rows(pt_ref, y_ref, x_ref, g2_ref), ng_ref[...])


def _scatter(pt, y, x, mod, n_pt, next_norm=None, final_g=None):
    m = x.shape[0]
    tm = ROW_TILE
    row = _mod_row(tm, n_pt)

    def in_specs(first):
        return [
            pl.BlockSpec((tm, N_EXPERTS * SLOTS), lambda i: (i + first, 0)),
            pl.BlockSpec((N_EXPERTS, SLOTS, D_MODEL), lambda i: (0, (i + first) // (TILE // tm), 0)),
            pl.BlockSpec((tm, D_MODEL), lambda i: (i + first, 0)),
            pl.BlockSpec((None, None, 1, D_MODEL), lambda i: (row(i + first), 5, 0, 0)),
        ]

    rows = pl.BlockSpec((tm, D_MODEL), lambda i: (i, 0))
    if next_norm is None:
        def part(first, steps):
            return pl.pallas_call(
                _scatter_final_kernel,
                out_shape=jax.ShapeDtypeStruct((steps * tm, D_MODEL), F32),
                grid=(steps,),
                in_specs=in_specs(first) + [pl.BlockSpec((1, D_MODEL), lambda i: (0, 0))],
                out_specs=rows,
                compiler_params=_cparams(("parallel",)),
                name="scatter_final",
            )(pt, y, x, mod, final_g.reshape(1, D_MODEL))

        p_steps = n_pt * (TILE // tm)
        return part(0, p_steps), part(p_steps, m // tm - p_steps)
    norm_g, next_mod, next_layer = next_norm
    return pl.pallas_call(
        _scatter_kernel,
        out_shape=(jax.ShapeDtypeStruct((m, D_MODEL), F32), jax.ShapeDtypeStruct((m, D_MODEL), BF16)),
        grid=(m // tm,),
        in_specs=in_specs(0) + [pl.BlockSpec((None, 1, D_MODEL), lambda i: (next_layer, 0, 0)),
                                *_mod_specs(row, 0)],
        out_specs=(rows, rows),
        compiler_params=_cparams(("parallel",)),
        name="scatter",
    )(pt, y, x, mod, norm_g, next_mod, next_mod)


def kernel(x_prompt, x_sample, cache_k, cache_v, state_rglru, state_hgrn, c, c_ctx, w_ada, b_ada, norm1_g, norm2_g, w_in, conv_w, conv_b, rg_wr, rg_br, rg_wi, rg_bi, rg_lambda, hgrn_lb_logits, hgrn_norm_g, q_norm_g, k_norm_g, w_branch_a, w_branch_b, w_branch_c, w_out, w_router, w_exp_gate, w_exp_up, w_exp_down, final_norm_g):
    bp, seq, _ = x_prompt.shape
    bs, dec_seq, _ = x_sample.shape
    depth = w_in.shape[0]
    past = cache_k.shape[2]
    m = bp * seq + bs * dec_seq
    assert dec_seq == TILE and TILE % seq == 0 and (bp * seq) % TILE == 0 and seq & (seq - 1) == 0
    assert seq % (Q_BLK * Q_PAR) == 0 and past == seq and m % BIG_TILE == 0 and seq == HGRN_ROWS
    n_pt = bp * seq // TILE

    n_cond = -(-(1 + bs) // 8) * 8
    cond = jnp.concatenate([c_ctx[None], c, jnp.zeros((n_cond - 1 - bs, D_MODEL), F32)], axis=0)
    mod_all = _ada_mod(cond, w_ada, b_ada).reshape(depth, n_cond, N_SUB, 1, D_MODEL)
    rope_tabs = _rope_tables(dec_seq)
    mmats = jnp.asarray(_hgrn_matrices(), BF16)
    tri = jnp.asarray(_prefix_matrices(seq), BF16)
    w_router_pad = jnp.pad(w_router, ((0, 0), (0, 0), (0, E_PAD - N_EXPERTS)))
    ck = cache_k.reshape(bs, depth, past, KV_DIM)
    cv = cache_v.reshape(bs, depth, past, KV_DIM)
    n1 = norm1_g.reshape(depth, 1, D_MODEL)
    n2 = norm2_g.reshape(depth, 1, D_MODEL)
    w_out_bf16 = w_out.astype(BF16)

    k_list, v_list, ra_list, hb_list = [], [], [], []
    x, u1 = _norm_mod(x_prompt.reshape(bp * seq, D_MODEL), x_sample.reshape(bs * dec_seq, D_MODEL),
                      n1, mod_all[0], 0, n_pt)
    for l in range(depth):
        mod = mod_all[l]
        y_in, kv = _in_proj(u1, w_in, l)
        y_a, fin_a = _mixer_a(y_in, conv_w, conv_b, rg_wr, rg_br, rg_wi, rg_bi, rg_lambda, state_rglru, l, n_pt, seq)
        y_b, fin_b = _mixer_b(y_in, hgrn_lb_logits, hgrn_norm_g, state_hgrn, mmats, l, n_pt, seq)
        y_c, k_n = _attention(y_in, kv, ck, cv, rope_tabs, q_norm_g, k_norm_g, l, n_pt, seq)
        merged = _merge(y_a, y_b, y_c, y_in, w_branch_a, w_branch_b, w_branch_c, l)
        x, u2, logits = _out_proj(merged, w_out_bf16, x, mod, n2, w_router_pad, l, n_pt)
        xin, pt, gates = _route(logits, u2, tri, n_pt, seq)
        y_e = _experts(xin, gates, w_exp_gate, w_exp_up, w_exp_down, l)
        if l + 1 < depth:
            x, u1 = _scatter(pt, y_e, x, mod, n_pt, next_norm=(n1, mod_all[l + 1], l + 1))
        else:
            y_prompt, y_sample = _scatter(pt, y_e, x, mod, n_pt, final_g=final_norm_g)
        k_list.append(k_n[:bp * seq].reshape(bp, seq, KV_HEADS, HEAD))
        v_list.append(kv[:bp * seq, KV_DIM:].reshape(bp, seq, KV_HEADS, HEAD))
        ra_list.append(fin_a[:n_pt].reshape(bp, 2, D_MIX))
        hb_list.append(fin_b[:n_pt].reshape(bp, 2, N_HEADS, HEAD, HEAD))

    y_prompt = y_prompt.reshape(bp, seq, D_MODEL)
    y_sample = y_sample.reshape(bs, dec_seq, D_MODEL)
    return (y_prompt, y_sample, jnp.stack(k_list, axis=1), jnp.stack(v_list, axis=1),
            jnp.stack(ra_list, axis=1), jnp.stack(hb_list, axis=1))
```

```python
import functools

import numpy as np
import jax
import jax.numpy as jnp
from jax import lax
from jax.experimental import pallas as pl
from jax.experimental.pallas import tpu as pltpu

F32 = jnp.float32
BF16 = jnp.bfloat16

D_MODEL = 2048
D_MIX = D_MODEL // 2
HEAD = 128
N_HEADS = D_MIX // HEAD
CONV_W = 4
RG_C = 8.0
CHUNK = 64
N_LEVELS = 6
KV_HEADS = 2
KV_DIM = KV_HEADS * HEAD
GROUP = N_HEADS // KV_HEADS
ROPE_FREQS = HEAD // 4
ROPE_BASE = 10000.0
GRID_W = 64
N_EXPERTS = 16
EC_FACTOR = 2
D_EXPERT = D_MODEL // 2
N_SUB = 6
EPS = 1e-6
TILE = 1024
BIG_TILE = 2 * TILE
SLOTS = TILE * EC_FACTOR // N_EXPERTS
E_PAD = 128
Q_BLK = 128
Q_PAR = 2
HGRN_GROUP = 4
HGRN_ROWS = HGRN_GROUP * CHUNK
F32_BITS_MAX_FINITE = 0x7F7FFFFF
LOG2_E = float(np.log2(np.e))
ROW_TILE = 256
VMEM_LIMIT = 48 * 1024 * 1024
EXPERT_VMEM_LIMIT = 56 * 1024 * 1024

OFF_XA, OFF_YA, OFF_QB, OFF_FZF, OFF_FZB, OFF_IB, OFF_GB, OFF_QC = [i * D_MIX for i in range(8)]
OFF_KC = 8 * D_MIX
OFF_VC = OFF_KC + KV_DIM
OFF_GA = OFF_VC + KV_DIM
OFF_GB2 = OFF_GA + D_MODEL
OFF_GC = OFF_GB2 + D_MODEL
N_IN = OFF_GC + D_MODEL
IN_TN = 2 * KV_DIM


def _cparams(sem):
    return pltpu.CompilerParams(dimension_semantics=sem, vmem_limit_bytes=VMEM_LIMIT)


def _sig(x):
    return 0.5 * jnp.tanh(0.5 * x) + 0.5


def _silu(x):
    return x * _sig(x)


def _dot(a, b):
    return jnp.dot(a, b, preferred_element_type=F32)


def _dot_nt(a, b):
    return lax.dot_general(a, b, (((1,), (1,)), ((), ())), preferred_element_type=F32)


def _rms(x, g):
    ms = jnp.mean(x * x, axis=-1, keepdims=True)
    return x * lax.rsqrt(ms + EPS) * g


def _mod_row(tile_rows, n_pt):
    per = TILE // tile_rows if tile_rows <= TILE else None
    if per is None:
        mult = tile_rows // TILE
        return lambda i: jnp.maximum(i * mult - n_pt + 1, 0)
    return lambda i: jnp.maximum(i // per - n_pt + 1, 0)


def _ada_kernel(c_ref, w_ref, b_ref, o_ref):
    s = _silu(c_ref[...]).astype(BF16)
    o_ref[...] = _dot(s, w_ref[...].astype(BF16)) + b_ref[...]


def _ada_mod(cond, w_ada, b_ada):
    depth, _, n = w_ada.shape
    rows = cond.shape[0]
    tn = 1024
    return pl.pallas_call(
        _ada_kernel,
        out_shape=jax.ShapeDtypeStruct((depth, rows, n), F32),
        grid=(depth, n // tn),
        in_specs=[
            pl.BlockSpec((rows, D_MODEL), lambda l, j: (0, 0)),
            pl.BlockSpec((None, D_MODEL, tn), lambda l, j: (l, 0, j)),
            pl.BlockSpec((None, 1, tn), lambda l, j: (l, 0, j)),
        ],
        out_specs=pl.BlockSpec((None, rows, tn), lambda l, j: (l, 0, j)),
        compiler_params=_cparams(("parallel", "parallel")),
        name="ada_mod",
    )(cond, w_ada, b_ada.reshape(depth, 1, n))


def _norm_mod_kernel(p_steps, xp_ref, xs_ref, g_ref, sc_ref, sh_ref, x_ref, u_ref):
    i = pl.program_id(0)

    def emit(src_ref):
        x = src_ref[...]
        x_ref[...] = x
        u_ref[...] = (_rms(x, g_ref[...]) * (1.0 + sc_ref[...]) + sh_ref[...]).astype(BF16)

    @pl.when(i < p_steps)
    def _():
        emit(xp_ref)

    @pl.when(i >= p_steps)
    def _():
        emit(xs_ref)


def _mod_specs(mod_row, sub):
    mspec = lambda c: pl.BlockSpec((None, None, 1, D_MODEL), lambda i, *_: (mod_row(i), c, 0, 0))
    return mspec(3 * sub + 1), mspec(3 * sub)


def _norm_mod(x_prompt, x_sample, norm_g, mod, layer, n_pt):
    tm = 512
    p_steps = x_prompt.shape[0] // tm
    m = x_prompt.shape[0] + x_sample.shape[0]
    rows = pl.BlockSpec((tm, D_MODEL), lambda i: (i, 0))
    return pl.pallas_call(
        functools.partial(_norm_mod_kernel, p_steps),
        out_shape=(jax.ShapeDtypeStruct((m, D_MODEL), F32), jax.ShapeDtypeStruct((m, D_MODEL), BF16)),
        grid=(m // tm,),
        in_specs=[pl.BlockSpec((tm, D_MODEL), lambda i: (jnp.minimum(i, p_steps - 1), 0)),
                  pl.BlockSpec((tm, D_MODEL), lambda i: (jnp.maximum(i - p_steps, 0), 0)),
                  pl.BlockSpec((None, 1, D_MODEL), lambda i: (layer, 0, 0)),
                  *_mod_specs(_mod_row(tm, n_pt), 0)],
        out_specs=(rows, rows),
        compiler_params=_cparams(("parallel",)), name="norm_mod",
    )(x_prompt, x_sample, norm_g, mod, mod)


def _in_proj_kernel(kv_tile, u_ref, w_ref, o_ref, kv_ref):
    acc = _dot(u_ref[...], w_ref[...].astype(BF16))
    o_ref[...] = acc.astype(BF16)

    @pl.when(pl.program_id(1) == kv_tile)
    def _():
        kv_ref[...] = acc


def _in_proj(u, w_in, layer):
    m = u.shape[0]
    assert OFF_KC % IN_TN == 0 and N_IN % IN_TN == 0
    return pl.pallas_call(
        functools.partial(_in_proj_kernel, OFF_KC // IN_TN),
        out_shape=(jax.ShapeDtypeStruct((m, N_IN), BF16), jax.ShapeDtypeStruct((m, IN_TN), F32)),
        grid=(m // BIG_TILE, N_IN // IN_TN),
        in_specs=[
            pl.BlockSpec((BIG_TILE, D_MODEL), lambda i, j: (i, 0)),
            pl.BlockSpec((None, D_MODEL, IN_TN), lambda i, j: (layer, 0, j)),
        ],
        out_specs=(pl.BlockSpec((BIG_TILE, IN_TN), lambda i, j: (i, j)),
                   pl.BlockSpec((BIG_TILE, IN_TN), lambda i, j: (i, 0))),
        compiler_params=_cparams(("parallel", "arbitrary")),
        name="in_proj",
    )(u, w_in)


def _softplus(z):
    return jnp.maximum(z, 0.0) + jnp.log1p(jnp.exp(-jnp.abs(z)))


def _gelu_tanh(x):
    c = float(np.sqrt(2.0 / np.pi))
    return (0.5 * x) * (1.0 + jnp.tanh(x * (c + (c * 0.044715) * (x * x))))


def _mixa_kernel(n_pt, seq, xa_ref, ya_ref, cw_ref, cb_ref, wr_ref, br_ref, wi_ref, bi_ref, lam_ref, h0_ref,
                 y_ref, fin_ref, af_ref, bf_ref, ab_ref, bb_ref, hf_ref, hb_ref, xc_ref):
    i = pl.program_id(0)

    def body(seg_len, use_state):
        x = xa_ref[...].astype(F32)
        cw = cw_ref[...]
        taps = (pltpu.roll(x, 1, 0), x, pltpu.roll(x, TILE - 1, 0), pltpu.roll(x, TILE - 2, 0))
        conv = lambda t: t[0] * cw[0:1] + t[1] * cw[1:2] + t[2] * cw[2:3] + t[3] * cw[3:4] + cb_ref[...]
        xc_ref[...] = conv(taps)
        row8 = lax.broadcasted_iota(jnp.int32, (8, HEAD), 0)
        for s in range(TILE // seg_len):
            lo, hi = s * seg_len, (s + 1) * seg_len - 8
            first = [t[lo:lo + 8] for t in taps]
            first[0] = jnp.where(row8 >= 1, first[0], 0.0)
            xc_ref[lo:lo + 8, :] = conv(first)
            last = [t[hi:hi + 8] for t in taps]
            last[2] = jnp.where(row8 < 7, last[2], 0.0)
            last[3] = jnp.where(row8 < 6, last[3], 0.0)
            xc_ref[hi:hi + 8, :] = conv(last)
        xc = xc_ref[...]
        xcb = xc.astype(BF16)
        half_xc = 0.5 * xc
        for d, (a_ref, b_ref) in enumerate(((af_ref, bf_ref), (ab_ref, bb_ref))):
            t_r = jnp.tanh(_dot(xcb, (0.5 * wr_ref[d]).astype(BF16)) + 0.5 * br_ref[d])
            t_i = jnp.tanh(_dot(xcb, (0.5 * wi_ref[d]).astype(BF16)) + 0.5 * bi_ref[d])
            c4 = (0.5 * RG_C) * _softplus(-lam_ref[d])
            nla = c4 * t_r + c4
            a = jnp.exp2(nla * (-LOG2_E))
            a_ref[...] = a
            b_ref[...] = jnp.sqrt(jnp.tanh(nla) * (a * a + 1.0)) * ((t_i + 1.0) * half_xc)

        nb = seg_len // 8
        rowi = lax.broadcasted_iota(jnp.int32, (8, HEAD), 0)
        n_seg = TILE // seg_len
        for s in range(n_seg):
            base = s * seg_len
            if use_state:
                c0 = (h0_ref[0:1, :], h0_ref[1:2, :])
            else:
                c0 = (jnp.zeros((1, HEAD), F32), jnp.zeros((1, HEAD), F32))

            def step(j, carry):
                cf, cb = carry
                of = pl.multiple_of(base + j * 8, 8)
                ob = pl.multiple_of(base + (nb - 1 - j) * 8, 8)
                a = af_ref[pl.ds(of, 8), :]
                b = bf_ref[pl.ds(of, 8), :]
                for d in (1, 2, 4):
                    m = rowi >= d
                    b = jnp.where(m, a * pltpu.roll(b, d, 0) + b, b)
                    a = jnp.where(m, a * pltpu.roll(a, d, 0), a)
                h = b + a * cf
                hf_ref[pl.ds(of, 8), :] = h
                cf = h[7:8, :]
                a = ab_ref[pl.ds(ob, 8), :]
                b = bb_ref[pl.ds(ob, 8), :]
                for d in (1, 2, 4):
                    m = rowi < 8 - d
                    b = jnp.where(m, a * pltpu.roll(b, 8 - d, 0) + b, b)
                    a = jnp.where(m, a * pltpu.roll(a, 8 - d, 0), a)
                h = b + a * cb
                hb_ref[pl.ds(ob, 8), :] = h
                cb = h[0:1, :]
                return cf, cb

            cf, cb = lax.fori_loop(0, nb, step, c0, unroll=4)
            fin_ref[s, 0:1, :] = cf
            fin_ref[s, 1:2, :] = cb
        for s in range(n_seg, TILE // seq):
            fin_ref[s] = jnp.zeros((2, HEAD), F32)
        y_ref[...] = ((hf_ref[...] + hb_ref[...]) * _gelu_tanh(ya_ref[...].astype(F32))).astype(BF16)

    @pl.when(i < n_pt)
    def _():
        body(seq, False)

    @pl.when(i >= n_pt)
    def _():
        body(TILE, True)


def _mixer_a(y_in, conv_w, conv_b, rg_wr, rg_br, rg_wi, rg_bi, rg_lambda, state, layer, n_pt, seq):
    m = y_in.shape[0]
    depth = conv_w.shape[0]
    nt = m // TILE
    spt = TILE // seq
    req = lambda i: jnp.maximum(i - n_pt, 0)
    vec = lambda a: a.reshape(depth, 2, 1, D_MIX)
    gate_w = pl.BlockSpec((None, 2, None, HEAD, HEAD), lambda i, h: (layer, 0, h, 0, 0))
    gate_b = pl.BlockSpec((None, 2, 1, HEAD), lambda i, h: (layer, 0, 0, h))
    slab = lambda off: pl.BlockSpec((TILE, HEAD), lambda i, h: (i, off // HEAD + h))
    return pl.pallas_call(
        functools.partial(_mixa_kernel, n_pt, seq),
        out_shape=(jax.ShapeDtypeStruct((m, D_MIX), BF16),
                   jax.ShapeDtypeStruct((nt, spt, 2, D_MIX), F32)),
        grid=(nt, N_HEADS),
        in_specs=[
            slab(OFF_XA), slab(OFF_YA),
            pl.BlockSpec((None, CONV_W, HEAD), lambda i, h: (layer, 0, h)),
            pl.BlockSpec((None, 1, HEAD), lambda i, h: (layer, 0, h)),
            gate_w, gate_b, gate_w, gate_b, gate_b,
            pl.BlockSpec((None, None, 2, HEAD), lambda i, h: (req(i), layer, 0, h)),
        ],
        out_specs=(pl.BlockSpec((TILE, HEAD), lambda i, h: (i, h)),
                   pl.BlockSpec((None, spt, 2, HEAD), lambda i, h: (i, 0, 0, h))),
        scratch_shapes=[pltpu.VMEM((TILE, HEAD), F32)] * 7,
        compiler_params=_cparams(("parallel", "parallel")),
        name="mixer_a",
    )(y_in, y_in, conv_w, conv_b.reshape(depth, 1, D_MIX), rg_wr, vec(rg_br), rg_wi, vec(rg_bi), vec(rg_lambda),
      state)


def _hgrn_matrices():
    t = np.arange(CHUNK)
    return np.stack([t[None, :] <= t[:, None], t[None, :] >= t[:, None]]).astype(np.float32)


def _hgrn_decays(g, tri, reverse):
    c = CHUNK
    g_cat = jnp.concatenate([g[n * c:(n + 1) * c] for n in range(HGRN_GROUP)], axis=1)
    gh = g_cat.astype(BF16)
    gl = (g_cat - gh.astype(F32)).astype(BF16)
    cum = _dot(tri, gh) + _dot(tri, gl)
    end = cum[c - 1:c] if not reverse else cum[0:1]
    out = [cum, end - cum]
    rowi = lax.broadcasted_iota(jnp.int32, g_cat.shape, 0)
    m = c // 2
    while m >= 4:
        mids = [blk * 2 * m + (m if reverse else m - 1) for blk in range(c // (2 * m))]
        mid = jnp.concatenate([jnp.broadcast_to(cum[r:r + 1], (2 * m, g_cat.shape[1])) for r in mids], axis=0)
        out.append(pltpu.bitcast(pltpu.bitcast(cum - mid, jnp.uint32) | jnp.uint32(0x80000000), F32))
        m //= 2
    g_next = pltpu.roll(g_cat, c - 1, 0)
    g_prev = pltpu.roll(g_cat, 1, 0)
    p = rowi & 3
    if not reverse:
        out.append(jnp.where(p == 0, g_next, jnp.where(p == 1, 0.0, jnp.where(p == 2, g_cat, g_cat + g_prev))))
        out.append(jnp.where((rowi & 1) != 0, g_cat, 0.0))
    else:
        out.append(jnp.where(p == 0, g_cat + g_next, jnp.where(p == 1, g_cat, jnp.where(p == 2, 0.0, g_prev))))
        out.append(jnp.where((rowi & 1) != 0, 0.0, g_cat))
    return out


def _hgrn_side_masks(reverse):
    rowi = lax.broadcasted_iota(jnp.int32, (CHUNK, HEAD), 0)
    masks = []
    for lvl in range(N_LEVELS):
        q_side = ((rowi & (CHUNK >> (lvl + 1))) != 0) != reverse
        on = jnp.where(q_side, 1.0, 0.0)
        masks.append((on.astype(BF16), (1.0 - on).astype(BF16)))
    return masks


def _hgrn_pair_masks():
    ti = lax.broadcasted_iota(jnp.int32, (CHUNK, CHUNK), 0)
    si = lax.broadcasted_iota(jnp.int32, (CHUNK, CHUNK), 1)
    return ti == si, [(ti ^ si) < (CHUNK >> lvl) for lvl in range(N_LEVELS)]


def _hgrn_chunk_local(q, k, v, decays, masks, pair_masks, reverse):
    c = CHUNK
    e_in = jnp.exp2(decays[0])
    e_out = jnp.exp2(decays[1])
    vb = v.astype(BF16)
    qb = q.astype(BF16)
    kb = k.astype(BF16)
    diag, same_block = pair_masks
    scores = None
    for lvl in range(N_LEVELS):
        eb = jnp.exp2(decays[2 + lvl]).astype(BF16)
        mq, mk = masks[lvl]
        part = _dot_nt(qb * eb * mq, kb * eb * mk)
        scores = part if scores is None else jnp.where(same_block[lvl], part, scores)
    scores = jnp.where(diag, _dot_nt(qb, kb), scores)
    ke = (k * e_out).astype(BF16)
    own = lax.dot_general(vb, ke, (((0,), (0,)), ((), ())), preferred_element_type=F32)
    d_end = e_in[c - 1:c] if not reverse else e_in[0:1]
    return (q * e_in).astype(BF16), scores.astype(BF16), vb, own, d_end


def _mixb_kernel(n_pt, seq, layer, qb_ref, fzf_ref, fzb_ref, ib_ref, gb_ref, lbl_ref, s0_ref, ng_ref, mm_ref,
                 y_ref, sfin_ref, of_ref, ob_ref):
    i = pl.program_id(0)
    depth = lbl_ref.shape[0]

    def lower_bound(d):
        lg = [lbl_ref[j, d] for j in range(depth)]
        mx = functools.reduce(jnp.maximum, lg)
        ex = [jnp.exp(v - mx) for v in lg]
        tot = functools.reduce(lambda a, b: a + b, ex)
        acc = jnp.zeros_like(tot)
        for j in range(1, layer + 1):
            acc = acc + ex[j] / tot
        return acc

    def body(seg_len, use_state):
        lbs = (lower_bound(0), lower_bound(1))
        n_grp = seg_len // HGRN_ROWS
        chunk_rows = lambda a, n: a[n * CHUNK:(n + 1) * CHUNK]

        def load(r0, d):
            rows = pl.ds(r0, HGRN_ROWS)
            q = _silu(qb_ref[rows, :].astype(F32))
            v = ib_ref[rows, :].astype(F32)
            fz = (fzf_ref if d == 0 else fzb_ref)[rows, :].astype(F32)
            f = lbs[d] + (1.0 - lbs[d]) / (1.0 + jnp.exp(-fz))
            return q, 1.0 - f, v, jnp.log2(f)

        def group(r_f, r_b, st_f, st_b):
            qf, kf, vf, gf = load(r_f, 0)
            qb, kb, vb, gb = load(r_b, 1)
            pair_masks = _hgrn_pair_masks()
            masks_f = _hgrn_side_masks(False)
            masks_b = _hgrn_side_masks(True)
            ef = _hgrn_decays(gf, mm_ref[0], False)
            eb = _hgrn_decays(gb, mm_ref[1], True)
            lanes = lambda e, n: [a[:, n * HEAD:(n + 1) * HEAD] for a in e]
            loc_f = [_hgrn_chunk_local(chunk_rows(qf, n), chunk_rows(kf, n), chunk_rows(vf, n),
                                       lanes(ef, n), masks_f, pair_masks, False) for n in range(HGRN_GROUP)]
            loc_b = [_hgrn_chunk_local(chunk_rows(qb, n), chunk_rows(kb, n), chunk_rows(vb, n),
                                       lanes(eb, n), masks_b, pair_masks, True) for n in range(HGRN_GROUP)]
            for n in range(HGRN_GROUP):
                qe, sc, vals, own, d_end = loc_f[n]
                of_ref[pl.ds(r_f + n * CHUNK, CHUNK), :] = _dot_nt(qe, st_f.astype(BF16)) + _dot(sc, vals)
                st_f = st_f * d_end + own
            for n in reversed(range(HGRN_GROUP)):
                qe, sc, vals, own, d_end = loc_b[n]
                ob_ref[pl.ds(r_b + n * CHUNK, CHUNK), :] = _dot_nt(qe, st_b.astype(BF16)) + _dot(sc, vals)
                st_b = st_b * d_end + own
            return st_f, st_b

        if use_state:
            def step(j, carry):
                r_f = pl.multiple_of(j * HGRN_ROWS, HGRN_ROWS)
                r_b = pl.multiple_of((n_grp - 1 - j) * HGRN_ROWS, HGRN_ROWS)
                return group(r_f, r_b, *carry)

            st_f, st_b = lax.fori_loop(0, n_grp, step, (s0_ref[0].T, s0_ref[1].T))
            sfin_ref[0, 0] = st_f.T
            sfin_ref[0, 1] = st_b.T
            for s in range(1, TILE // seq):
                sfin_ref[s] = jnp.zeros((2, HEAD, HEAD), F32)
        else:
            def request(s, carry):
                r0 = pl.multiple_of(s * HGRN_ROWS, HGRN_ROWS)
                zero = jnp.zeros((HEAD, HEAD), F32)
                st_f, st_b = group(r0, r0, zero, zero)
                sfin_ref[s, 0] = st_f.T
                sfin_ref[s, 1] = st_b.T
                return carry

            lax.fori_loop(0, TILE // seg_len, request, 0)
        o = _rms(of_ref[...] + ob_ref[...], ng_ref[...]) * _silu(gb_ref[...].astype(F32))
        y_ref[...] = o.astype(BF16)

    @pl.when(i < n_pt)
    def _():
        body(seq, False)

    @pl.when(i >= n_pt)
    def _():
        body(TILE, True)


def _mixer_b(y_in, lb_logits, norm_g, state, mmats, layer, n_pt, seq):
    m = y_in.shape[0]
    depth = lb_logits.shape[0]
    nt = m // TILE
    spt = TILE // seq
    req = lambda i: jnp.maximum(i - n_pt, 0)
    slab = lambda off: pl.BlockSpec((TILE, HEAD), lambda i, h: (i, off // HEAD + h))
    return pl.pallas_call(
        functools.partial(_mixb_kernel, n_pt, seq, layer),
        out_shape=(jax.ShapeDtypeStruct((m, D_MIX), BF16),
                   jax.ShapeDtypeStruct((nt, spt, 2, N_HEADS, HEAD, HEAD), F32)),
        grid=(nt, N_HEADS),
        in_specs=[
            slab(OFF_QB), slab(OFF_FZF), slab(OFF_FZB), slab(OFF_IB), slab(OFF_GB),
            pl.BlockSpec((depth, 2, 1, HEAD), lambda i, h: (0, 0, 0, h)),
            pl.BlockSpec((None, None, 2, None, HEAD, HEAD), lambda i, h: (req(i), layer, 0, h, 0, 0)),
            pl.BlockSpec((None, 1, HEAD), lambda i, h: (layer, 0, 0)),
            pl.BlockSpec(mmats.shape, lambda i, h: (0, 0, 0)),
        ],
        out_specs=(pl.BlockSpec((TILE, HEAD), lambda i, h: (i, h)),
                   pl.BlockSpec((None, spt, 2, None, HEAD, HEAD), lambda i, h: (i, 0, 0, h, 0, 0))),
        scratch_shapes=[pltpu.VMEM((TILE, HEAD), F32)] * 2,
        compiler_params=_cparams(("parallel", "parallel")),
        name="mixer_b",
    )(y_in, y_in, y_in, y_in, y_in, lb_logits.reshape(depth, 2, 1, D_MIX), state,
      norm_g.reshape(depth, 1, HEAD), mmats)


def _rope(x, cos, sin_lo, sin_hi):
    return x * cos + pltpu.roll(x, HEAD - ROPE_FREQS, 1) * sin_lo + pltpu.roll(x, ROPE_FREQS, 1) * sin_hi


def _attn_kernel(n_pt, seq, q_ref, k_ref, v_ref, ck_ref, cv_ref, cos_ref, slo_ref, shi_ref, qg_ref, kg_ref,
                 y_ref, kn_ref, qs_ref, ks_ref, vs_ref):
    i = pl.program_id(0)
    scale = HEAD ** -0.5

    def body(seg_len, ctx):
        kn = _rms(k_ref[...], kg_ref[...])
        kn_ref[...] = kn
        if ctx:
            kn = _rope(kn, cos_ref[...], slo_ref[...], shi_ref[...])
        ks_ref[...] = kn.astype(BF16)
        vs_ref[...] = v_ref[...].astype(BF16)
        for g in range(GROUP):
            qn = _rms(q_ref[:, g * HEAD:(g + 1) * HEAD].astype(F32), qg_ref[...])
            if ctx:
                qn = _rope(qn, cos_ref[...], slo_ref[...], shi_ref[...])
            qs_ref[g] = (qn * scale).astype(BF16)
        if ctx:
            ckb = ck_ref[...].astype(BF16)
            cvb = cv_ref[...].astype(BF16)
        for s in range(TILE // seg_len):
            base = s * seg_len

            def qblocks(j, carry):
                keys = ks_ref[pl.ds(base, seg_len), :]
                vals = vs_ref[pl.ds(base, seg_len), :]
                rows = [pl.ds(pl.multiple_of(base + (j * Q_PAR + t) * Q_BLK, Q_BLK), Q_BLK) for t in range(Q_PAR)]
                qcat = [jnp.concatenate([qs_ref[g, r, :] for g in range(GROUP)], axis=0) for r in rows]
                sc = [_dot_nt(qc, keys) for qc in qcat]
                sc_c = [_dot_nt(qc, ckb) for qc in qcat] if ctx else None
                for t in range(Q_PAR):
                    mx = jnp.max(sc[t], axis=-1, keepdims=True)
                    if ctx:
                        mx = jnp.maximum(mx, jnp.max(sc_c[t], axis=-1, keepdims=True))
                    p = jnp.exp(sc[t] - mx)
                    den = jnp.sum(p, axis=-1, keepdims=True)
                    o = _dot(p.astype(BF16), vals)
                    if ctx:
                        pc = jnp.exp(sc_c[t] - mx)
                        den = den + jnp.sum(pc, axis=-1, keepdims=True)
                        o = o + _dot(pc.astype(BF16), cvb)
                    o = o / den
                    for g in range(GROUP):
                        y_ref[rows[t], g * HEAD:(g + 1) * HEAD] = o[g * Q_BLK:(g + 1) * Q_BLK].astype(BF16)
                return carry

            lax.fori_loop(0, seg_len // (Q_BLK * Q_PAR), qblocks, 0)

    @pl.when(i < n_pt)
    def _():
        body(seq, False)

    @pl.when(i >= n_pt)
    def _():
        body(TILE, True)


def _attention(y_in, kv, cache_k, cache_v, rope_tabs, q_norm_g, k_norm_g, layer, n_pt, seq):
    m = y_in.shape[0]
    depth = q_norm_g.shape[0]
    nt = m // TILE
    gw = GROUP * HEAD
    past = cache_k.shape[2]
    req = lambda i: jnp.maximum(i - n_pt, 0)
    ctx_spec = pl.BlockSpec((None, None, past, HEAD), lambda i, h: (req(i), layer, 0, h))
    tab = pl.BlockSpec((TILE, HEAD), lambda i, h: (0, 0))
    gspec = pl.BlockSpec((None, 1, HEAD), lambda i, h: (layer, 0, 0))
    return pl.pallas_call(
        functools.partial(_attn_kernel, n_pt, seq),
        out_shape=(jax.ShapeDtypeStruct((m, D_MIX), BF16),
                   jax.ShapeDtypeStruct((m, KV_DIM), F32)),
        grid=(nt, KV_HEADS),
        in_specs=[
            pl.BlockSpec((TILE, gw), lambda i, h: (i, OFF_QC // gw + h)),
            pl.BlockSpec((TILE, HEAD), lambda i, h: (i, h)),
            pl.BlockSpec((TILE, HEAD), lambda i, h: (i, KV_HEADS + h)),
            ctx_spec, ctx_spec, tab, tab, tab, gspec, gspec,
        ],
        out_specs=(pl.BlockSpec((TILE, gw), lambda i, h: (i, h)),
                   pl.BlockSpec((TILE, HEAD), lambda i, h: (i, h))),
        scratch_shapes=[pltpu.VMEM((GROUP, TILE, HEAD), BF16), pltpu.VMEM((TILE, HEAD), BF16),
                        pltpu.VMEM((TILE, HEAD), BF16)],
        compiler_params=_cparams(("parallel", "parallel")),
        name="attention",
    )(y_in, kv, kv, cache_k, cache_v, *rope_tabs, q_norm_g.reshape(depth, 1, HEAD),
      k_norm_g.reshape(depth, 1, HEAD))


def _rope_tables(n):
    rows = n // GRID_W
    r = jnp.repeat(jnp.arange(rows, dtype=F32), GRID_W)
    col = jnp.tile(jnp.arange(GRID_W, dtype=F32), rows)
    inv = ROPE_BASE ** (-jnp.arange(ROPE_FREQS, dtype=F32) / ROPE_FREQS)
    ar = r[:, None] * inv
    ac = col[:, None] * inv
    ang = jnp.concatenate([ar, ar, ac, ac], axis=-1)
    cos, sin = jnp.cos(ang), jnp.sin(ang)
    first_half = (jnp.arange(HEAD) // ROPE_FREQS) % 2 == 0
    return cos, jnp.where(first_half, -sin, 0.0), jnp.where(first_half, 0.0, sin)


def _merge_kernel(ya_ref, yb_ref, yc_ref, wa_ref, wb_ref, wc_ref, ga_ref, gb_ref, gc_ref, o_ref):
    acc = _sig(ga_ref[...].astype(F32)) * _dot(ya_ref[...], wa_ref[...].astype(BF16))
    acc = acc + _sig(gb_ref[...].astype(F32)) * _dot(yb_ref[...], wb_ref[...].astype(BF16))
    acc = acc + _sig(gc_ref[...].astype(F32)) * _dot(yc_ref[...], wc_ref[...].astype(BF16))
    o_ref[...] = acc.astype(BF16)


def _merge(y_a, y_b, y_c, y_in, w_a, w_b, w_c, layer):
    m = y_in.shape[0]
    tn = 256
    act = pl.BlockSpec((BIG_TILE, D_MIX), lambda i, j: (i, 0))
    wsp = pl.BlockSpec((None, D_MIX, tn), lambda i, j: (layer, 0, j))
    gate = lambda off: pl.BlockSpec((BIG_TILE, tn), lambda i, j: (i, off // tn + j))
    return pl.pallas_call(
        _merge_kernel,
        out_shape=jax.ShapeDtypeStruct((m, D_MODEL), BF16),
        grid=(m // BIG_TILE, D_MODEL // tn),
        in_specs=[act, act, act, wsp, wsp, wsp, gate(OFF_GA), gate(OFF_GB2), gate(OFF_GC)],
        out_specs=pl.BlockSpec((BIG_TILE, tn), lambda i, j: (i, j)),
        compiler_params=_cparams(("parallel", "parallel")),
        name="merge",
    )(y_a, y_b, y_c, w_a, w_b, w_c, y_in, y_in, y_in)


def _out_proj_kernel(mg_ref, w_ref, x_ref, g1_ref, ng_ref, sc_ref, sh_ref, wr_ref, xo_ref, u_ref, lg_ref):
    xn = x_ref[...] + g1_ref[...] * _dot(mg_ref[...], w_ref[...])
    xo_ref[...] = xn
    u = _rms(xn, ng_ref[...]) * (1.0 + sc_ref[...]) + sh_ref[...]
    u_ref[...] = u.astype(BF16)
    w = wr_ref[...]
    uh = u.astype(BF16)
    ul = (u - uh.astype(F32)).astype(BF16)
    wh = w.astype(BF16)
    wl = (w - wh.astype(F32)).astype(BF16)
    lg_ref[...] = _dot(uh, wh) + (_dot(uh, wl) + _dot(ul, wh))


def _out_proj(merged, w_out_bf16, x, mod, norm_g, w_router_pad, layer, n_pt):
    m = x.shape[0]
    tm = ROW_TILE
    row = _mod_row(tm, n_pt)
    return pl.pallas_call(
        _out_proj_kernel,
        out_shape=(jax.ShapeDtypeStruct((m, D_MODEL), F32),
                   jax.ShapeDtypeStruct((m, D_MODEL), BF16),
                   jax.ShapeDtypeStruct((m, E_PAD), F32)),
        grid=(m // tm,),
        in_specs=[
            pl.BlockSpec((tm, D_MODEL), lambda i: (i, 0)),
            pl.BlockSpec((None, D_MODEL, D_MODEL), lambda i: (layer, 0, 0)),
            pl.BlockSpec((tm, D_MODEL), lambda i: (i, 0)),
            pl.BlockSpec((None, None, 1, D_MODEL), lambda i: (row(i), 2, 0, 0)),
            pl.BlockSpec((None, 1, D_MODEL), lambda i: (layer, 0, 0)),
            *_mod_specs(row, 1),
            pl.BlockSpec((None, D_MODEL, E_PAD), lambda i: (layer, 0, 0)),
        ],
        out_specs=(pl.BlockSpec((tm, D_MODEL), lambda i: (i, 0)),
                   pl.BlockSpec((tm, D_MODEL), lambda i: (i, 0)),
                   pl.BlockSpec((tm, E_PAD), lambda i: (i, 0))),
        compiler_params=_cparams(("parallel",)),
        name="out_proj",
    )(merged, w_out_bf16, x, mod, norm_g, mod, mod, w_router_pad)


def _prefix_matrices(seq):
    t = np.arange(TILE)
    before = t[:, None] < t[None, :]
    same = (t[:, None] // seq) == (t[None, :] // seq)
    return np.stack([before & same, before]).astype(np.float32)


def _route_kernel(n_pt, seq, lg_ref, u_ref, tri_ref, expand_ref, xin_ref, pt_ref, gate_ref, p_ref):
    i = pl.program_id(0)

    def body(seg_len):
        n_seg = TILE // seg_len
        cap = EC_FACTOR * seg_len // N_EXPERTS
        lt = lg_ref[...].T[0:N_EXPERTS]
        ex = jnp.exp(lt - jnp.max(lt, axis=0, keepdims=True))
        aff = ex / jnp.sum(ex, axis=0, keepdims=True)
        bits = pltpu.bitcast(aff, jnp.int32)
        segs = [bits[:, s * seg_len:(s + 1) * seg_len] for s in range(n_seg)]

        def count_ge(seg, thr):
            return jnp.sum(jnp.where(seg >= thr, 1.0, 0.0), axis=1, keepdims=True)

        def bisect(_, carry):
            out = []
            for seg, (lo, hi) in zip(segs, carry):
                mid = lo + ((hi - lo + 1) >> 1)
                ok = count_ge(seg, mid) >= cap
                out.append((jnp.where(ok, mid, lo), jnp.where(ok, hi, mid - 1)))
            return tuple(out)

        init = tuple((jnp.zeros((N_EXPERTS, 1), jnp.int32),
                      jnp.full((N_EXPERTS, 1), F32_BITS_MAX_FINITE, jnp.int32)) for _ in segs)
        bounds = lax.fori_loop(0, 31, bisect, init)
        gt_l, eq_l, need_l = [], [], []
        for seg, (thr, _) in zip(segs, bounds):
            gt = seg > thr
            gt_l.append(jnp.where(gt, 1.0, 0.0))
            eq_l.append(jnp.where(seg == thr, 1.0, 0.0))
            n_gt = jnp.sum(gt_l[-1], axis=1, keepdims=True)
            need_l.append(jnp.broadcast_to(cap - n_gt, (N_EXPERTS, seg_len)))
        gt = jnp.concatenate(gt_l, axis=1)
        eq = jnp.concatenate(eq_l, axis=1)
        need = jnp.concatenate(need_l, axis=1)
        before = _dot(jnp.concatenate([gt, eq], axis=0).astype(BF16), tri_ref[...])
        gt_before, eq_before = before[0:N_EXPERTS], before[N_EXPERTS:]
        chosen = jnp.logical_or(gt > 0.0, jnp.logical_and(eq > 0.0, eq_before < need))
        tok = lax.broadcasted_iota(jnp.int32, (N_EXPERTS, TILE), 1)
        seg_off = ((tok // seg_len) * cap).astype(F32)
        slot = jnp.where(chosen, gt_before + jnp.minimum(eq_before, need) + seg_off, -1.0)
        slot_t = jnp.concatenate([slot, jnp.full((E_PAD - N_EXPERTS, TILE), -1.0, F32)], axis=0).T
        sub = lax.broadcasted_iota(jnp.int32, (SLOTS, TILE), 0).astype(F32)
        lane = lax.broadcasted_iota(jnp.int32, (TILE, SLOTS), 1).astype(F32)
        for e in range(N_EXPERTS):
            hit = slot[e:e + 1] == sub
            gate_ref[e] = jnp.sum(jnp.where(hit, aff[e:e + 1], 0.0), axis=1, keepdims=True)
            p_ref[e * SLOTS:(e + 1) * SLOTS, :] = jnp.where(hit, 1.0, 0.0).astype(BF16)
            if n_seg == 1:
                hit_t = slot_t[:, e:e + 1] == lane
                pt_ref[:, e * SLOTS:(e + 1) * SLOTS] = jnp.where(hit_t, 1.0, 0.0).astype(BF16)
        if n_seg > 1:
            grp = N_EXPERTS * cap
            slot_in_req = (lax.broadcasted_iota(jnp.int32, (seg_len, grp), 1) % cap).astype(F32)
            zeros = jnp.zeros((seg_len, grp), BF16)
            for s in range(n_seg):
                mine = slot_t[s * seg_len:(s + 1) * seg_len].astype(BF16)
                per_col = _dot(mine, expand_ref[...]) - float(s * cap)
                hit_t = jnp.where(per_col == slot_in_req, 1.0, 0.0).astype(BF16)
                pt_ref[s * seg_len:(s + 1) * seg_len, :] = jnp.concatenate(
                    [hit_t if c == s else zeros for c in range(n_seg)], axis=1)
        tn = 512
        if n_seg == 1:
            for n in range(D_MODEL // tn):
                rows = _dot(p_ref[...], u_ref[:, n * tn:(n + 1) * tn])
                xin_ref[:, :, n * tn:(n + 1) * tn] = rows.astype(BF16).reshape(N_EXPERTS, SLOTS, tn)
        else:
            for s in range(n_seg):
                tok = slice(s * seg_len, (s + 1) * seg_len)
                sel = jnp.concatenate([p_ref[e * SLOTS + s * cap:e * SLOTS + (s + 1) * cap, tok]
                                       for e in range(N_EXPERTS)], axis=0)
                for n in range(D_MODEL // tn):
                    rows = _dot(sel, u_ref[tok, n * tn:(n + 1) * tn]).astype(BF16)
                    for e in range(N_EXPERTS):
                        xin_ref[e, s * cap:(s + 1) * cap, n * tn:(n + 1) * tn] = rows[e * cap:(e + 1) * cap]

    @pl.when(i < n_pt)
    def _():
        body(seq)

    @pl.when(i >= n_pt)
    def _():
        body(TILE)


def _expert_expansion(seq):
    cap = EC_FACTOR * seq // N_EXPERTS
    mat = np.zeros((E_PAD, N_EXPERTS * cap), np.float32)
    for e in range(N_EXPERTS):
        mat[e, e * cap:(e + 1) * cap] = 1.0
    return mat


def _route(logits, u, tri, expand, n_pt, seq):
    m = u.shape[0]
    nt = m // TILE
    return pl.pallas_call(
        functools.partial(_route_kernel, n_pt, seq),
        out_shape=(jax.ShapeDtypeStruct((N_EXPERTS, nt * SLOTS, D_MODEL), BF16),
                   jax.ShapeDtypeStruct((m, N_EXPERTS * SLOTS), BF16),
                   jax.ShapeDtypeStruct((N_EXPERTS, nt * SLOTS, 1), F32)),
        grid=(nt,),
        in_specs=[pl.BlockSpec((TILE, E_PAD), lambda i: (i, 0)),
                  pl.BlockSpec((TILE, D_MODEL), lambda i: (i, 0)),
                  pl.BlockSpec((None, TILE, TILE), lambda i: (jnp.where(i < n_pt, 0, 1), 0, 0)),
                  pl.BlockSpec(expand.shape, lambda i: (0, 0))],
        out_specs=(pl.BlockSpec((N_EXPERTS, SLOTS, D_MODEL), lambda i: (0, i, 0)),
                   pl.BlockSpec((TILE, N_EXPERTS * SLOTS), lambda i: (i, 0)),
                   pl.BlockSpec((N_EXPERTS, SLOTS, 1), lambda i: (0, i, 0))),
        scratch_shapes=[pltpu.VMEM((N_EXPERTS * SLOTS, TILE), BF16)],
        compiler_params=_cparams(("parallel",)),
        name="route",
    )(logits, u, tri, expand)


def _expert_kernel(n_f, x_ref, wg_ref, wu_ref, wd_ref, gate_ref, o_ref, h_ref):
    s = pl.program_id(1)
    tf = wg_ref.shape[-1]

    @pl.when(s < n_f)
    def _():
        x = x_ref[...]
        h = _silu(_dot(x, wg_ref[...].astype(BF16))) * _dot(x, wu_ref[...].astype(BF16))
        h_ref[s] = h.astype(BF16)

    @pl.when(s >= n_f)
    def _():
        acc = _dot(h_ref[0], wd_ref[0:tf, :].astype(BF16))
        for k in range(1, n_f):
            acc = acc + _dot(h_ref[k], wd_ref[k * tf:(k + 1) * tf, :].astype(BF16))
        o_ref[...] = (acc * gate_ref[...]).astype(BF16)


def _experts(xin, gates, w_gate, w_up, w_down, layer):
    _, rows, _ = xin.shape
    tf, tn = 256, 1024
    n_f, n_o = D_EXPERT // tf, D_MODEL // tn
    f_idx = lambda s: jnp.minimum(s, n_f - 1)
    o_idx = lambda s: jnp.maximum(s - n_f, 0)
    x_idx = lambda e, s: jnp.minimum(e + jnp.where(s >= n_f, 1, 0), N_EXPERTS - 1)
    return pl.pallas_call(
        functools.partial(_expert_kernel, n_f),
        out_shape=jax.ShapeDtypeStruct((N_EXPERTS, rows, D_MODEL), BF16),
        grid=(N_EXPERTS, n_f + n_o),
        in_specs=[
            pl.BlockSpec((None, rows, D_MODEL), lambda e, s: (x_idx(e, s), 0, 0)),
            pl.BlockSpec((None, None, D_MODEL, tf), lambda e, s: (layer, e, 0, f_idx(s))),
            pl.BlockSpec((None, None, D_MODEL, tf), lambda e, s: (layer, e, 0, f_idx(s))),
            pl.BlockSpec((None, None, D_EXPERT, tn), lambda e, s: (layer, e, 0, o_idx(s))),
            pl.BlockSpec((None, rows, 1), lambda e, s: (e, 0, 0)),
        ],
        out_specs=pl.BlockSpec((None, rows, tn), lambda e, s: (e, 0, o_idx(s))),
        scratch_shapes=[pltpu.VMEM((n_f, rows, tf), BF16)],
        compiler_params=pltpu.CompilerParams(dimension_semantics=("parallel", "arbitrary"),
                                             vmem_limit_bytes=EXPERT_VMEM_LIMIT),
        name="experts",
    )(xin, w_gate, w_up, w_down, gates)


def _scatter_steps(first, p_steps, seq, pt_ref, y_ref, x_ref, g2_ref, emit):
    i = pl.program_id(0) + first
    per_tile = TILE // seq
    cap = EC_FACTOR * seq // N_EXPERTS
    grp = N_EXPERTS * cap

    @pl.when(i >= p_steps)
    def _():
        emit(x_ref[...] + g2_ref[...] * _dot(pt_ref[...], y_ref[...].reshape(N_EXPERTS * SLOTS, D_MODEL)))

    for s in range(per_tile):
        @pl.when(jnp.logical_and(i < p_steps, i % per_tile == s))
        def _():
            mine = jnp.concatenate([y_ref[e, s * cap:(s + 1) * cap, :] for e in range(N_EXPERTS)], axis=0)
            emit(x_ref[...] + g2_ref[...] * _dot(pt_ref[:, s * grp:(s + 1) * grp], mine))


def _scatter_kernel(first, p_steps, seq, pt_ref, y_ref, x_ref, g2_ref, ng_ref, sc_ref, sh_ref, xo_ref, u_ref):
    def emit(xn):
        xo_ref[...] = xn
        u_ref[...] = (_rms(xn, ng_ref[...]) * (1.0 + sc_ref[...]) + sh_ref[...]).astype(BF16)

    _scatter_steps(first, p_steps, seq, pt_ref, y_ref, x_ref, g2_ref, emit)


def _scatter_final_kernel(first, p_steps, seq, pt_ref, y_ref, x_ref, g2_ref, ng_ref, o_ref):
    def emit(xn):
        o_ref[...] = _rms(xn, ng_ref[...])

    _scatter_steps(first, p_steps, seq, pt_ref, y_ref, x_ref, g2_ref, emit)


def _scatter(pt, y, x, mod, n_pt, seq, next_norm=None, final_g=None):
    m = x.shape[0]
    tm = ROW_TILE
    assert tm == seq
    row = _mod_row(tm, n_pt)
    p_steps = n_pt * (TILE // tm)

    def in_specs(first):
        return [
            pl.BlockSpec((tm, N_EXPERTS * SLOTS), lambda i: (i + first, 0)),
            pl.BlockSpec((N_EXPERTS, SLOTS, D_MODEL), lambda i: (0, (i + first) // (TILE // tm), 0)),
            pl.BlockSpec((tm, D_MODEL), lambda i: (i + first, 0)),
            pl.BlockSpec((None, None, 1, D_MODEL), lambda i: (row(i + first), 5, 0, 0)),
        ]

    rows = pl.BlockSpec((tm, D_MODEL), lambda i: (i, 0))
    if next_norm is None:
        def part(first, steps):
            return pl.pallas_call(
                functools.partial(_scatter_final_kernel, first, p_steps, seq),
                out_shape=jax.ShapeDtypeStruct((steps * tm, D_MODEL), F32),
                grid=(steps,),
                in_specs=in_specs(first) + [pl.BlockSpec((1, D_MODEL), lambda i: (0, 0))],
                out_specs=rows,
                compiler_params=_cparams(("parallel",)),
                name="scatter_final",
            )(pt, y, x, mod, final_g.reshape(1, D_MODEL))

        return part(0, p_steps), part(p_steps, m // tm - p_steps)
    norm_g, next_mod, next_layer = next_norm
    return pl.pallas_call(
        functools.partial(_scatter_kernel, 0, p_steps, seq),
        out_shape=(jax.ShapeDtypeStruct((m, D_MODEL), F32), jax.ShapeDtypeStruct((m, D_MODEL), BF16)),
        grid=(m // tm,),
        in_specs=in_specs(0) + [pl.BlockSpec((None, 1, D_MODEL), lambda i: (next_layer, 0, 0)),
                                *_mod_specs(row, 0)],
        out_specs=(rows, rows),
        compiler_params=_cparams(("parallel",)),
        name="scatter",
    )(pt, y, x, mod, norm_g, next_mod, next_mod)


def kernel(x_prompt, x_sample, cache_k, cache_v, state_rglru, state_hgrn, c, c_ctx, w_ada, b_ada, norm1_g, norm2_g, w_in, conv_w, conv_b, rg_wr, rg_br, rg_wi, rg_bi, rg_lambda, hgrn_lb_logits, hgrn_norm_g, q_norm_g, k_norm_g, w_branch_a, w_branch_b, w_branch_c, w_out, w_router, w_exp_gate, w_exp_up, w_exp_down, final_norm_g):
    bp, seq, _ = x_prompt.shape
    bs, dec_seq, _ = x_sample.shape
    depth = w_in.shape[0]
    past = cache_k.shape[2]
    m = bp * seq + bs * dec_seq
    assert dec_seq == TILE and TILE % seq == 0 and (bp * seq) % TILE == 0 and seq & (seq - 1) == 0
    assert seq % (Q_BLK * Q_PAR) == 0 and past == seq and m % BIG_TILE == 0 and seq == HGRN_ROWS
    n_pt = bp * seq // TILE

    n_cond = -(-(1 + bs) // 8) * 8
    cond = jnp.concatenate([c_ctx[None], c, jnp.zeros((n_cond - 1 - bs, D_MODEL), F32)], axis=0)
    mod_all = _ada_mod(cond, w_ada, b_ada).reshape(depth, n_cond, N_SUB, 1, D_MODEL)
    rope_tabs = _rope_tables(dec_seq)
    mmats = jnp.asarray(_hgrn_matrices(), BF16)
    tri = jnp.asarray(_prefix_matrices(seq), BF16)
    expand = jnp.asarray(_expert_expansion(seq), BF16)
    w_router_pad = jnp.pad(w_router, ((0, 0), (0, 0), (0, E_PAD - N_EXPERTS)))
    ck = cache_k.reshape(bs, depth, past, KV_DIM)
    cv = cache_v.reshape(bs, depth, past, KV_DIM)
    n1 = norm1_g.reshape(depth, 1, D_MODEL)
    n2 = norm2_g.reshape(depth, 1, D_MODEL)
    w_out_bf16 = w_out.astype(BF16)

    k_list, v_list, ra_list, hb_list = [], [], [], []
    x, u1 = _norm_mod(x_prompt.reshape(bp * seq, D_MODEL), x_sample.reshape(bs * dec_seq, D_MODEL),
                      n1, mod_all[0], 0, n_pt)
    for l in range(depth):
        mod = mod_all[l]
        y_in, kv = _in_proj(u1, w_in, l)
        y_a, fin_a = _mixer_a(y_in, conv_w, conv_b, rg_wr, rg_br, rg_wi, rg_bi, rg_lambda, state_rglru, l, n_pt, seq)
        y_b, fin_b = _mixer_b(y_in, hgrn_lb_logits, hgrn_norm_g, state_hgrn, mmats, l, n_pt, seq)
        y_c, k_n = _attention(y_in, kv, ck, cv, rope_tabs, q_norm_g, k_norm_g, l, n_pt, seq)
        merged = _merge(y_a, y_b, y_c, y_in, w_branch_a, w_branch_b, w_branch_c, l)
        x, u2, logits = _out_proj(merged, w_out_bf16, x, mod, n2, w_router_pad, l, n_pt)
        xin, pt, gates = _route(logits, u2, tri, expand, n_pt, seq)
        y_e = _experts(xin, gates, w_exp_gate, w_exp_up, w_exp_down, l)
        if l + 1 < depth:
            x, u1 = _scatter(pt, y_e, x, mod, n_pt, seq, next_norm=(n1, mod_all[l + 1], l + 1))
        else:
            y_prompt, y_sample = _scatter(pt, y_e, x, mod, n_pt, seq, final_g=final_norm_g)
        k_list.append(k_n[:bp * seq].reshape(bp, seq, KV_HEADS, HEAD))
        v_list.append(kv[:bp * seq, KV_DIM:].reshape(bp, seq, KV_HEADS, HEAD))
        ra_list.append(fin_a[:n_pt].reshape(bp, 2, D_MIX))
        hb_list.append(fin_b[:n_pt].reshape(bp, 2, N_HEADS, HEAD, HEAD))

    y_prompt = y_prompt.reshape(bp, seq, D_MODEL)
    y_sample = y_sample.reshape(bs, dec_seq, D_MODEL)
    return (y_prompt, y_sample, jnp.stack(k_list, axis=1), jnp.stack(v_list, axis=1),
            jnp.stack(ra_list, axis=1), jnp.stack(hb_list, axis=1))
```

```python
import functools

import numpy as np
import jax
import jax.numpy as jnp
from jax import lax
from jax.experimental import pallas as pl
from jax.experimental.pallas import tpu as pltpu

F32 = jnp.float32
BF16 = jnp.bfloat16

D_MODEL = 2048
D_MIX = D_MODEL // 2
HEAD = 128
N_HEADS = D_MIX // HEAD
CONV_W = 4
RG_C = 8.0
CHUNK = 64
N_LEVELS = 6
KV_HEADS = 2
KV_DIM = KV_HEADS * HEAD
GROUP = N_HEADS // KV_HEADS
ROPE_FREQS = HEAD // 4
ROPE_BASE = 10000.0
GRID_W = 64
N_EXPERTS = 16
EC_FACTOR = 2
D_EXPERT = D_MODEL // 2
N_SUB = 6
EPS = 1e-6
TILE = 1024
BIG_TILE = 2 * TILE
SLOTS = TILE * EC_FACTOR // N_EXPERTS
E_PAD = 128
Q_BLK = 128
Q_PAR = 4
HGRN_GROUP = 4
HGRN_ROWS = HGRN_GROUP * CHUNK
AFF_TINY = 1e-37
BISECT_STEPS = 36
LOG2_E = float(np.log2(np.e))
ROW_TILE = 256
VMEM_LIMIT = 48 * 1024 * 1024
EXPERT_VMEM_LIMIT = 56 * 1024 * 1024

OFF_XA, OFF_YA, OFF_QB, OFF_FZF, OFF_FZB, OFF_IB, OFF_GB, OFF_QC = [i * D_MIX for i in range(8)]
OFF_KC = 8 * D_MIX
OFF_VC = OFF_KC + KV_DIM
OFF_GA = OFF_VC + KV_DIM
OFF_GB2 = OFF_GA + D_MODEL
OFF_GC = OFF_GB2 + D_MODEL
N_IN = OFF_GC + D_MODEL
IN_TN = 2 * KV_DIM


def _cparams(sem):
    return pltpu.CompilerParams(dimension_semantics=sem, vmem_limit_bytes=VMEM_LIMIT)


def _sig(x):
    return 0.5 * jnp.tanh(0.5 * x) + 0.5


def _silu(x):
    return x * _sig(x)


def _dot(a, b):
    return jnp.dot(a, b, preferred_element_type=F32)


def _dot_nt(a, b):
    return lax.dot_general(a, b, (((1,), (1,)), ((), ())), preferred_element_type=F32)


def _rms(x, g):
    ms = jnp.mean(x * x, axis=-1, keepdims=True)
    return x * lax.rsqrt(ms + EPS) * g


def _mod_row(tile_rows, n_pt):
    per = TILE // tile_rows if tile_rows <= TILE else None
    if per is None:
        mult = tile_rows // TILE
        return lambda i: jnp.maximum(i * mult - n_pt + 1, 0)
    return lambda i: jnp.maximum(i // per - n_pt + 1, 0)


def _ada_kernel(c_ref, w_ref, b_ref, o_ref):
    s = _silu(c_ref[...]).astype(BF16)
    o_ref[...] = _dot(s, w_ref[...].astype(BF16)) + b_ref[...]


def _ada_mod(cond, w_ada, b_ada):
    depth, _, n = w_ada.shape
    rows = cond.shape[0]
    tn = 1024
    return pl.pallas_call(
        _ada_kernel,
        out_shape=jax.ShapeDtypeStruct((depth, rows, n), F32),
        grid=(depth, n // tn),
        in_specs=[
            pl.BlockSpec((rows, D_MODEL), lambda l, j: (0, 0)),
            pl.BlockSpec((None, D_MODEL, tn), lambda l, j: (l, 0, j)),
            pl.BlockSpec((None, 1, tn), lambda l, j: (l, 0, j)),
        ],
        out_specs=pl.BlockSpec((None, rows, tn), lambda l, j: (l, 0, j)),
        compiler_params=_cparams(("parallel", "parallel")),
        name="ada_mod",
    )(cond, w_ada, b_ada.reshape(depth, 1, n))


def _norm_mod_kernel(p_steps, xp_ref, xs_ref, g_ref, sc_ref, sh_ref, x_ref, u_ref):
    i = pl.program_id(0)

    def emit(src_ref):
        x = src_ref[...]
        x_ref[...] = x
        u_ref[...] = (_rms(x, g_ref[...]) * (1.0 + sc_ref[...]) + sh_ref[...]).astype(BF16)

    @pl.when(i < p_steps)
    def _():
        emit(xp_ref)

    @pl.when(i >= p_steps)
    def _():
        emit(xs_ref)


def _mod_specs(mod_row, sub):
    mspec = lambda c: pl.BlockSpec((None, None, 1, D_MODEL), lambda i, *_: (mod_row(i), c, 0, 0))
    return mspec(3 * sub + 1), mspec(3 * sub)


def _norm_mod(x_prompt, x_sample, norm_g, mod, layer, n_pt):
    tm = 512
    p_steps = x_prompt.shape[0] // tm
    m = x_prompt.shape[0] + x_sample.shape[0]
    rows = pl.BlockSpec((tm, D_MODEL), lambda i: (i, 0))
    return pl.pallas_call(
        functools.partial(_norm_mod_kernel, p_steps),
        out_shape=(jax.ShapeDtypeStruct((m, D_MODEL), F32), jax.ShapeDtypeStruct((m, D_MODEL), BF16)),
        grid=(m // tm,),
        in_specs=[pl.BlockSpec((tm, D_MODEL), lambda i: (jnp.minimum(i, p_steps - 1), 0)),
                  pl.BlockSpec((tm, D_MODEL), lambda i: (jnp.maximum(i - p_steps, 0), 0)),
                  pl.BlockSpec((None, 1, D_MODEL), lambda i: (layer, 0, 0)),
                  *_mod_specs(_mod_row(tm, n_pt), 0)],
        out_specs=(rows, rows),
        compiler_params=_cparams(("parallel",)), name="norm_mod",
    )(x_prompt, x_sample, norm_g, mod, mod)


def _in_proj_kernel(kv_tile, u_ref, w_ref, o_ref, kv_ref):
    acc = _dot(u_ref[...], w_ref[...].astype(BF16))
    o_ref[...] = acc.astype(BF16)

    @pl.when(pl.program_id(1) == kv_tile)
    def _():
        kv_ref[...] = acc


def _in_proj(u, w_in, layer):
    m = u.shape[0]
    assert OFF_KC % IN_TN == 0 and N_IN % IN_TN == 0
    return pl.pallas_call(
        functools.partial(_in_proj_kernel, OFF_KC // IN_TN),
        out_shape=(jax.ShapeDtypeStruct((m, N_IN), BF16), jax.ShapeDtypeStruct((m, IN_TN), F32)),
        grid=(m // BIG_TILE, N_IN // IN_TN),
        in_specs=[
            pl.BlockSpec((BIG_TILE, D_MODEL), lambda i, j: (i, 0)),
            pl.BlockSpec((None, D_MODEL, IN_TN), lambda i, j: (layer, 0, j)),
        ],
        out_specs=(pl.BlockSpec((BIG_TILE, IN_TN), lambda i, j: (i, j)),
                   pl.BlockSpec((BIG_TILE, IN_TN), lambda i, j: (i, 0))),
        compiler_params=_cparams(("parallel", "arbitrary")),
        name="in_proj",
    )(u, w_in)


def _softplus(z):
    return jnp.maximum(z, 0.0) + jnp.log1p(jnp.exp(-jnp.abs(z)))


def _gelu_tanh(x):
    c = float(np.sqrt(2.0 / np.pi))
    return (0.5 * x) * (1.0 + jnp.tanh(x * (c + (c * 0.044715) * (x * x))))


def _mixa_kernel(n_pt, seq, xa_ref, ya_ref, cw_ref, cb_ref, wr_ref, br_ref, wi_ref, bi_ref, lam_ref, h0_ref,
                 y_ref, fin_ref, af_ref, bf_ref, ab_ref, bb_ref, hf_ref, hb_ref, xc_ref):
    i = pl.program_id(0)

    def body(seg_len, use_state):
        x = xa_ref[...].astype(F32)
        cw = cw_ref[...]
        taps = (pltpu.roll(x, 1, 0), x, pltpu.roll(x, TILE - 1, 0), pltpu.roll(x, TILE - 2, 0))
        conv = lambda t: t[0] * cw[0:1] + t[1] * cw[1:2] + t[2] * cw[2:3] + t[3] * cw[3:4] + cb_ref[...]
        xc_ref[...] = conv(taps)
        row8 = lax.broadcasted_iota(jnp.int32, (8, HEAD), 0)
        for s in range(TILE // seg_len):
            lo, hi = s * seg_len, (s + 1) * seg_len - 8
            first = [t[lo:lo + 8] for t in taps]
            first[0] = jnp.where(row8 >= 1, first[0], 0.0)
            xc_ref[lo:lo + 8, :] = conv(first)
            last = [t[hi:hi + 8] for t in taps]
            last[2] = jnp.where(row8 < 7, last[2], 0.0)
            last[3] = jnp.where(row8 < 6, last[3], 0.0)
            xc_ref[hi:hi + 8, :] = conv(last)
        xc = xc_ref[...]
        xcb = xc.astype(BF16)
        half_xc = 0.5 * xc
        for d, (a_ref, b_ref) in enumerate(((af_ref, bf_ref), (ab_ref, bb_ref))):
            t_r = jnp.tanh(_dot(xcb, (0.5 * wr_ref[d]).astype(BF16)) + 0.5 * br_ref[d])
            t_i = jnp.tanh(_dot(xcb, (0.5 * wi_ref[d]).astype(BF16)) + 0.5 * bi_ref[d])
            c4 = (0.5 * RG_C) * _softplus(-lam_ref[d])
            nla = c4 * t_r + c4
            a = jnp.exp2(nla * (-LOG2_E))
            a_ref[...] = a
            z = jnp.tanh(nla) * (a * a + 1.0)
            b_ref[...] = jnp.where(z > 0.0, z * lax.rsqrt(z), 0.0) * ((t_i + 1.0) * half_xc)

        nb = seg_len // 8
        rowi = lax.broadcasted_iota(jnp.int32, (8, HEAD), 0)
        n_seg = TILE // seg_len
        for s in range(n_seg):
            base = s * seg_len
            if use_state:
                c0 = (h0_ref[0:1, :], h0_ref[1:2, :])
            else:
                c0 = (jnp.zeros((1, HEAD), F32), jnp.zeros((1, HEAD), F32))

            def step(j, carry):
                cf, cb = carry
                of = pl.multiple_of(base + j * 8, 8)
                ob = pl.multiple_of(base + (nb - 1 - j) * 8, 8)
                a = af_ref[pl.ds(of, 8), :]
                b = bf_ref[pl.ds(of, 8), :]
                for d in (1, 2, 4):
                    m = rowi >= d
                    b = jnp.where(m, a * pltpu.roll(b, d, 0) + b, b)
                    a = jnp.where(m, a * pltpu.roll(a, d, 0), a)
                h = b + a * cf
                hf_ref[pl.ds(of, 8), :] = h
                cf = h[7:8, :]
                a = ab_ref[pl.ds(ob, 8), :]
                b = bb_ref[pl.ds(ob, 8), :]
                for d in (1, 2, 4):
                    m = rowi < 8 - d
                    b = jnp.where(m, a * pltpu.roll(b, 8 - d, 0) + b, b)
                    a = jnp.where(m, a * pltpu.roll(a, 8 - d, 0), a)
                h = b + a * cb
                hb_ref[pl.ds(ob, 8), :] = h
                cb = h[0:1, :]
                return cf, cb

            cf, cb = lax.fori_loop(0, nb, step, c0, unroll=4)
            fin_ref[s, 0:1, :] = cf
            fin_ref[s, 1:2, :] = cb
        for s in range(n_seg, TILE // seq):
            fin_ref[s] = jnp.zeros((2, HEAD), F32)
        y_ref[...] = ((hf_ref[...] + hb_ref[...]) * _gelu_tanh(ya_ref[...].astype(F32))).astype(BF16)

    @pl.when(i < n_pt)
    def _():
        body(seq, False)

    @pl.when(i >= n_pt)
    def _():
        body(TILE, True)


def _mixer_a(y_in, conv_w, conv_b, rg_wr, rg_br, rg_wi, rg_bi, rg_lambda, state, layer, n_pt, seq):
    m = y_in.shape[0]
    depth = conv_w.shape[0]
    nt = m // TILE
    spt = TILE // seq
    req = lambda i: jnp.maximum(i - n_pt, 0)
    vec = lambda a: a.reshape(depth, 2, 1, D_MIX)
    gate_w = pl.BlockSpec((None, 2, None, HEAD, HEAD), lambda i, h: (layer, 0, h, 0, 0))
    gate_b = pl.BlockSpec((None, 2, 1, HEAD), lambda i, h: (layer, 0, 0, h))
    slab = lambda off: pl.BlockSpec((TILE, HEAD), lambda i, h: (i, off // HEAD + h))
    return pl.pallas_call(
        functools.partial(_mixa_kernel, n_pt, seq),
        out_shape=(jax.ShapeDtypeStruct((m, D_MIX), BF16),
                   jax.ShapeDtypeStruct((nt, spt, 2, D_MIX), F32)),
        grid=(nt, N_HEADS),
        in_specs=[
            slab(OFF_XA), slab(OFF_YA),
            pl.BlockSpec((None, CONV_W, HEAD), lambda i, h: (layer, 0, h)),
            pl.BlockSpec((None, 1, HEAD), lambda i, h: (layer, 0, h)),
            gate_w, gate_b, gate_w, gate_b, gate_b,
            pl.BlockSpec((None, None, 2, HEAD), lambda i, h: (req(i), layer, 0, h)),
        ],
        out_specs=(pl.BlockSpec((TILE, HEAD), lambda i, h: (i, h)),
                   pl.BlockSpec((None, spt, 2, HEAD), lambda i, h: (i, 0, 0, h))),
        scratch_shapes=[pltpu.VMEM((TILE, HEAD), F32)] * 7,
        compiler_params=_cparams(("parallel", "parallel")),
        name="mixer_a",
    )(y_in, y_in, conv_w, conv_b.reshape(depth, 1, D_MIX), rg_wr, vec(rg_br), rg_wi, vec(rg_bi), vec(rg_lambda),
      state)


def _hgrn_matrices():
    t = np.arange(CHUNK)
    return np.stack([t[None, :] <= t[:, None], t[None, :] >= t[:, None]]).astype(np.float32)


def _hgrn_decays(g, tri, reverse):
    c = CHUNK
    g_cat = jnp.concatenate([g[n * c:(n + 1) * c] for n in range(HGRN_GROUP)], axis=1)
    gh = g_cat.astype(BF16)
    gl = (g_cat - gh.astype(F32)).astype(BF16)
    cum = _dot(tri, gh) + _dot(tri, gl)
    end = cum[c - 1:c] if not reverse else cum[0:1]
    out = [cum, end - cum]
    rowi = lax.broadcasted_iota(jnp.int32, g_cat.shape, 0)
    m = c // 2
    while m >= 4:
        mids = [blk * 2 * m + (m if reverse else m - 1) for blk in range(c // (2 * m))]
        mid = jnp.concatenate([jnp.broadcast_to(cum[r:r + 1], (2 * m, g_cat.shape[1])) for r in mids], axis=0)
        out.append(pltpu.bitcast(pltpu.bitcast(cum - mid, jnp.uint32) | jnp.uint32(0x80000000), F32))
        m //= 2
    g_next = pltpu.roll(g_cat, c - 1, 0)
    g_prev = pltpu.roll(g_cat, 1, 0)
    p = rowi & 3
    if not reverse:
        out.append(jnp.where(p == 0, g_next, jnp.where(p == 1, 0.0, jnp.where(p == 2, g_cat, g_cat + g_prev))))
        out.append(jnp.where((rowi & 1) != 0, g_cat, 0.0))
    else:
        out.append(jnp.where(p == 0, g_cat + g_next, jnp.where(p == 1, g_cat, jnp.where(p == 2, 0.0, g_prev))))
        out.append(jnp.where((rowi & 1) != 0, 0.0, g_cat))
    return out


def _hgrn_side_masks(reverse):
    rowi = lax.broadcasted_iota(jnp.int32, (CHUNK, HEAD), 0)
    masks = []
    for lvl in range(N_LEVELS):
        q_side = ((rowi & (CHUNK >> (lvl + 1))) != 0) != reverse
        on = jnp.where(q_side, 1.0, 0.0)
        masks.append((on.astype(BF16), (1.0 - on).astype(BF16)))
    return masks


def _hgrn_pair_masks():
    ti = lax.broadcasted_iota(jnp.int32, (CHUNK, CHUNK), 0)
    si = lax.broadcasted_iota(jnp.int32, (CHUNK, CHUNK), 1)
    return ti == si, [(ti ^ si) < (CHUNK >> lvl) for lvl in range(N_LEVELS)]


def _hgrn_chunk_local(q, k, v, decays, masks, pair_masks, reverse):
    c = CHUNK
    e_in = jnp.exp2(decays[0])
    e_out = jnp.exp2(decays[1])
    vb = v.astype(BF16)
    qb = q.astype(BF16)
    kb = k.astype(BF16)
    diag, same_block = pair_masks
    scores = None
    for lvl in range(N_LEVELS):
        eb = jnp.exp2(decays[2 + lvl]).astype(BF16)
        mq, mk = masks[lvl]
        part = _dot_nt(qb * eb * mq, kb * eb * mk)
        scores = part if scores is None else jnp.where(same_block[lvl], part, scores)
    scores = jnp.where(diag, _dot_nt(qb, kb), scores)
    ke = (k * e_out).astype(BF16)
    own = lax.dot_general(vb, ke, (((0,), (0,)), ((), ())), preferred_element_type=F32)
    d_end = e_in[c - 1:c] if not reverse else e_in[0:1]
    return (q * e_in).astype(BF16), scores.astype(BF16), vb, own, d_end


def _mixb_kernel(n_pt, seq, layer, qb_ref, fzf_ref, fzb_ref, ib_ref, gb_ref, lbl_ref, s0_ref, ng_ref, mm_ref,
                 y_ref, sfin_ref, of_ref, ob_ref):
    i = pl.program_id(0)
    depth = lbl_ref.shape[0]

    def lower_bound(d):
        lg = [lbl_ref[j, d] for j in range(depth)]
        mx = functools.reduce(jnp.maximum, lg)
        ex = [jnp.exp(v - mx) for v in lg]
        tot = functools.reduce(lambda a, b: a + b, ex)
        acc = jnp.zeros_like(tot)
        for j in range(1, layer + 1):
            acc = acc + ex[j] / tot
        return acc

    def body(seg_len, use_state):
        lbs = (lower_bound(0), lower_bound(1))
        n_grp = seg_len // HGRN_ROWS
        chunk_rows = lambda a, n: a[n * CHUNK:(n + 1) * CHUNK]

        def load(r0, d):
            rows = pl.ds(r0, HGRN_ROWS)
            q = _silu(qb_ref[rows, :].astype(F32))
            v = ib_ref[rows, :].astype(F32)
            fz = (fzf_ref if d == 0 else fzb_ref)[rows, :].astype(F32)
            f = lbs[d] + (1.0 - lbs[d]) / (1.0 + jnp.exp(-fz))
            return q, 1.0 - f, v, jnp.log2(f)

        def group(r_f, r_b, st_f, st_b):
            qf, kf, vf, gf = load(r_f, 0)
            qb, kb, vb, gb = load(r_b, 1)
            pair_masks = _hgrn_pair_masks()
            masks_f = _hgrn_side_masks(False)
            masks_b = _hgrn_side_masks(True)
            ef = _hgrn_decays(gf, mm_ref[0], False)
            eb = _hgrn_decays(gb, mm_ref[1], True)
            lanes = lambda e, n: [a[:, n * HEAD:(n + 1) * HEAD] for a in e]
            loc_f = [_hgrn_chunk_local(chunk_rows(qf, n), chunk_rows(kf, n), chunk_rows(vf, n),
                                       lanes(ef, n), masks_f, pair_masks, False) for n in range(HGRN_GROUP)]
            loc_b = [_hgrn_chunk_local(chunk_rows(qb, n), chunk_rows(kb, n), chunk_rows(vb, n),
                                       lanes(eb, n), masks_b, pair_masks, True) for n in range(HGRN_GROUP)]
            for n in range(HGRN_GROUP):
                qe, sc, vals, own, d_end = loc_f[n]
                of_ref[pl.ds(r_f + n * CHUNK, CHUNK), :] = _dot_nt(qe, st_f.astype(BF16)) + _dot(sc, vals)
                st_f = st_f * d_end + own
            for n in reversed(range(HGRN_GROUP)):
                qe, sc, vals, own, d_end = loc_b[n]
                ob_ref[pl.ds(r_b + n * CHUNK, CHUNK), :] = _dot_nt(qe, st_b.astype(BF16)) + _dot(sc, vals)
                st_b = st_b * d_end + own
            return st_f, st_b

        if use_state:
            def step(j, carry):
                r_f = pl.multiple_of(j * HGRN_ROWS, HGRN_ROWS)
                r_b = pl.multiple_of((n_grp - 1 - j) * HGRN_ROWS, HGRN_ROWS)
                return group(r_f, r_b, *carry)

            st_f, st_b = lax.fori_loop(0, n_grp, step, (s0_ref[0].T, s0_ref[1].T))
            sfin_ref[0, 0] = st_f.T
            sfin_ref[0, 1] = st_b.T
            for s in range(1, TILE // seq):
                sfin_ref[s] = jnp.zeros((2, HEAD, HEAD), F32)
        else:
            def request(s, carry):
                r0 = pl.multiple_of(s * HGRN_ROWS, HGRN_ROWS)
                zero = jnp.zeros((HEAD, HEAD), F32)
                st_f, st_b = group(r0, r0, zero, zero)
                sfin_ref[s, 0] = st_f.T
                sfin_ref[s, 1] = st_b.T
                return carry

            lax.fori_loop(0, TILE // seg_len, request, 0)
        o = _rms(of_ref[...] + ob_ref[...], ng_ref[...]) * _silu(gb_ref[...].astype(F32))
        y_ref[...] = o.astype(BF16)

    @pl.when(i < n_pt)
    def _():
        body(seq, False)

    @pl.when(i >= n_pt)
    def _():
        body(TILE, True)


def _mixer_b(y_in, lb_logits, norm_g, state, mmats, layer, n_pt, seq):
    m = y_in.shape[0]
    depth = lb_logits.shape[0]
    nt = m // TILE
    spt = TILE // seq
    req = lambda i: jnp.maximum(i - n_pt, 0)
    slab = lambda off: pl.BlockSpec((TILE, HEAD), lambda i, h: (i, off // HEAD + h))
    return pl.pallas_call(
        functools.partial(_mixb_kernel, n_pt, seq, layer),
        out_shape=(jax.ShapeDtypeStruct((m, D_MIX), BF16),
                   jax.ShapeDtypeStruct((nt, spt, 2, N_HEADS, HEAD, HEAD), F32)),
        grid=(nt, N_HEADS),
        in_specs=[
            slab(OFF_QB), slab(OFF_FZF), slab(OFF_FZB), slab(OFF_IB), slab(OFF_GB),
            pl.BlockSpec((depth, 2, 1, HEAD), lambda i, h: (0, 0, 0, h)),
            pl.BlockSpec((None, None, 2, None, HEAD, HEAD), lambda i, h: (req(i), layer, 0, h, 0, 0)),
            pl.BlockSpec((None, 1, HEAD), lambda i, h: (layer, 0, 0)),
            pl.BlockSpec(mmats.shape, lambda i, h: (0, 0, 0)),
        ],
        out_specs=(pl.BlockSpec((TILE, HEAD), lambda i, h: (i, h)),
                   pl.BlockSpec((None, spt, 2, None, HEAD, HEAD), lambda i, h: (i, 0, 0, h, 0, 0))),
        scratch_shapes=[pltpu.VMEM((TILE, HEAD), F32)] * 2,
        compiler_params=_cparams(("parallel", "parallel")),
        name="mixer_b",
    )(y_in, y_in, y_in, y_in, y_in, lb_logits.reshape(depth, 2, 1, D_MIX), state,
      norm_g.reshape(depth, 1, HEAD), mmats)


def _rope(x, cos, sin_lo, sin_hi):
    return x * cos + pltpu.roll(x, HEAD - ROPE_FREQS, 1) * sin_lo + pltpu.roll(x, ROPE_FREQS, 1) * sin_hi


def _attn_kernel(n_pt, seq, q_ref, k_ref, v_ref, ck_ref, cv_ref, cos_ref, slo_ref, shi_ref, qg_ref, kg_ref,
                 y_ref, kn_ref, qs_ref, ks_ref, vs_ref):
    i = pl.program_id(0)
    scale = HEAD ** -0.5

    def body(seg_len, ctx):
        kn = _rms(k_ref[...], kg_ref[...])
        kn_ref[...] = kn
        if ctx:
            kn = _rope(kn, cos_ref[...], slo_ref[...], shi_ref[...])
        ks_ref[...] = kn.astype(BF16)
        vs_ref[...] = v_ref[...].astype(BF16)
        for g in range(GROUP):
            qn = _rms(q_ref[:, g * HEAD:(g + 1) * HEAD].astype(F32), qg_ref[...])
            if ctx:
                qn = _rope(qn, cos_ref[...], slo_ref[...], shi_ref[...])
            qs_ref[g] = (qn * scale).astype(BF16)
        if ctx:
            ckb = ck_ref[...].astype(BF16)
            cvb = cv_ref[...].astype(BF16)
        for s in range(TILE // seg_len):
            base = s * seg_len

            q_par = min(Q_PAR, seg_len // Q_BLK)

            def qblocks(j, carry):
                keys = ks_ref[pl.ds(base, seg_len), :]
                vals = vs_ref[pl.ds(base, seg_len), :]
                rows = [pl.ds(pl.multiple_of(base + (j * q_par + t) * Q_BLK, Q_BLK), Q_BLK) for t in range(q_par)]
                qcat = [jnp.concatenate([qs_ref[g, r, :] for g in range(GROUP)], axis=0) for r in rows]
                sc = [_dot_nt(qc, keys) for qc in qcat]
                sc_c = [_dot_nt(qc, ckb) for qc in qcat] if ctx else None
                for t in range(q_par):
                    mx = jnp.max(sc[t], axis=-1, keepdims=True)
                    if ctx:
                        mx = jnp.maximum(mx, jnp.max(sc_c[t], axis=-1, keepdims=True))
                    p = jnp.exp(sc[t] - mx)
                    den = jnp.sum(p, axis=-1, keepdims=True)
                    o = _dot(p.astype(BF16), vals)
                    if ctx:
                        pc = jnp.exp(sc_c[t] - mx)
                        den = den + jnp.sum(pc, axis=-1, keepdims=True)
                        o = o + _dot(pc.astype(BF16), cvb)
                    o = o / den
                    for g in range(GROUP):
                        y_ref[rows[t], g * HEAD:(g + 1) * HEAD] = o[g * Q_BLK:(g + 1) * Q_BLK].astype(BF16)
                return carry

            lax.fori_loop(0, seg_len // (Q_BLK * q_par), qblocks, 0)

    @pl.when(i < n_pt)
    def _():
        body(seq, False)

    @pl.when(i >= n_pt)
    def _():
        body(TILE, True)


def _attention(y_in, kv, cache_k, cache_v, rope_tabs, q_norm_g, k_norm_g, layer, n_pt, seq):
    m = y_in.shape[0]
    depth = q_norm_g.shape[0]
    nt = m // TILE
    gw = GROUP * HEAD
    past = cache_k.shape[2]
    req = lambda i: jnp.maximum(i - n_pt, 0)
    ctx_spec = pl.BlockSpec((None, None, past, HEAD), lambda i, h: (req(i), layer, 0, h))
    tab = pl.BlockSpec((TILE, HEAD), lambda i, h: (0, 0))
    gspec = pl.BlockSpec((None, 1, HEAD), lambda i, h: (layer, 0, 0))
    return pl.pallas_call(
        functools.partial(_attn_kernel, n_pt, seq),
        out_shape=(jax.ShapeDtypeStruct((m, D_MIX), BF16),
                   jax.ShapeDtypeStruct((m, KV_DIM), F32)),
        grid=(nt, KV_HEADS),
        in_specs=[
            pl.BlockSpec((TILE, gw), lambda i, h: (i, OFF_QC // gw + h)),
            pl.BlockSpec((TILE, HEAD), lambda i, h: (i, h)),
            pl.BlockSpec((TILE, HEAD), lambda i, h: (i, KV_HEADS + h)),
            ctx_spec, ctx_spec, tab, tab, tab, gspec, gspec,
        ],
        out_specs=(pl.BlockSpec((TILE, gw), lambda i, h: (i, h)),
                   pl.BlockSpec((TILE, HEAD), lambda i, h: (i, h))),
        scratch_shapes=[pltpu.VMEM((GROUP, TILE, HEAD), BF16), pltpu.VMEM((TILE, HEAD), BF16),
                        pltpu.VMEM((TILE, HEAD), BF16)],
        compiler_params=_cparams(("parallel", "parallel")),
        name="attention",
    )(y_in, kv, kv, cache_k, cache_v, *rope_tabs, q_norm_g.reshape(depth, 1, HEAD),
      k_norm_g.reshape(depth, 1, HEAD))


def _rope_tables(n):
    rows = n // GRID_W
    r = jnp.repeat(jnp.arange(rows, dtype=F32), GRID_W)
    col = jnp.tile(jnp.arange(GRID_W, dtype=F32), rows)
    inv = ROPE_BASE ** (-jnp.arange(ROPE_FREQS, dtype=F32) / ROPE_FREQS)
    ar = r[:, None] * inv
    ac = col[:, None] * inv
    ang = jnp.concatenate([ar, ar, ac, ac], axis=-1)
    cos, sin = jnp.cos(ang), jnp.sin(ang)
    first_half = (jnp.arange(HEAD) // ROPE_FREQS) % 2 == 0
    return cos, jnp.where(first_half, -sin, 0.0), jnp.where(first_half, 0.0, sin)


def _merge_kernel(ya_ref, yb_ref, yc_ref, wa_ref, wb_ref, wc_ref, ga_ref, gb_ref, gc_ref, o_ref):
    acc = _sig(ga_ref[...].astype(F32)) * _dot(ya_ref[...], wa_ref[...])
    acc = acc + _sig(gb_ref[...].astype(F32)) * _dot(yb_ref[...], wb_ref[...])
    acc = acc + _sig(gc_ref[...].astype(F32)) * _dot(yc_ref[...], wc_ref[...])
    o_ref[...] = acc.astype(BF16)


def _merge(y_a, y_b, y_c, y_in, w_a, w_b, w_c, layer):
    m = y_in.shape[0]
    tn = 512
    act = pl.BlockSpec((TILE, D_MIX), lambda i, j: (i, 0))
    wsp = pl.BlockSpec((None, D_MIX, tn), lambda i, j: (layer, 0, j))
    gate = lambda off: pl.BlockSpec((TILE, tn), lambda i, j: (i, off // tn + j))
    return pl.pallas_call(
        _merge_kernel,
        out_shape=jax.ShapeDtypeStruct((m, D_MODEL), BF16),
        grid=(m // TILE, D_MODEL // tn),
        in_specs=[act, act, act, wsp, wsp, wsp, gate(OFF_GA), gate(OFF_GB2), gate(OFF_GC)],
        out_specs=pl.BlockSpec((TILE, tn), lambda i, j: (i, j)),
        compiler_params=_cparams(("parallel", "parallel")),
        name="merge",
    )(y_a, y_b, y_c, w_a, w_b, w_c, y_in, y_in, y_in)


def _out_proj_kernel(mg_ref, w_ref, x_ref, g1_ref, ng_ref, sc_ref, sh_ref, wr_ref, xo_ref, u_ref, lg_ref):
    xn = x_ref[...] + g1_ref[...] * _dot(mg_ref[...], w_ref[...])
    xo_ref[...] = xn
    u = _rms(xn, ng_ref[...]) * (1.0 + sc_ref[...]) + sh_ref[...]
    u_ref[...] = u.astype(BF16)
    w = wr_ref[...]
    uh = u.astype(BF16)
    ul = (u - uh.astype(F32)).astype(BF16)
    wh = w.astype(BF16)
    wl = (w - wh.astype(F32)).astype(BF16)
    lg_ref[...] = _dot(uh, wh) + (_dot(uh, wl) + _dot(ul, wh))


def _out_proj(merged, w_out_bf16, x, mod, norm_g, w_router_pad, layer, n_pt):
    m = x.shape[0]
    tm = ROW_TILE
    row = _mod_row(tm, n_pt)
    return pl.pallas_call(
        _out_proj_kernel,
        out_shape=(jax.ShapeDtypeStruct((m, D_MODEL), F32),
                   jax.ShapeDtypeStruct((m, D_MODEL), BF16),
                   jax.ShapeDtypeStruct((m, E_PAD), F32)),
        grid=(m // tm,),
        in_specs=[
            pl.BlockSpec((tm, D_MODEL), lambda i: (i, 0)),
            pl.BlockSpec((None, D_MODEL, D_MODEL), lambda i: (layer, 0, 0)),
            pl.BlockSpec((tm, D_MODEL), lambda i: (i, 0)),
            pl.BlockSpec((None, None, 1, D_MODEL), lambda i: (row(i), 2, 0, 0)),
            pl.BlockSpec((None, 1, D_MODEL), lambda i: (layer, 0, 0)),
            *_mod_specs(row, 1),
            pl.BlockSpec((None, D_MODEL, E_PAD), lambda i: (layer, 0, 0)),
        ],
        out_specs=(pl.BlockSpec((tm, D_MODEL), lambda i: (i, 0)),
                   pl.BlockSpec((tm, D_MODEL), lambda i: (i, 0)),
                   pl.BlockSpec((tm, E_PAD), lambda i: (i, 0))),
        compiler_params=_cparams(("parallel",)),
        name="out_proj",
    )(merged, w_out_bf16, x, mod, norm_g, mod, mod, w_router_pad)


def _prefix_matrices(seq):
    t = np.arange(TILE)
    before = t[:, None] < t[None, :]
    same = (t[:, None] // seq) == (t[None, :] // seq)
    return np.stack([before & same, before]).astype(np.float32)


def _route_kernel(n_pt, seq, lg_ref, u_ref, tri_ref, expand_ref, xin_ref, pt_ref, gate_ref, p_ref):
    i = pl.program_id(0)

    def body(seg_len):
        n_seg = TILE // seg_len
        cap = EC_FACTOR * seg_len // N_EXPERTS
        lt = lg_ref[...].T[0:N_EXPERTS]
        ex = jnp.exp(lt - jnp.max(lt, axis=0, keepdims=True))
        aff = ex / jnp.sum(ex, axis=0, keepdims=True)
        segs = [aff[:, s * seg_len:(s + 1) * seg_len] for s in range(n_seg)]

        def count_ge(seg, thr):
            return jnp.sum(jnp.where(seg >= thr, 1.0, 0.0), axis=1, keepdims=True)

        def bisect(_, carry):
            out = []
            for seg, (lo, hi) in zip(segs, carry):
                mid = jnp.where(lo > 0.0, jnp.sqrt(lo) * jnp.sqrt(hi), AFF_TINY)
                ok = count_ge(seg, mid) >= cap
                out.append((jnp.where(ok, mid, lo), jnp.where(ok, hi, mid)))
            return tuple(out)

        init = tuple((jnp.zeros((N_EXPERTS, 1), F32), jnp.full((N_EXPERTS, 1), 2.0, F32)) for _ in segs)
        bounds = lax.fori_loop(0, BISECT_STEPS, bisect, init)
        gt_l, eq_l, need_l = [], [], []
        for seg, (_, hi) in zip(segs, bounds):
            thr = jnp.max(jnp.where(seg < hi, seg, -1.0), axis=1, keepdims=True)
            gt = seg > thr
            gt_l.append(jnp.where(gt, 1.0, 0.0))
            eq_l.append(jnp.where(seg == thr, 1.0, 0.0))
            n_gt = jnp.sum(gt_l[-1], axis=1, keepdims=True)
            need_l.append(jnp.broadcast_to(cap - n_gt, (N_EXPERTS, seg_len)))
        gt = jnp.concatenate(gt_l, axis=1)
        eq = jnp.concatenate(eq_l, axis=1)
        need = jnp.concatenate(need_l, axis=1)
        before = _dot(jnp.concatenate([gt, eq], axis=0).astype(BF16), tri_ref[...])
        gt_before, eq_before = before[0:N_EXPERTS], before[N_EXPERTS:]
        chosen = jnp.logical_or(gt > 0.0, jnp.logical_and(eq > 0.0, eq_before < need))
        tok = lax.broadcasted_iota(jnp.int32, (N_EXPERTS, TILE), 1)
        seg_off = ((tok // seg_len) * cap).astype(F32)
        slot = jnp.where(chosen, gt_before + jnp.minimum(eq_before, need) + seg_off, -1.0)
        slot_t = jnp.concatenate([slot, jnp.full((E_PAD - N_EXPERTS, TILE), -1.0, F32)], axis=0).T
        sub = lax.broadcasted_iota(jnp.int32, (SLOTS, TILE), 0).astype(F32)
        lane = lax.broadcasted_iota(jnp.int32, (TILE, SLOTS), 1).astype(F32)
        for e in range(N_EXPERTS):
            hit = slot[e:e + 1] == sub
            gate_ref[e] = jnp.sum(jnp.where(hit, aff[e:e + 1], 0.0), axis=1, keepdims=True)
            p_ref[e * SLOTS:(e + 1) * SLOTS, :] = jnp.where(hit, 1.0, 0.0).astype(BF16)
            if n_seg == 1:
                hit_t = slot_t[:, e:e + 1] == lane
                pt_ref[:, e * SLOTS:(e + 1) * SLOTS] = jnp.where(hit_t, 1.0, 0.0).astype(BF16)
        if n_seg > 1:
            grp = N_EXPERTS * cap
            slot_in_req = (lax.broadcasted_iota(jnp.int32, (seg_len, grp), 1) % cap).astype(F32)
            zeros = jnp.zeros((seg_len, grp), BF16)
            for s in range(n_seg):
                mine = slot_t[s * seg_len:(s + 1) * seg_len].astype(BF16)
                per_col = _dot(mine, expand_ref[...]) - float(s * cap)
                hit_t = jnp.where(per_col == slot_in_req, 1.0, 0.0).astype(BF16)
                pt_ref[s * seg_len:(s + 1) * seg_len, :] = jnp.concatenate(
                    [hit_t if c == s else zeros for c in range(n_seg)], axis=1)
        tn = 512
        if n_seg == 1:
            for n in range(D_MODEL // tn):
                rows = _dot(p_ref[...], u_ref[:, n * tn:(n + 1) * tn])
                xin_ref[:, :, n * tn:(n + 1) * tn] = rows.astype(BF16).reshape(N_EXPERTS, SLOTS, tn)
        else:
            for s in range(n_seg):
                tok = slice(s * seg_len, (s + 1) * seg_len)
                sel = jnp.concatenate([p_ref[e * SLOTS + s * cap:e * SLOTS + (s + 1) * cap, tok]
                                       for e in range(N_EXPERTS)], axis=0)
                for n in range(D_MODEL // tn):
                    rows = _dot(sel, u_ref[tok, n * tn:(n + 1) * tn]).astype(BF16)
                    for e in range(N_EXPERTS):
                        xin_ref[e, s * cap:(s + 1) * cap, n * tn:(n + 1) * tn] = rows[e * cap:(e + 1) * cap]

    @pl.when(i < n_pt)
    def _():
        body(seq)

    @pl.when(i >= n_pt)
    def _():
        body(TILE)


def _expert_expansion(seq):
    cap = EC_FACTOR * seq // N_EXPERTS
    mat = np.zeros((E_PAD, N_EXPERTS * cap), np.float32)
    for e in range(N_EXPERTS):
        mat[e, e * cap:(e + 1) * cap] = 1.0
    return mat


def _route(logits, u, tri, expand, n_pt, seq):
    m = u.shape[0]
    nt = m // TILE
    return pl.pallas_call(
        functools.partial(_route_kernel, n_pt, seq),
        out_shape=(jax.ShapeDtypeStruct((N_EXPERTS, nt * SLOTS, D_MODEL), BF16),
                   jax.ShapeDtypeStruct((m, N_EXPERTS * SLOTS), BF16),
                   jax.ShapeDtypeStruct((N_EXPERTS, nt * SLOTS, 1), F32)),
        grid=(nt,),
        in_specs=[pl.BlockSpec((TILE, E_PAD), lambda i: (i, 0)),
                  pl.BlockSpec((TILE, D_MODEL), lambda i: (i, 0)),
                  pl.BlockSpec((None, TILE, TILE), lambda i: (jnp.where(i < n_pt, 0, 1), 0, 0)),
                  pl.BlockSpec(expand.shape, lambda i: (0, 0))],
        out_specs=(pl.BlockSpec((N_EXPERTS, SLOTS, D_MODEL), lambda i: (0, i, 0)),
                   pl.BlockSpec((TILE, N_EXPERTS * SLOTS), lambda i: (i, 0)),
                   pl.BlockSpec((N_EXPERTS, SLOTS, 1), lambda i: (0, i, 0))),
        scratch_shapes=[pltpu.VMEM((N_EXPERTS * SLOTS, TILE), BF16)],
        compiler_params=_cparams(("parallel",)),
        name="route",
    )(logits, u, tri, expand)


def _expert_kernel(n_f, x_ref, wg_ref, wu_ref, wd_ref, gate_ref, o_ref, h_ref):
    s = pl.program_id(1)
    tf = wg_ref.shape[-1]

    @pl.when(s < n_f)
    def _():
        x = x_ref[...]
        h = _silu(_dot(x, wg_ref[...].astype(BF16))) * _dot(x, wu_ref[...].astype(BF16))
        h_ref[s] = h.astype(BF16)

    @pl.when(s >= n_f)
    def _():
        acc = _dot(h_ref[0], wd_ref[0:tf, :].astype(BF16))
        for k in range(1, n_f):
            acc = acc + _dot(h_ref[k], wd_ref[k * tf:(k + 1) * tf, :].astype(BF16))
        o_ref[...] = (acc * gate_ref[...]).astype(BF16)


def _experts(xin, gates, w_gate, w_up, w_down, layer):
    _, rows, _ = xin.shape
    tf, tn = 256, 1024
    n_f, n_o = D_EXPERT // tf, D_MODEL // tn
    f_idx = lambda s: jnp.minimum(s, n_f - 1)
    o_idx = lambda s: jnp.maximum(s - n_f, 0)
    x_idx = lambda e, s: jnp.minimum(e + jnp.where(s >= n_f, 1, 0), N_EXPERTS - 1)
    return pl.pallas_call(
        functools.partial(_expert_kernel, n_f),
        out_shape=jax.ShapeDtypeStruct((N_EXPERTS, rows, D_MODEL), BF16),
        grid=(N_EXPERTS, n_f + n_o),
        in_specs=[
            pl.BlockSpec((None, rows, D_MODEL), lambda e, s: (x_idx(e, s), 0, 0)),
            pl.BlockSpec((None, None, D_MODEL, tf), lambda e, s: (layer, e, 0, f_idx(s))),
            pl.BlockSpec((None, None, D_MODEL, tf), lambda e, s: (layer, e, 0, f_idx(s))),
            pl.BlockSpec((None, None, D_EXPERT, tn), lambda e, s: (layer, e, 0, o_idx(s))),
            pl.BlockSpec((None, rows, 1), lambda e, s: (e, 0, 0)),
        ],
        out_specs=pl.BlockSpec((None, rows, tn), lambda e, s: (e, 0, o_idx(s))),
        scratch_shapes=[pltpu.VMEM((n_f, rows, tf), BF16)],
        compiler_params=pltpu.CompilerParams(dimension_semantics=("parallel", "arbitrary"),
                                             vmem_limit_bytes=EXPERT_VMEM_LIMIT),
        name="experts",
    )(xin, w_gate, w_up, w_down, gates)


def _scatter_steps(first, p_steps, seq, pt_ref, y_ref, x_ref, g2_ref, emit):
    i = pl.program_id(0) + first
    per_tile = TILE // seq
    cap = EC_FACTOR * seq // N_EXPERTS
    grp = N_EXPERTS * cap

    @pl.when(i >= p_steps)
    def _():
        emit(x_ref[...] + g2_ref[...] * _dot(pt_ref[...], y_ref[...].reshape(N_EXPERTS * SLOTS, D_MODEL)))

    for s in range(per_tile):
        @pl.when(jnp.logical_and(i < p_steps, i % per_tile == s))
        def _():
            mine = jnp.concatenate([y_ref[e, s * cap:(s + 1) * cap, :] for e in range(N_EXPERTS)], axis=0)
            emit(x_ref[...] + g2_ref[...] * _dot(pt_ref[:, s * grp:(s + 1) * grp], mine))


def _scatter_kernel(first, p_steps, seq, pt_ref, y_ref, x_ref, g2_ref, ng_ref, sc_ref, sh_ref, xo_ref, u_ref):
    def emit(xn):
        xo_ref[...] = xn
        u_ref[...] = (_rms(xn, ng_ref[...]) * (1.0 + sc_ref[...]) + sh_ref[...]).astype(BF16)

    _scatter_steps(first, p_steps, seq, pt_ref, y_ref, x_ref, g2_ref, emit)


def _scatter_final_kernel(first, p_steps, seq, pt_ref, y_ref, x_ref, g2_ref, ng_ref, o_ref):
    def emit(xn):
        o_ref[...] = _rms(xn, ng_ref[...])

    _scatter_steps(first, p_steps, seq, pt_ref, y_ref, x_ref, g2_ref, emit)


def _scatter(pt, y, x, mod, n_pt, seq, next_norm=None, final_g=None):
    m = x.shape[0]
    tm = ROW_TILE
    assert tm == seq
    row = _mod_row(tm, n_pt)
    p_steps = n_pt * (TILE // tm)

    def in_specs(first):
        return [
            pl.BlockSpec((tm, N_EXPERTS * SLOTS), lambda i: (i + first, 0)),
            pl.BlockSpec((N_EXPERTS, SLOTS, D_MODEL), lambda i: (0, (i + first) // (TILE // tm), 0)),
            pl.BlockSpec((tm, D_MODEL), lambda i: (i + first, 0)),
            pl.BlockSpec((None, None, 1, D_MODEL), lambda i: (row(i + first), 5, 0, 0)),
        ]

    rows = pl.BlockSpec((tm, D_MODEL), lambda i: (i, 0))
    if next_norm is None:
        def part(first, steps):
            return pl.pallas_call(
                functools.partial(_scatter_final_kernel, first, p_steps, seq),
                out_shape=jax.ShapeDtypeStruct((steps * tm, D_MODEL), F32),
                grid=(steps,),
                in_specs=in_specs(first) + [pl.BlockSpec((1, D_MODEL), lambda i: (0, 0))],
                out_specs=rows,
                compiler_params=_cparams(("parallel",)),
                name="scatter_final",
            )(pt, y, x, mod, final_g.reshape(1, D_MODEL))

        return part(0, p_steps), part(p_steps, m // tm - p_steps)
    norm_g, next_mod, next_layer = next_norm
    return pl.pallas_call(
        functools.partial(_scatter_kernel, 0, p_steps, seq),
        out_shape=(jax.ShapeDtypeStruct((m, D_MODEL), F32), jax.ShapeDtypeStruct((m, D_MODEL), BF16)),
        grid=(m // tm,),
        in_specs=in_specs(0) + [pl.BlockSpec((None, 1, D_MODEL), lambda i: (next_layer, 0, 0)),
                                *_mod_specs(row, 0)],
        out_specs=(rows, rows),
        compiler_params=_cparams(("parallel",)),
        name="scatter",
    )(pt, y, x, mod, norm_g, next_mod, next_mod)


def kernel(x_prompt, x_sample, cache_k, cache_v, state_rglru, state_hgrn, c, c_ctx, w_ada, b_ada, norm1_g, norm2_g, w_in, conv_w, conv_b, rg_wr, rg_br, rg_wi, rg_bi, rg_lambda, hgrn_lb_logits, hgrn_norm_g, q_norm_g, k_norm_g, w_branch_a, w_branch_b, w_branch_c, w_out, w_router, w_exp_gate, w_exp_up, w_exp_down, final_norm_g):
    bp, seq, _ = x_prompt.shape
    bs, dec_seq, _ = x_sample.shape
    depth = w_in.shape[0]
    past = cache_k.shape[2]
    m = bp * seq + bs * dec_seq
    assert dec_seq == TILE and TILE % seq == 0 and (bp * seq) % TILE == 0 and seq & (seq - 1) == 0
    assert seq % (2 * Q_BLK) == 0 and TILE % (Q_BLK * Q_PAR) == 0 and past == seq and m % BIG_TILE == 0 and seq == HGRN_ROWS
    n_pt = bp * seq // TILE

    n_cond = -(-(1 + bs) // 8) * 8
    cond = jnp.concatenate([c_ctx[None], c, jnp.zeros((n_cond - 1 - bs, D_MODEL), F32)], axis=0)
    mod_all = _ada_mod(cond, w_ada, b_ada).reshape(depth, n_cond, N_SUB, 1, D_MODEL)
    rope_tabs = _rope_tables(dec_seq)
    mmats = jnp.asarray(_hgrn_matrices(), BF16)
    tri = jnp.asarray(_prefix_matrices(seq), BF16)
    expand = jnp.asarray(_expert_expansion(seq), BF16)
    w_router_pad = jnp.pad(w_router, ((0, 0), (0, 0), (0, E_PAD - N_EXPERTS)))
    ck = cache_k.reshape(bs, depth, past, KV_DIM)
    cv = cache_v.reshape(bs, depth, past, KV_DIM)
    n1 = norm1_g.reshape(depth, 1, D_MODEL)
    n2 = norm2_g.reshape(depth, 1, D_MODEL)
    w_out_bf16 = w_out.astype(BF16)
    w_br = [w.astype(BF16) for w in (w_branch_a, w_branch_b, w_branch_c)]

    k_list, v_list, ra_list, hb_list = [], [], [], []
    x, u1 = _norm_mod(x_prompt.reshape(bp * seq, D_MODEL), x_sample.reshape(bs * dec_seq, D_MODEL),
                      n1, mod_all[0], 0, n_pt)
    for l in range(depth):
        mod = mod_all[l]
        y_in, kv = _in_proj(u1, w_in, l)
        y_a, fin_a = _mixer_a(y_in, conv_w, conv_b, rg_wr, rg_br, rg_wi, rg_bi, rg_lambda, state_rglru, l, n_pt, seq)
        y_b, fin_b = _mixer_b(y_in, hgrn_lb_logits, hgrn_norm_g, state_hgrn, mmats, l, n_pt, seq)
        y_c, k_n = _attention(y_in, kv, ck, cv, rope_tabs, q_norm_g, k_norm_g, l, n_pt, seq)
        merged = _merge(y_a, y_b, y_c, y_in, *w_br, l)
        x, u2, logits = _out_proj(merged, w_out_bf16, x, mod, n2, w_router_pad, l, n_pt)
        xin, pt, gates = _route(logits, u2, tri, expand, n_pt, seq)
        y_e = _experts(xin, gates, w_exp_gate, w_exp_up, w_exp_down, l)
        if l + 1 < depth:
            x, u1 = _scatter(pt, y_e, x, mod, n_pt, seq, next_norm=(n1, mod_all[l + 1], l + 1))
        else:
            y_prompt, y_sample = _scatter(pt, y_e, x, mod, n_pt, seq, final_g=final_norm_g)
        k_list.append(k_n[:bp * seq].reshape(bp, seq, KV_HEADS, HEAD))
        v_list.append(kv[:bp * seq, KV_DIM:].reshape(bp, seq, KV_HEADS, HEAD))
        ra_list.append(fin_a[:n_pt].reshape(bp, 2, D_MIX))
        hb_list.append(fin_b[:n_pt].reshape(bp, 2, N_HEADS, HEAD, HEAD))

    y_prompt = y_prompt.reshape(bp, seq, D_MODEL)
    y_sample = y_sample.reshape(bs, dec_seq, D_MODEL)
    return (y_prompt, y_sample, jnp.stack(k_list, axis=1), jnp.stack(v_list, axis=1),
            jnp.stack(ra_list, axis=1), jnp.stack(hb_list, axis=1))
```
